```python
import jax, jax.numpy as jnp
from jax import lax
import numpy as np

D_MODEL = 1024
BATCH = 1
SEQ = 16384
DEPTH = 2
DEC_BATCH = 32
DEC_SEQ = 4
PAST_LEN = 16384
PAGE_SIZE = 128

HEAD_DIM = 64
NSA_HEADS = 8
NSA_KV_HEADS = 2
NSA_GROUP = NSA_HEADS // NSA_KV_HEADS
NSA_WIDTH = NSA_HEADS * HEAD_DIM
NSA_KV_COLS = 2 * NSA_KV_HEADS * HEAD_DIM
CMP_BLOCK = 32
SEL_BLOCK = 64
N_SELECT = 16
WINDOW = 512
Q_BLOCK = 128
RET_HEADS = 4
RET_DK = 64
RET_DV = 128
RET_WIDTH = RET_HEADS * RET_DV
RET_CHUNK = 128
CONV_WIDTH = 31
ROPE_THETA = 10000.0
N_EVEN = (DEPTH + 1) // 2
N_ODD = DEPTH // 2
EVEN_SPLITS = (NSA_WIDTH, NSA_KV_COLS, NSA_KV_COLS, NSA_KV_COLS, 3 * NSA_HEADS, NSA_WIDTH,
               RET_HEADS * RET_DK, RET_HEADS * RET_DK, RET_WIDTH, RET_WIDTH)
EVEN_IN = sum(EVEN_SPLITS)
EVEN_MIX = NSA_WIDTH + RET_WIDTH
ODD_IN = 3 * D_MODEL
NEG_INF = -1e30
FORCE = 1e9
EPS = 1e-6

kernel_name = 'nsa_retention_conformer_hybrid_step'


def _split(h, sizes):
    return jnp.split(h, np.cumsum(sizes)[:-1].tolist(), axis=-1)


def rms_norm(x, g):
    xf = x.astype(jnp.float32)
    y = xf * lax.rsqrt(jnp.mean(xf * xf, -1, keepdims=True) + EPS)
    return (y * g.astype(jnp.float32)).astype(x.dtype)


def layer_norm(x, g, b):
    xf = x.astype(jnp.float32)
    mu = jnp.mean(xf, -1, keepdims=True)
    var = jnp.mean(jnp.square(xf - mu), -1, keepdims=True)
    return ((xf - mu) * lax.rsqrt(var + EPS) * g.astype(jnp.float32) + b.astype(jnp.float32)).astype(x.dtype)


def head_group_norm(o):
    of = o.astype(jnp.float32)
    mu = jnp.mean(of, -1, keepdims=True)
    var = jnp.mean(jnp.square(of - mu), -1, keepdims=True)
    return ((of - mu) * lax.rsqrt(var + EPS)).astype(o.dtype)


def rope(x, pos):
    half = x.shape[-1] // 2
    freq = ROPE_THETA ** (-jnp.arange(half, dtype=jnp.float32) / half)
    ang = pos.astype(jnp.float32)[:, None] * freq[None, :]
    cos = jnp.cos(ang)[:, None, :]
    sin = jnp.sin(ang)[:, None, :]
    xf = x.astype(jnp.float32)
    x1, x2 = xf[..., :half], xf[..., half:]
    return jnp.concatenate([x1 * cos - x2 * sin, x2 * cos + x1 * sin], -1).astype(x.dtype)


def masked_softmax(s, mask):
    p = jax.nn.softmax(jnp.where(mask, s, NEG_INF), axis=-1)
    return jnp.where(mask, p, 0.0)


def nsa_project(p_q, p_kc, p_ks, p_kw, p_g, pos, q_g, k_g):
    B, T, _ = p_q.shape
    q = rope(rms_norm(p_q.reshape(B, T, NSA_HEADS, HEAD_DIM), q_g), pos)

    def kv_rows(p, g):
        p = p.reshape(B, T, 2, NSA_KV_HEADS, HEAD_DIM)
        k = rope(rms_norm(p[:, :, 0], g), pos)
        return jnp.stack([k, p[:, :, 1]], axis=2)

    gates = jax.nn.sigmoid(p_g.astype(jnp.float32)).astype(p_g.dtype).reshape(B, T, NSA_HEADS, 3)
    return q, kv_rows(p_kc, k_g[0]), kv_rows(p_ks, k_g[1]), kv_rows(p_kw, k_g[2]), gates


def compress(rows, w_pos):
    B, L = rows.shape[:2]
    n_cb = L // CMP_BLOCK
    blk = rows[:, :n_cb * CMP_BLOCK].reshape(B, n_cb, CMP_BLOCK, 2, NSA_KV_HEADS, HEAD_DIM)
    return jnp.einsum('bnpckd,cpk->bnckd', blk, w_pos)


def rows_gather(rows, pos):
    b = jnp.arange(rows.shape[0])[:, None, None, None]
    h = jnp.arange(NSA_KV_HEADS)[None, None, :, None]
    return rows[b, pos, 0, h], rows[b, pos, 1, h]


def paged_gather(pool, li, page_table, new_rows, pos):
    past = page_table.shape[1] * PAGE_SIZE
    b = jnp.arange(page_table.shape[0])[:, None, None, None]
    h = jnp.arange(NSA_KV_HEADS)[None, None, :, None]
    pp = jnp.minimum(pos, past - 1)
    phys = page_table[b, pp // PAGE_SIZE]
    slot = pp % PAGE_SIZE
    npos = jnp.clip(pos - past, 0, new_rows.shape[1] - 1)
    in_past = (pos < past)[..., None]
    k = jnp.where(in_past, pool[li, phys, slot, 0, h], new_rows[b, npos, 0, h])
    v = jnp.where(in_past, pool[li, phys, slot, 1, h], new_rows[b, npos, 1, h])
    return k, v


def nsa_block(q, g, q_pos, cmp, sel_gather, win_kv, win_pos, total_len):
    B, Tq = q.shape[:2]
    scale = HEAD_DIM ** -0.5
    qg = q.reshape(B, Tq, NSA_KV_HEADS, NSA_GROUP, HEAD_DIM)
    gg = g.reshape(B, Tq, NSA_KV_HEADS, NSA_GROUP, 3)
    n_cb = cmp.shape[1]
    cmp_end = (jnp.arange(n_cb) + 1) * CMP_BLOCK - 1
    m_cmp = (cmp_end[None, :] <= q_pos[:, None])[None, :, None, None, :]
    s = jnp.einsum('btkgd,bnkd->btkgn', qg, cmp[:, :, 0]).astype(jnp.float32) * scale
    p_cmp = masked_softmax(s, m_cmp)
    o_cmp = jnp.einsum('btkgn,bnkd->btkgd', p_cmp.astype(q.dtype), cmp[:, :, 1])
    ratio = SEL_BLOCK // CMP_BLOCK
    n_sb = -(-total_len // SEL_BLOCK)
    imp = jnp.pad(jnp.sum(p_cmp, axis=3), ((0, 0), (0, 0), (0, 0), (0, n_sb * ratio - n_cb)))
    imp = imp.reshape(B, Tq, NSA_KV_HEADS, n_sb, ratio).sum(-1)
    sb = jnp.arange(n_sb)
    forced = (sb[None, :] == (q_pos // SEL_BLOCK)[:, None]) | (sb[None, :] == 0)
    future = sb[None, :] * SEL_BLOCK > q_pos[:, None]
    score = jnp.where(forced[None, :, None, :], FORCE, imp)
    score = jnp.where(future[None, :, None, :], -FORCE, score)
    _, idx = lax.top_k(score, min(N_SELECT, n_sb))
    sel_pos = (idx[..., None] * SEL_BLOCK + jnp.arange(SEL_BLOCK)).reshape(B, Tq, NSA_KV_HEADS, -1)
    k_sel, v_sel = sel_gather(sel_pos)
    s = jnp.einsum('btkgd,btknd->btkgn', qg, k_sel).astype(jnp.float32) * scale
    p_sel = masked_softmax(s, (sel_pos <= q_pos[None, :, None, None])[:, :, :, None, :])
    o_sel = jnp.einsum('btkgn,btknd->btkgd', p_sel.astype(q.dtype), v_sel)
    m_win = ((win_pos[None, :] <= q_pos[:, None]) & (win_pos[None, :] > q_pos[:, None] - WINDOW)
             & (win_pos[None, :] >= 0))
    s = jnp.einsum('btkgd,bnkd->btkgn', qg, win_kv[:, :, 0]).astype(jnp.float32) * scale
    p_win = masked_softmax(s, m_win[None, :, None, None, :])
    o_win = jnp.einsum('btkgn,bnkd->btkgd', p_win.astype(q.dtype), win_kv[:, :, 1])
    o = gg[..., 0:1] * o_cmp + gg[..., 1:2] * o_sel + gg[..., 2:3] * o_win
    return o.reshape(B, Tq, NSA_HEADS, HEAD_DIM)


def nsa_prompt(q, g, kv_cmp, kv_sel, kv_win, w_pos):
    B, T = q.shape[:2]
    cmp = compress(kv_cmp, w_pos)
    nb = T // Q_BLOCK
    win_pad = jnp.pad(kv_win, ((0, 0), (WINDOW, 0), (0, 0), (0, 0), (0, 0)))
    qb = q.reshape(B, nb, Q_BLOCK, NSA_HEADS, HEAD_DIM).swapaxes(0, 1)
    gb = g.reshape(B, nb, Q_BLOCK, NSA_HEADS, 3).swapaxes(0, 1)

    def one_block(args):
        qi, gi, i = args
        start = i * Q_BLOCK
        q_pos = start + jnp.arange(Q_BLOCK)
        wkv = lax.dynamic_slice_in_dim(win_pad, start, WINDOW + Q_BLOCK, axis=1)
        wpos = start - WINDOW + jnp.arange(WINDOW + Q_BLOCK)
        return nsa_block(qi, gi, q_pos, cmp, lambda p: rows_gather(kv_sel, p), wkv, wpos, T)

    out = lax.map(one_block, (qb, gb, jnp.arange(nb)))
    return out.swapaxes(0, 1).reshape(B, T, NSA_HEADS, HEAD_DIM)


def nsa_sample(q, g, kv_cmp, kv_sel, kv_win, cache_cmp_kv, cache_sel_kv, win_buf, page_table, li, w_pos):
    B, T = q.shape[:2]
    past = page_table.shape[1] * PAGE_SIZE
    past_cmp = cache_cmp_kv[li, page_table].reshape(B, past, 2, NSA_KV_HEADS, HEAD_DIM)
    cmp = compress(jnp.concatenate([past_cmp, kv_cmp], axis=1), w_pos)
    n_buf = win_buf.shape[1]
    win_all = jnp.concatenate([win_buf, kv_win], axis=1)
    wpos = past - n_buf + jnp.arange(n_buf + T)
    q_pos = past + jnp.arange(T)
    o = nsa_block(q, g, q_pos, cmp,
                  lambda p: paged_gather(cache_sel_kv, li, page_table, kv_sel, p),
                  win_all, wpos, past + T)
    return o, win_all[:, -n_buf:]


def retention_chunk(S, q, k, v, log_g):
    C = q.shape[1]
    i = jnp.arange(C, dtype=jnp.float32)
    rel = i[:, None] - i[None, :]
    decay = jnp.where(rel >= 0, jnp.exp(log_g[:, None, None] * jnp.maximum(rel, 0.0)), 0.0).astype(q.dtype)
    a = jnp.einsum('bihd,bjhd->bhij', q, k) * decay
    o = jnp.einsum('bhij,bjhe->bihe', a, v)
    q_dec = jnp.exp(log_g[None, :] * (i[:, None] + 1.0)).astype(q.dtype)
    o = o + jnp.einsum('bihd,bhde->bihe', q * q_dec[None, :, :, None], S)
    k_dec = jnp.exp(log_g[None, :] * (C - 1.0 - i[:, None])).astype(q.dtype)
    S = (jnp.exp(log_g * C).astype(S.dtype)[None, :, None, None] * S
         + jnp.einsum('bjhd,bjhe->bhde', k * k_dec[None, :, :, None], v))
    return S, o


def retention_prompt(q, k, v, log_g):
    B, T = q.shape[:2]
    nc = T // RET_CHUNK

    def to_chunks(a):
        return a.reshape(B, nc, RET_CHUNK, *a.shape[2:]).swapaxes(0, 1)

    S0 = jnp.zeros((B, RET_HEADS, RET_DK, RET_DV), q.dtype)
    S, o = lax.scan(lambda S, xs: retention_chunk(S, xs[0], xs[1], xs[2], log_g), S0,
                    (to_chunks(q), to_chunks(k), to_chunks(v)))
    return o.swapaxes(0, 1).reshape(B, T, RET_HEADS, RET_DV), S


def even_layer(x, pos, li, norm_even, w_in_even, q_norm_g, k_norm_g, w_cmp_pos, w_out_even, past):
    B, T, _ = x.shape
    h = rms_norm(x, norm_even[li]) @ w_in_even[li]
    p_q, p_kc, p_ks, p_kw, p_g, z_nsa, r_q, r_k, r_v, z_ret = _split(h, EVEN_SPLITS)
    q, kv_cmp, kv_sel, kv_win, gates = nsa_project(p_q, p_kc, p_ks, p_kw, p_g, pos, q_norm_g[li], k_norm_g[li])
    rq = rope(r_q.reshape(B, T, RET_HEADS, RET_DK), pos)
    rk = rope(r_k.reshape(B, T, RET_HEADS, RET_DK), pos) * (RET_DK ** -0.5)
    rv = r_v.reshape(B, T, RET_HEADS, RET_DV)
    log_g = jnp.log1p(-jnp.exp2(-5.0 - jnp.arange(RET_HEADS, dtype=jnp.float32)))
    if past is None:
        o_nsa = nsa_prompt(q, gates, kv_cmp, kv_sel, kv_win, w_cmp_pos[li])
        win_new = kv_win[:, -min(WINDOW, T):]
        o_ret, S = retention_prompt(rq, rk, rv, log_g)
    else:
        cache_cmp_kv, cache_sel_kv, cache_win_kv, state_ret, page_table = past
        o_nsa, win_new = nsa_sample(q, gates, kv_cmp, kv_sel, kv_win, cache_cmp_kv, cache_sel_kv,
                                    cache_win_kv[li], page_table, li, w_cmp_pos[li])
        S, o_ret = retention_chunk(state_ret[li], rq, rk, rv, log_g)
    mixed = jnp.concatenate([
        o_nsa.reshape(B, T, NSA_WIDTH) * jax.nn.silu(z_nsa),
        head_group_norm(o_ret).reshape(B, T, RET_WIDTH) * jax.nn.silu(z_ret)], axis=-1)
    return x + mixed @ w_out_even[li], (kv_cmp, kv_sel, win_new, S)


def odd_layer(x, li, norm_odd, w_in_odd, conv_w, conv_b, ln_g, ln_b, w_out_odd, conv_state):
    h = rms_norm(x, norm_odd[li]) @ w_in_odd[li]
    a, b, z = _split(h, (D_MODEL, D_MODEL, D_MODEL))
    u = a * jax.nn.sigmoid(b)
    if conv_state is None:
        ext = jnp.pad(u, ((0, 0), (CONV_WIDTH - 1, 0), (0, 0)))
    else:
        ext = jnp.concatenate([conv_state[li], u], axis=1)
    c = lax.conv_general_dilated(ext, conv_w[li][:, None, :], (1,), 'VALID',
                                 dimension_numbers=('NWC', 'WIO', 'NWC'),
                                 feature_group_count=D_MODEL) + conv_b[li]
    y = jax.nn.silu(layer_norm(c, ln_g[li], ln_b[li])) * jax.nn.silu(z)
    return x + y @ w_out_odd[li], ext[:, -(CONV_WIDTH - 1):]


def setup_inputs(seed: int = 0) -> dict:
    key = jax.random.key(seed)
    ks = jax.random.split(key, 24)
    n_pages = PAST_LEN // PAGE_SIZE
    n_pool = (5 * DEC_BATCH * n_pages + 3) // 4
    win_buf = min(WINDOW, PAST_LEN)

    def nrm(k, shape, s=1.0):
        return s * jax.random.normal(k, shape, jnp.float32)

    page_table = jax.random.permutation(ks[0], n_pool)[:DEC_BATCH * n_pages].reshape(DEC_BATCH, n_pages).astype(jnp.int32)
    return {
        'x_prompt': nrm(ks[1], (BATCH, SEQ, D_MODEL)),
        'x_sample': nrm(ks[2], (DEC_BATCH, DEC_SEQ, D_MODEL)),
        'cache_cmp_kv': nrm(ks[3], (N_EVEN, n_pool, PAGE_SIZE, 2, NSA_KV_HEADS, HEAD_DIM)),
        'cache_sel_kv': nrm(ks[4], (N_EVEN, n_pool, PAGE_SIZE, 2, NSA_KV_HEADS, HEAD_DIM)),
        'cache_win_kv': nrm(ks[5], (N_EVEN, DEC_BATCH, win_buf, 2, NSA_KV_HEADS, HEAD_DIM)),
        'state_ret': nrm(ks[6], (N_EVEN, DEC_BATCH, RET_HEADS, RET_DK, RET_DV), 0.1),
        'state_conv': nrm(ks[7], (N_ODD, DEC_BATCH, CONV_WIDTH - 1, D_MODEL)),
        'page_table': page_table,
        'norm_even': 1.0 + nrm(ks[8], (N_EVEN, D_MODEL), 0.02),
        'w_in_even': nrm(ks[9], (N_EVEN, D_MODEL, EVEN_IN), D_MODEL ** -0.5),
        'q_norm_g': 1.0 + nrm(ks[10], (N_EVEN, HEAD_DIM), 0.02),
        'k_norm_g': 1.0 + nrm(ks[11], (N_EVEN, 3, HEAD_DIM), 0.02),
        'w_cmp_pos': (1.0 + nrm(ks[12], (N_EVEN, 2, CMP_BLOCK, NSA_KV_HEADS), 0.1)) / CMP_BLOCK,
        'w_out_even': nrm(ks[13], (N_EVEN, EVEN_MIX, D_MODEL), EVEN_MIX ** -0.5),
        'norm_odd': 1.0 + nrm(ks[14], (N_ODD, D_MODEL), 0.02),
        'w_in_odd': nrm(ks[15], (N_ODD, D_MODEL, ODD_IN), D_MODEL ** -0.5),
        'conv_w': nrm(ks[16], (N_ODD, CONV_WIDTH, D_MODEL), CONV_WIDTH ** -0.5),
        'conv_b': nrm(ks[17], (N_ODD, D_MODEL), 0.01),
        'ln_g': 1.0 + nrm(ks[18], (N_ODD, D_MODEL), 0.02),
        'ln_b': nrm(ks[19], (N_ODD, D_MODEL), 0.01),
        'w_out_odd': nrm(ks[20], (N_ODD, D_MODEL, D_MODEL), D_MODEL ** -0.5),
    }


def reference(x_prompt, x_sample, cache_cmp_kv, cache_sel_kv, cache_win_kv, state_ret, state_conv, page_table,
              norm_even, w_in_even, q_norm_g, k_norm_g, w_cmp_pos, w_out_even,
              norm_odd, w_in_odd, conv_w, conv_b, ln_g, ln_b, w_out_odd):
    past = page_table.shape[1] * PAGE_SIZE
    pos_p = jnp.arange(x_prompt.shape[1])
    pos_s = past + jnp.arange(x_sample.shape[1])
    yp, ys = x_prompt, x_sample
    even_p, even_s, conv_p, conv_s = [], [], [], []
    for i in range(DEPTH):
        li = i // 2
        if i % 2 == 0:
            yp, st = even_layer(yp, pos_p, li, norm_even, w_in_even, q_norm_g, k_norm_g, w_cmp_pos, w_out_even, None)
            even_p.append(st)
            ys, st = even_layer(ys, pos_s, li, norm_even, w_in_even, q_norm_g, k_norm_g, w_cmp_pos, w_out_even,
                                (cache_cmp_kv, cache_sel_kv, cache_win_kv, state_ret, page_table))
            even_s.append(st)
        else:
            yp, st = odd_layer(yp, li, norm_odd, w_in_odd, conv_w, conv_b, ln_g, ln_b, w_out_odd, None)
            conv_p.append(st)
            ys, st = odd_layer(ys, li, norm_odd, w_in_odd, conv_w, conv_b, ln_g, ln_b, w_out_odd, state_conv)
            conv_s.append(st)

    def stack(sts, j):
        return jnp.stack([s[j] for s in sts])

    return (yp, ys,
            stack(even_p, 0), stack(even_s, 0),
            stack(even_p, 1), stack(even_s, 1),
            stack(even_p, 2), stack(even_s, 2),
            stack(even_p, 3), stack(even_s, 3),
            jnp.stack(conv_p), jnp.stack(conv_s))
```

```python
import functools

import numpy as np
import jax
import jax.numpy as jnp
from jax import lax
from jax.experimental import pallas as pl
from jax.experimental.pallas import tpu as pltpu

F32 = jnp.float32
BF16 = jnp.bfloat16

D_MODEL = 1024
PAGE_SIZE = 128
HEAD_DIM = 64
NSA_HEADS = 8
NSA_KV_HEADS = 2
NSA_GROUP = 4
CMP_BLOCK = 32
SEL_BLOCK = 64
N_SELECT = 16
WINDOW = 512
RET_HEADS = 4
RET_DK = 64
RET_DV = 128
RET_CHUNK = 128
CONV_WIDTH = 31
ROPE_THETA = 10000.0
NEG_INF = -1e30
FORCE = 1e9
EPS = 1e-6

LANES = 128
VMEM_LIMIT = 56 * 1024 * 1024

_QP = 0
_KV = 1024
_ZN = 1792
_RQ = 2304
_RK = 2560
_RV = 2816
_ZR = 3328
_GT = 3840
_EVEN_COLS = 3968


def _cparams(sem):
    return pltpu.CompilerParams(dimension_semantics=sem, vmem_limit_bytes=VMEM_LIMIT)


def _sigmoid(x):
    return 1.0 / (1.0 + jnp.exp(-x))


def _nt_dot(a, b):
    return lax.dot_general(a, b, (((1,), (1,)), ((), ())), preferred_element_type=F32)


def _dot(a, b):
    return jnp.dot(a, b, preferred_element_type=F32)


def _full(shape):
    n = len(shape)
    return pl.BlockSpec(shape, lambda *_: (0,) * n)


def _even_in_kernel(x_ref, g_ref, w_ref, cos_ref, sin_ref, gq_ref, gk_ref,
                    q_ref, kvc_ref, kvs_ref, kvw_ref, k2s_ref, k2w_ref, vts_ref, vtw_ref,
                    gate_ref, sz_ref, rq_ref, rk_ref, rv_ref, szr_ref):
    x = x_ref[...]
    ms = jnp.mean(x * x, axis=-1, keepdims=True)
    xn = (x * lax.rsqrt(ms + EPS) * g_ref[...]).astype(BF16)
    cos = cos_ref[...]
    sin = sin_ref[...]
    lane = lax.broadcasted_iota(jnp.int32, (1, LANES), 1)
    lo_half = (lane & 32) == 0
    gr = lax.broadcasted_iota(jnp.int32, (LANES, LANES), 0) // HEAD_DIM
    gc = lax.broadcasted_iota(jnp.int32, (LANES, LANES), 1) // HEAD_DIM
    gmat = (gr == gc).astype(BF16)

    def proj(a, b):
        return _dot(xn, w_ref[:, a:b])

    def head_rms(h, gain):
        h2 = h * h
        hi = h2.astype(BF16)
        lo = (h2 - hi.astype(F32)).astype(BF16)
        ss = _dot(hi, gmat) + _dot(lo, gmat)
        return h * lax.rsqrt(ss * (1.0 / HEAD_DIM) + EPS) * gain

    def rope(h):
        sw = jnp.where(lo_half, pltpu.roll(h, LANES - 32, 1), pltpu.roll(h, 32, 1))
        return h * cos + sw * sin

    hq = proj(_QP, _KV)
    gq = gq_ref[...]
    for c in range(NSA_HEADS):
        hc = hq[:, c * LANES:(c + 1) * LANES]
        q_ref[:, c * LANES:(c + 1) * LANES] = (rope(head_rms(hc, gq)) * (HEAD_DIM ** -0.5)).astype(BF16)

    hk = proj(_KV, _ZN)
    kv_refs = (kvc_ref, kvs_ref, kvw_ref)
    k2_refs = (None, k2s_ref, k2w_ref)
    vt_refs = (None, vts_ref, vtw_ref)
    for b in range(3):
        k = rope(head_rms(hk[:, 2 * b * LANES:(2 * b + 1) * LANES], gk_ref[b:b + 1, :]))
        v = hk[:, (2 * b + 1) * LANES:(2 * b + 2) * LANES]
        kv_refs[b][:, 0:LANES] = k
        kv_refs[b][:, LANES:2 * LANES] = v
        if k2_refs[b] is not None:
            k2_refs[b][...] = k.astype(BF16)
            vt_refs[b][...] = v.T.astype(BF16)

    hz = proj(_ZN, _RQ)
    sz_ref[...] = hz * _sigmoid(hz)
    hrq = proj(_RQ, _RK)
    hrk = proj(_RK, _RV)
    for c in range(2):
        rq_ref[:, c * LANES:(c + 1) * LANES] = rope(hrq[:, c * LANES:(c + 1) * LANES])
        rk_ref[:, c * LANES:(c + 1) * LANES] = rope(hrk[:, c * LANES:(c + 1) * LANES]) * (RET_DK ** -0.5)
    rv_ref[...] = proj(_RV, _ZR)
    hzr = proj(_ZR, _GT)
    szr_ref[...] = hzr * _sigmoid(hzr)
    gate_ref[...] = _sigmoid(proj(_GT, _EVEN_COLS))


def _even_in(x, norm_g, w_e, cos, sin, gq, gk, tm):
    t = x.shape[0]
    row = lambda n: pl.BlockSpec((tm, n), lambda i: (i, 0))
    out_shape = (
        jax.ShapeDtypeStruct((t, 1024), BF16),
        jax.ShapeDtypeStruct((t, 256), F32),
        jax.ShapeDtypeStruct((t, 256), F32),
        jax.ShapeDtypeStruct((t, 256), F32),
        jax.ShapeDtypeStruct((t, 128), BF16),
        jax.ShapeDtypeStruct((t, 128), BF16),
        jax.ShapeDtypeStruct((128, t), BF16),
        jax.ShapeDtypeStruct((128, t), BF16),
        jax.ShapeDtypeStruct((t, 128), F32),
        jax.ShapeDtypeStruct((t, 512), F32),
        jax.ShapeDtypeStruct((t, 256), F32),
        jax.ShapeDtypeStruct((t, 256), F32),
        jax.ShapeDtypeStruct((t, 512), F32),
        jax.ShapeDtypeStruct((t, 512), F32),
    )
    vt_spec = pl.BlockSpec((128, tm), lambda i: (0, i))
    out_specs = (row(1024), row(256), row(256), row(256), row(128), row(128), vt_spec, vt_spec,
                 row(128), row(512), row(256), row(256), row(512), row(512))
    return pl.pallas_call(
        _even_in_kernel,
        grid=(t // tm,),
        in_specs=[row(D_MODEL), _full((1, D_MODEL)), _full((D_MODEL, _EVEN_COLS)),
                  row(128), row(128), _full((1, 128)), _full((3, 128))],
        out_specs=out_specs,
        out_shape=out_shape,
        compiler_params=_cparams(("parallel",)),
        name="even_in",
    )(x, norm_g, w_e, cos, sin, gq, gk)


def _compress_kernel(pt_ref, *refs, pp):
    del pt_ref
    page_refs = refs[:pp]
    w_ref, oe_ref, oo_ref = refs[pp:]
    w = w_ref[...]
    for p in range(pp):
        x = page_refs[p][0]
        s = jnp.sum(x.reshape(PAGE_SIZE // CMP_BLOCK, CMP_BLOCK, 256) * w[None], axis=1)
        oe_ref[0, 2 * p:2 * p + 1, :] = s[0:1]
        oo_ref[0, 2 * p:2 * p + 1, :] = s[1:2]
        oe_ref[0, 2 * p + 1:2 * p + 2, :] = s[2:3]
        oo_ref[0, 2 * p + 1:2 * p + 2, :] = s[3:4]


def _compress(pool, page_table, w2, pp):
    nb, n_pages = page_table.shape

    def page_spec(p):
        return pl.BlockSpec((1, PAGE_SIZE, 256), lambda b, g, pt: (pt[b * n_pages + g * pp + p], 0, 0))

    out_spec = pl.BlockSpec((1, 2 * pp, 256), lambda b, g, pt: (b, g, 0))
    grid_spec = pltpu.PrefetchScalarGridSpec(
        num_scalar_prefetch=1,
        grid=(nb, n_pages // pp),
        in_specs=[page_spec(p) for p in range(pp)] + [pl.BlockSpec((CMP_BLOCK, 256), lambda b, g, pt: (0, 0))],
        out_specs=(out_spec, out_spec),
    )
    shp = jax.ShapeDtypeStruct((nb, 2 * n_pages, 256), F32)
    return pl.pallas_call(
        functools.partial(_compress_kernel, pp=pp),
        grid_spec=grid_spec,
        out_shape=(shp, shp),
        compiler_params=_cparams(("parallel", "parallel")),
        name="compress",
    )(page_table.reshape(-1), *([pool] * pp), w2)


def _stack_heads(q, k):
    return jnp.concatenate([q[:, (NSA_GROUP * k + g) * LANES:(NSA_GROUP * k + g + 1) * LANES]
                            for g in range(NSA_GROUP)], axis=0)


def _store_heads(o_ref, og, k, tq):
    for j in range(2):
        blk = jnp.concatenate([og[2 * j], og[2 * j + 1]], axis=0)
        o_ref[:, (2 * k + j) * LANES:(2 * k + j + 1) * LANES] = blk.T


def _cmp_topk_p_kernel(q_ref, kc_ref, vtc_ref, g_ref, o_ref, sel_ref, *, tq, n_half):
    i = pl.program_id(0)
    qpos = i * tq + lax.broadcasted_iota(jnp.int32, (1, tq), 1)
    r = lax.broadcasted_iota(jnp.int32, (2 * n_half, 1), 0)
    blk = jnp.where(r < n_half, 2 * r, 2 * (r - n_half) + 1)
    mask = ((blk + 1) * CMP_BLOCK - 1) <= qpos
    sb = lax.broadcasted_iota(jnp.int32, (n_half, 1), 0)
    sbf = sb.astype(F32)
    forced = (sb == qpos // SEL_BLOCK) | (sb == 0)
    future = sb * SEL_BLOCK > qpos
    q = q_ref[...]
    kc = kc_ref[...]
    gates = g_ref[...]
    for k in range(NSA_KV_HEADS):
        s = _nt_dot(kc, _stack_heads(q, k))
        imp = jnp.zeros((2 * n_half, tq), F32)
        ps = []
        for g in range(NSA_GROUP):
            sg = jnp.where(mask, s[:, g * tq:(g + 1) * tq], NEG_INF)
            e = jnp.exp(sg - jnp.max(sg, axis=0, keepdims=True))
            p = jnp.where(mask, e / jnp.sum(e, axis=0, keepdims=True), 0.0)
            imp = imp + p
            ps.append(p.astype(BF16))
        ot = _dot(vtc_ref[HEAD_DIM * k:HEAD_DIM * (k + 1), :], jnp.concatenate(ps, axis=1))
        og = [ot[:, g * tq:(g + 1) * tq] * gates[NSA_GROUP * k + g:NSA_GROUP * k + g + 1, :]
              for g in range(NSA_GROUP)]
        _store_heads(o_ref, og, k, tq)

        score = jnp.where(forced, FORCE, imp[:n_half] + imp[n_half:])
        score = jnp.where(future, -FORCE, score)

        def pick(_, carry):
            sc, chosen = carry
            mx = jnp.max(sc, axis=0, keepdims=True)
            first = jnp.min(jnp.where(sc == mx, sbf, 1e9), axis=0, keepdims=True)
            hit = sbf == first
            return jnp.where(hit, -jnp.inf, sc), jnp.where(hit, 1.0, chosen)

        _, chosen = lax.fori_loop(0, N_SELECT, pick, (score, jnp.zeros((n_half, tq), F32)))
        sel_ref[0, k] = chosen


def _cmp_topk_p(q, kc, vtc, gates0, tq):
    t = q.shape[0]
    n_half = kc.shape[0] // 2
    nq = t // tq
    return pl.pallas_call(
        functools.partial(_cmp_topk_p_kernel, tq=tq, n_half=n_half),
        grid=(nq,),
        in_specs=[pl.BlockSpec((tq, 1024), lambda i: (i, 0)), _full(kc.shape), _full(vtc.shape),
                  pl.BlockSpec((NSA_HEADS, tq), lambda i: (0, i))],
        out_specs=(pl.BlockSpec((tq, 512), lambda i: (i, 0)),
                   pl.BlockSpec((1, NSA_KV_HEADS, n_half, tq), lambda i: (i, 0, 0, 0))),
        out_shape=(jax.ShapeDtypeStruct((t, 512), F32),
                   jax.ShapeDtypeStruct((nq, NSA_KV_HEADS, n_half, tq), F32)),
        compiler_params=_cparams(("parallel",)),
        name="cmp_topk_prompt",
    )(q, kc, vtc, gates0)


def _sel_p_kernel(q_ref, k2_ref, vt_ref, sel_ref, g_ref, o_ref, acc_ref, m_ref, l_ref, *, tq, tk):
    i = pl.program_id(0)
    qpos = i * tq + lax.broadcasted_iota(jnp.int32, (1, tq), 1)
    q = q_ref[...]
    qs = [_stack_heads(q, k) for k in range(NSA_KV_HEADS)]
    nblk = tk // SEL_BLOCK
    kiota = lax.broadcasted_iota(jnp.int32, (SEL_BLOCK, 1), 0)
    m_ref[...] = jnp.full(m_ref.shape, NEG_INF, F32)
    l_ref[...] = jnp.zeros(l_ref.shape, F32)
    acc_ref[...] = jnp.zeros(acc_ref.shape, F32)
    n_tiles = (i * tq + tq + tk - 1) // tk

    def body(kt, carry):
        start = pl.multiple_of(kt * tk, tk)
        ktile = k2_ref[pl.ds(start, tk), :]
        vtile = vt_ref[:, pl.ds(start, tk)]
        for k in range(NSA_KV_HEADS):
            selb = sel_ref[0, k, pl.ds(pl.multiple_of(kt * nblk, nblk), nblk), :]
            s = _nt_dot(ktile, qs[k])
            strips = []
            for j in range(nblk):
                ok = (selb[j:j + 1, :] > 0.5) & ((start + j * SEL_BLOCK + kiota) <= qpos)
                bias = jnp.where(ok, 0.0, NEG_INF)
                strips.append(s[j * SEL_BLOCK:(j + 1) * SEL_BLOCK, :] + jnp.concatenate([bias] * NSA_GROUP, axis=1))
            sm = jnp.concatenate(strips, axis=0)
            m_old = m_ref[k]
            m_new = jnp.maximum(m_old, jnp.max(sm, axis=0, keepdims=True))
            alpha = jnp.exp(m_old - m_new)
            p = jnp.exp(sm - m_new)
            l_ref[k] = alpha * l_ref[k] + jnp.sum(p, axis=0, keepdims=True)
            acc_ref[k] = alpha * acc_ref[k] + _dot(vtile[HEAD_DIM * k:HEAD_DIM * (k + 1), :], p.astype(BF16))
            m_ref[k] = m_new
        return carry

    lax.fori_loop(0, n_tiles, body, 0)
    gates = g_ref[...]
    for k in range(NSA_KV_HEADS):
        ot = acc_ref[k] / l_ref[k]
        og = [ot[:, g * tq:(g + 1) * tq] * gates[NSA_GROUP * k + g:NSA_GROUP * k + g + 1, :]
              for g in range(NSA_GROUP)]
        _store_heads(o_ref, og, k, tq)


def _sel_p(q, k2, vt, sel, gates1, tq, tk):
    t = q.shape[0]
    n_half = sel.shape[2]
    return pl.pallas_call(
        functools.partial(_sel_p_kernel, tq=tq, tk=tk),
        grid=(t // tq,),
        in_specs=[pl.BlockSpec((tq, 1024), lambda i: (i, 0)), _full(k2.shape), _full(vt.shape),
                  pl.BlockSpec((1, NSA_KV_HEADS, n_half, tq), lambda i: (i, 0, 0, 0)),
                  pl.BlockSpec((NSA_HEADS, tq), lambda i: (0, i))],
        out_specs=pl.BlockSpec((tq, 512), lambda i: (i, 0)),
        out_shape=jax.ShapeDtypeStruct((t, 512), F32),
        scratch_shapes=[pltpu.VMEM((NSA_KV_HEADS, HEAD_DIM, NSA_GROUP * tq), F32),
                        pltpu.VMEM((NSA_KV_HEADS, 1, NSA_GROUP * tq), F32),
                        pltpu.VMEM((NSA_KV_HEADS, 1, NSA_GROUP * tq), F32)],
        compiler_params=_cparams(("parallel",)),
        name="sel_prompt",
    )(q, k2, vt, sel, gates1)


def _win_p_kernel(q_ref, k2_ref, vt_ref, g_ref, o_ref, *, tq):
    i = pl.program_id(0)
    nk = WINDOW + tq
    qpos = i * tq + lax.broadcasted_iota(jnp.int32, (1, tq), 1)
    start = pl.multiple_of(jnp.maximum(i * tq - WINDOW, 0), tq)
    kpos = start + lax.broadcasted_iota(jnp.int32, (nk, 1), 0)
    ok = (kpos <= qpos) & (kpos > qpos - WINDOW)
    bias = jnp.where(ok, 0.0, NEG_INF)
    q = q_ref[...]
    ktile = k2_ref[pl.ds(start, nk), :]
    vtile = vt_ref[:, pl.ds(start, nk)]
    gates = g_ref[...]
    for k in range(NSA_KV_HEADS):
        s = _nt_dot(ktile, _stack_heads(q, k)) + jnp.concatenate([bias] * NSA_GROUP, axis=1)
        p = jnp.exp(s - jnp.max(s, axis=0, keepdims=True))
        l = jnp.sum(p, axis=0, keepdims=True)
        ot = _dot(vtile[HEAD_DIM * k:HEAD_DIM * (k + 1), :], p.astype(BF16)) / l
        og = [ot[:, g * tq:(g + 1) * tq] * gates[NSA_GROUP * k + g:NSA_GROUP * k + g + 1, :]
              for g in range(NSA_GROUP)]
        _store_heads(o_ref, og, k, tq)


def _win_p(q, k2, vt, gates2, tq):
    t = q.shape[0]
    return pl.pallas_call(
        functools.partial(_win_p_kernel, tq=tq),
        grid=(t // tq,),
        in_specs=[pl.BlockSpec((tq, 1024), lambda i: (i, 0)), _full(k2.shape), _full(vt.shape),
                  pl.BlockSpec((NSA_HEADS, tq), lambda i: (0, i))],
        out_specs=pl.BlockSpec((tq, 512), lambda i: (i, 0)),
        out_shape=jax.ShapeDtypeStruct((t, 512), F32),
        compiler_params=_cparams(("parallel",)),
        name="win_prompt",
    )(q, k2, vt, gates2)


def _cmp_topk_s_kernel(q_ref, kc_ref, vc_ref, g_ref, o_ref, sel_ref, *, tq, past, n_half, n_sb, n_cols):
    qs = q_ref[0]
    nrow = qs.shape[0]
    s = _nt_dot(qs, kc_ref[0])
    qpos = past + lax.broadcasted_iota(jnp.int32, (nrow, 1), 0) % tq
    col = lax.broadcasted_iota(jnp.int32, (1, 2 * n_half), 1)
    blk = jnp.where(col < n_half, 2 * col, 2 * (col - n_half) + 1)
    mask = ((blk + 1) * CMP_BLOCK - 1) <= qpos
    sm = jnp.where(mask, s, NEG_INF)
    e = jnp.exp(sm - jnp.max(sm, axis=-1, keepdims=True))
    p = jnp.where(mask, e / jnp.sum(e, axis=-1, keepdims=True), 0.0)
    o_ref[0] = _dot(p.astype(BF16), vc_ref[0]) * g_ref[0][:, 0:1]

    nr8 = NSA_KV_HEADS * tq
    imp = p[0:nr8]
    for g in range(1, NSA_GROUP):
        imp = imp + p[g * nr8:(g + 1) * nr8]
    imp = jnp.concatenate([imp[:, :n_half] + imp[:, n_half:], jnp.zeros((nr8, n_cols - n_half), F32)], axis=1)
    qpos8 = past + lax.broadcasted_iota(jnp.int32, (nr8, 1), 0) % tq
    sb = lax.broadcasted_iota(jnp.int32, (1, n_cols), 1)
    sbf = sb.astype(F32)
    score = jnp.where((sb == qpos8 // SEL_BLOCK) | (sb == 0), FORCE, imp)
    score = jnp.where(sb * SEL_BLOCK > qpos8, -FORCE, score)
    score = jnp.where(sb >= n_sb, -jnp.inf, score)

    def pick(_, carry):
        sc, chosen = carry
        mx = jnp.max(sc, axis=-1, keepdims=True)
        first = jnp.min(jnp.where(sc == mx, sbf, 1e9), axis=-1, keepdims=True)
        hit = sbf == first
        return jnp.where(hit, -jnp.inf, sc), jnp.where(hit, 1.0, chosen)

    _, chosen = lax.fori_loop(0, min(N_SELECT, n_sb), pick, (score, jnp.zeros((nr8, n_cols), F32)))
    sel_ref[0] = chosen


def _cmp_topk_s(qs, kc, vc, gates, tq, past):
    nb, nrow, _ = qs.shape
    n_half = kc.shape[1] // 2
    n_sb = -(-(past + tq) // SEL_BLOCK)
    n_cols = -(-n_sb // LANES) * LANES
    b3 = lambda a: pl.BlockSpec((1,) + a.shape[1:], lambda b: (b, 0, 0))
    return pl.pallas_call(
        functools.partial(_cmp_topk_s_kernel, tq=tq, past=past, n_half=n_half, n_sb=n_sb, n_cols=n_cols),
        grid=(nb,),
        in_specs=[b3(qs), b3(kc), b3(vc), b3(gates)],
        out_specs=(pl.BlockSpec((1, nrow, 128), lambda b: (b, 0, 0)),
                   pl.BlockSpec((1, NSA_KV_HEADS * tq, n_cols), lambda b: (b, 0, 0))),
        out_shape=(jax.ShapeDtypeStruct((nb, nrow, 128), F32),
                   jax.ShapeDtypeStruct((nb, NSA_KV_HEADS * tq, n_cols), F32)),
        compiler_params=_cparams(("parallel",)),
        name="cmp_topk_sample",
    )(qs, kc, vc, gates)


def _selwin_s_kernel(pt_ref, q_ref, sel_ref, g_ref, *refs, pp, tq, past):
    del pt_ref
    page_refs = refs[:pp]
    newsel_ref, winbuf_ref, newwin_ref, osel_ref, owin_ref, m_ref, l_ref, acc_ref = refs[pp:]
    g = pl.program_id(1)
    ng = pl.num_programs(1)
    qs = q_ref[0]
    nrow = qs.shape[0]
    flags = sel_ref[0]
    n_cols = flags.shape[1]
    qpos = past + lax.broadcasted_iota(jnp.int32, (nrow, 1), 0) % tq

    @pl.when(g == 0)
    def _():
        m_ref[...] = jnp.full(m_ref.shape, NEG_INF, F32)
        l_ref[...] = jnp.zeros(l_ref.shape, F32)
        acc_ref[...] = jnp.zeros(acc_ref.shape, F32)

    def update(s, ok, vs):
        sm = jnp.where(ok, s, NEG_INF)
        m_old = m_ref[...]
        m_new = jnp.maximum(m_old, jnp.max(sm, axis=-1, keepdims=True))
        alpha = jnp.exp(m_old - m_new)
        p = jnp.where(ok, jnp.exp(sm - m_new), 0.0)
        l_ref[...] = alpha * l_ref[...] + jnp.sum(p, axis=-1, keepdims=True)
        pb = p.astype(BF16)
        pv = _dot(pb[:, 0:PAGE_SIZE], vs[0])
        for j in range(1, len(vs)):
            pv = pv + _dot(pb[:, j * PAGE_SIZE:(j + 1) * PAGE_SIZE], vs[j])
        acc_ref[...] = alpha * acc_ref[...] + pv
        m_ref[...] = m_new

    def expand(first_block, n):
        ncol = lax.broadcasted_iota(jnp.int32, (1, n), 1)
        rowj = lax.broadcasted_iota(jnp.int32, (n_cols, 1), 0)
        emat = (rowj == first_block + ncol // SEL_BLOCK).astype(BF16)
        mf = _dot(flags.astype(BF16), emat)
        return jnp.concatenate([mf] * NSA_GROUP, axis=0) > 0.5

    n = pp * PAGE_SIZE
    pages = [page_refs[p][0] for p in range(pp)]
    s = jnp.concatenate([_nt_dot(qs, pg[:, 0:LANES].astype(BF16)) for pg in pages], axis=1)
    kpos = g * n + lax.broadcasted_iota(jnp.int32, (1, n), 1)
    update(s, expand(g * (n // SEL_BLOCK), n) & (kpos <= qpos), [pg[:, LANES:2 * LANES].astype(BF16) for pg in pages])

    @pl.when(g == ng - 1)
    def _():
        new = newsel_ref[0]
        kpos_n = past + lax.broadcasted_iota(jnp.int32, (1, PAGE_SIZE), 1)
        update(_nt_dot(qs, new[:, 0:LANES].astype(BF16)),
               expand(past // SEL_BLOCK, PAGE_SIZE) & (kpos_n <= qpos), [new[:, LANES:2 * LANES].astype(BF16)])
        gates = g_ref[0]
        osel_ref[0] = acc_ref[...] / l_ref[...] * gates[:, 1:2]

        wb = winbuf_ref[0]
        nw = newwin_ref[0]
        n_buf = wb.shape[0]
        kw = jnp.concatenate([wb[:, 0:LANES], nw[:, 0:LANES]], axis=0).astype(BF16)
        vw = jnp.concatenate([wb[:, LANES:2 * LANES], nw[:, LANES:2 * LANES]], axis=0).astype(BF16)
        wpos = past - n_buf + lax.broadcasted_iota(jnp.int32, (1, n_buf + PAGE_SIZE), 1)
        okw = (wpos <= qpos) & (wpos > qpos - WINDOW) & (wpos >= 0)
        sw = jnp.where(okw, _nt_dot(qs, kw), NEG_INF)
        e = jnp.where(okw, jnp.exp(sw - jnp.max(sw, axis=-1, keepdims=True)), 0.0)
        owin_ref[0] = _dot(e.astype(BF16), vw) / jnp.sum(e, axis=-1, keepdims=True) * gates[:, 2:3]


def _selwin_s(qs, sel, gates, pool, page_table, newsel, winbuf, newwin, tq, past, pp):
    nb, nrow, _ = qs.shape
    n_pages = page_table.shape[1]
    bmap = lambda b, g, pt: (b, 0, 0)
    b3 = lambda a: pl.BlockSpec((1,) + a.shape[1:], bmap)

    def page_spec(p):
        return pl.BlockSpec((1, PAGE_SIZE, 256), lambda b, g, pt: (pt[b * n_pages + g * pp + p], 0, 0))

    grid_spec = pltpu.PrefetchScalarGridSpec(
        num_scalar_prefetch=1,
        grid=(nb, n_pages // pp),
        in_specs=[b3(qs), b3(sel), b3(gates)] + [page_spec(p) for p in range(pp)]
                 + [b3(newsel), b3(winbuf), b3(newwin)],
        out_specs=(pl.BlockSpec((1, nrow, 128), bmap), pl.BlockSpec((1, nrow, 128), bmap)),
        scratch_shapes=[pltpu.VMEM((nrow, 1), F32), pltpu.VMEM((nrow, 1), F32), pltpu.VMEM((nrow, 128), F32)],
    )
    shp = jax.ShapeDtypeStruct((nb, nrow, 128), F32)
    return pl.pallas_call(
        functools.partial(_selwin_s_kernel, pp=pp, tq=tq, past=past),
        grid_spec=grid_spec,
        out_shape=(shp, shp),
        compiler_params=_cparams(("parallel", "arbitrary")),
        name="selwin_sample",
    )(page_table.reshape(-1), qs, sel, gates, *([pool] * pp), newsel, winbuf, newwin)


def _ret_kernel(rq_ref, rk_ref, rv_ref, sz_ref, s0_ref, dec_ref, qd_ref, kd_ref, gc_ref,
                o_ref, sout_ref, s_ref):
    c = pl.program_id(1)

    @pl.when(c == 0)
    def _():
        s_ref[...] = s0_ref[0]

    lane = lax.broadcasted_iota(jnp.int32, (1, LANES), 1)
    for pr in range(RET_HEADS // 2):
        cs = slice(pr * LANES, (pr + 1) * LANES)
        qc = rq_ref[:, cs]
        kc = rk_ref[:, cs]
        kb = kc.astype(BF16)
        state = s_ref[pr]
        sb = state.astype(BF16)
        new_state = gc_ref[pr] * state
        for hh in range(2):
            h = 2 * pr + hh
            hm = (lane >= RET_DK * hh) & (lane < RET_DK * (hh + 1))
            qm = jnp.where(hm, qc, 0.0)
            a = _nt_dot(qm.astype(BF16), kb) * dec_ref[h]
            v = rv_ref[:, h * RET_DV:(h + 1) * RET_DV]
            vb = v.astype(BF16)
            o = _dot(a.astype(BF16), vb) + _dot((qm * qd_ref[:, cs]).astype(BF16), sb)
            km = jnp.where(hm, kc * kd_ref[:, cs], 0.0)
            new_state = new_state + _dot(km.T.astype(BF16), vb)
            mu = jnp.mean(o, axis=-1, keepdims=True)
            var = jnp.mean(jnp.square(o - mu), axis=-1, keepdims=True)
            o_ref[:, h * RET_DV:(h + 1) * RET_DV] = ((o - mu) * lax.rsqrt(var + EPS)
                                                    * sz_ref[:, h * RET_DV:(h + 1) * RET_DV])
        s_ref[pr] = new_state
        sout_ref[0, pr] = new_state


def _retention(rq, rk, rv, szr, s0, c_real, nb, nc):
    cc = RET_CHUNK
    log_g = jnp.log1p(-jnp.exp2(-5.0 - jnp.arange(RET_HEADS, dtype=F32)))
    i = jnp.arange(cc, dtype=F32)
    rel = i[:, None] - i[None, :]
    dec = jnp.where(rel >= 0, jnp.exp(log_g[:, None, None] * jnp.maximum(rel, 0.0)), 0.0)
    qd = jnp.repeat(jnp.exp(log_g[None, :] * (i[:, None] + 1.0)), RET_DK, axis=1)
    kd = jnp.repeat(jnp.exp(log_g[None, :] * (c_real - 1.0 - i[:, None])), RET_DK, axis=1)
    gc = jnp.broadcast_to(jnp.repeat(jnp.exp(log_g * c_real), RET_DK).reshape(2, 128, 1), (2, 128, 128))
    row = lambda n: pl.BlockSpec((cc, n), lambda b, c: (b * nc + c, 0))
    st = pl.BlockSpec((1, 2, 128, 128), lambda b, c: (b, 0, 0, 0))
    t = rq.shape[0]
    return pl.pallas_call(
        _ret_kernel,
        grid=(nb, nc),
        in_specs=[row(256), row(256), row(512), row(512), st, _full((RET_HEADS, cc, cc)),
                  _full((cc, 256)), _full((cc, 256)), _full((2, 128, 128))],
        out_specs=(row(512), st),
        out_shape=(jax.ShapeDtypeStruct((t, 512), F32), jax.ShapeDtypeStruct((nb, 2, 128, 128), F32)),
        scratch_shapes=[pltpu.VMEM((2, 128, 128), F32)],
        compiler_params=_cparams(("parallel", "arbitrary")),
        name="retention",
    )(rq, rk, rv, szr, s0, dec, qd, kd, gc)


def _even_out_kernel(x_ref, oc_ref, os_ref, ow_ref, sz_ref, mr_ref, w_ref, y_ref):
    mixed = ((oc_ref[...] + os_ref[...] + ow_ref[...]) * sz_ref[...]).astype(BF16)
    y_ref[...] = (x_ref[...] + _dot(mixed, w_ref[0:512, :]) + _dot(mr_ref[...].astype(BF16), w_ref[512:1024, :]))


def _even_out(x, oc, osel, ow, sz, mret, w_out, tm):
    t = x.shape[0]
    row = lambda n: pl.BlockSpec((tm, n), lambda i: (i, 0))
    return pl.pallas_call(
        _even_out_kernel,
        grid=(t // tm,),
        in_specs=[row(D_MODEL), row(512), row(512), row(512), row(512), row(512), _full((1024, D_MODEL))],
        out_specs=row(D_MODEL),
        out_shape=jax.ShapeDtypeStruct((t, D_MODEL), F32),
        compiler_params=_cparams(("parallel",)),
        name="even_out",
    )(x, oc, osel, ow, sz, mret, w_out)


_HALO = 32
_CONV_ROWS = 16


def _rms(x, g):
    return (x * lax.rsqrt(jnp.mean(x * x, axis=-1, keepdims=True) + EPS) * g).astype(BF16)


def _glu(xn, w_ref):
    return _dot(xn, w_ref[:, 0:D_MODEL]) * _sigmoid(_dot(xn, w_ref[:, D_MODEL:2 * D_MODEL]))


def _ln_gate_out(x, c, z, lg_ref, lb_ref, wout_ref):
    mu = jnp.mean(c, axis=-1, keepdims=True)
    var = jnp.mean(jnp.square(c - mu), axis=-1, keepdims=True)
    yn = (c - mu) * lax.rsqrt(var + EPS) * lg_ref[...] + lb_ref[...]
    y = (yn * _sigmoid(yn)) * (z * _sigmoid(z))
    return x + _dot(y.astype(BF16), wout_ref[...])


def _odd_p_kernel(x_ref, xh_ref, g_ref, win_ref, cw_ref, cb_ref, lg_ref, lb_ref, wout_ref,
                  y_ref, ust_ref, u_scr, c_scr, *, tm):
    i = pl.program_id(0)
    x = x_ref[...]
    g = g_ref[...]
    xn = _rms(x, g)
    u = _glu(xn, win_ref)
    z = _dot(xn, win_ref[:, 2 * D_MODEL:3 * D_MODEL])
    uh = _glu(_rms(xh_ref[...], g), win_ref)
    u_scr[0:_HALO, :] = jnp.where(i > 0, uh, 0.0)
    u_scr[_HALO:_HALO + tm, :] = u
    ust_ref[...] = u[tm - _HALO:tm, :]
    off = _HALO - (CONV_WIDTH - 1)
    cb = cb_ref[...]
    for rc in range(tm // _CONV_ROWS):
        r0 = rc * _CONV_ROWS
        acc = jnp.broadcast_to(cb, (_CONV_ROWS, D_MODEL))
        for j in range(CONV_WIDTH):
            acc = acc + cw_ref[j:j + 1, :] * u_scr[r0 + j + off:r0 + j + off + _CONV_ROWS, :]
        c_scr[r0:r0 + _CONV_ROWS, :] = acc
    y_ref[...] = _ln_gate_out(x, c_scr[...], z, lg_ref, lb_ref, wout_ref)


def _odd_p(x, norm_g, w_in, cw, cb, lg, lb, w_out, tm):
    t = x.shape[0]
    row = pl.BlockSpec((tm, D_MODEL), lambda i: (i, 0))
    halo = pl.BlockSpec((_HALO, D_MODEL), lambda i: (jnp.maximum(i * (tm // _HALO) - 1, 0), 0))
    vec = _full((1, D_MODEL))
    return pl.pallas_call(
        functools.partial(_odd_p_kernel, tm=tm),
        grid=(t // tm,),
        in_specs=[row, halo, vec, _full((D_MODEL, 3 * D_MODEL)), _full((CONV_WIDTH, D_MODEL)), vec, vec, vec,
                  _full((D_MODEL, D_MODEL))],
        out_specs=(row, _full((_HALO, D_MODEL))),
        out_shape=(jax.ShapeDtypeStruct((t, D_MODEL), F32), jax.ShapeDtypeStruct((_HALO, D_MODEL), F32)),
        scratch_shapes=[pltpu.VMEM((_HALO + tm, D_MODEL), F32), pltpu.VMEM((tm, D_MODEL), F32)],
        compiler_params=_cparams(("arbitrary",)),
        name="odd_prompt",
    )(x, x, norm_g, w_in, cw, cb, lg, lb, w_out)


def _odd_s_kernel(x_ref, st_ref, g_ref, win_ref, cw_ref, cb_ref, lg_ref, lb_ref, wout_ref,
                  y_ref, u_ref, *, nb, tq):
    x = x_ref[...]
    xn = _rms(x, g_ref[...])
    u = _glu(xn, win_ref)
    z = _dot(xn, win_ref[:, 2 * D_MODEL:3 * D_MODEL])
    u_ref[...] = u
    n_state = CONV_WIDTH - 1
    cb = cb_ref[...]
    cs = []
    for t in range(tq):
        acc = jnp.broadcast_to(cb, (nb, D_MODEL))
        for j in range(CONV_WIDTH):
            e = t + j
            slab = st_ref[e * nb:(e + 1) * nb, :] if e < n_state else u[(e - n_state) * nb:(e - n_state + 1) * nb, :]
            acc = acc + cw_ref[j:j + 1, :] * slab
        cs.append(acc)
    y_ref[...] = _ln_gate_out(x, jnp.concatenate(cs, axis=0), z, lg_ref, lb_ref, wout_ref)


def _odd_s(x_tm, state_tm, norm_g, w_in, cw, cb, lg, lb, w_out, nb, tq):
    n = x_tm.shape[0]
    return pl.pallas_call(
        functools.partial(_odd_s_kernel, nb=nb, tq=tq),
        out_shape=(jax.ShapeDtypeStruct((n, D_MODEL), F32), jax.ShapeDtypeStruct((n, D_MODEL), F32)),
        compiler_params=pltpu.CompilerParams(vmem_limit_bytes=VMEM_LIMIT),
        name="odd_sample",
    )(x_tm, state_tm, norm_g, w_in, cw, cb, lg, lb, w_out)


def _rope_tables(pos):
    half = HEAD_DIM // 2
    freq = ROPE_THETA ** (-jnp.arange(half, dtype=F32) / half)
    ang = pos.astype(F32)[:, None] * freq[None, :]
    cos = jnp.cos(ang)
    sin = jnp.sin(ang)
    return jnp.tile(cos, (1, 4)), jnp.tile(jnp.concatenate([-sin, sin], axis=1), (1, 2))


def _even_weights(w_in):
    pq, kc, ks, kw, pg, zn, rq, rk, rv, zr = jnp.split(
        w_in, np.cumsum([512, 256, 256, 256, 24, 512, 256, 256, 512])[:].tolist(), axis=-1)
    zero = jnp.zeros((D_MODEL, HEAD_DIM), w_in.dtype)
    qcols = []
    for h in range(NSA_HEADS):
        wh = pq[:, h * HEAD_DIM:(h + 1) * HEAD_DIM]
        qcols += [wh, zero] if h < NSA_GROUP else [zero, wh]
    gpad = jnp.pad(pg, ((0, 0), (0, LANES - 3 * NSA_HEADS)))
    return jnp.concatenate(qcols + [kc, ks, kw, zn, rq, rk, rv, zr, gpad], axis=-1).astype(BF16)


def _pick_kv_half(o, nb, tq):
    o6 = o.reshape(nb, NSA_GROUP, NSA_KV_HEADS, tq, NSA_KV_HEADS, HEAD_DIM)
    o5 = jnp.stack([o6[:, :, 0, :, 0], o6[:, :, 1, :, 1]], axis=2)
    return o5.transpose(0, 3, 2, 1, 4).reshape(nb * tq, NSA_HEADS * HEAD_DIM)


def kernel(x_prompt, x_sample, cache_cmp_kv, cache_sel_kv, cache_win_kv, state_ret, state_conv, page_table,
           norm_even, w_in_even, q_norm_g, k_norm_g, w_cmp_pos, w_out_even,
           norm_odd, w_in_odd, conv_w, conv_b, ln_g, ln_b, w_out_odd):
    _, seq, _ = x_prompt.shape
    nb, tq, _ = x_sample.shape
    n_pages = page_table.shape[1]
    past = n_pages * PAGE_SIZE
    n_buf = cache_win_kv.shape[2]
    tm = 256
    tq_p = 128

    w_e = _even_weights(w_in_even[0])
    w_oe = w_out_even[0].astype(BF16)
    w_io = w_in_odd[0].astype(BF16)
    w_oo = w_out_odd[0].astype(BF16)
    gq = jnp.tile(q_norm_g[0], 2)[None, :]
    gk = jnp.tile(k_norm_g[0], (1, 2))
    w2 = jnp.repeat(w_cmp_pos[0].transpose(1, 0, 2).reshape(CMP_BLOCK, 4), HEAD_DIM, axis=1)
    ne = norm_even[0][None, :]
    no = norm_odd[0][None, :]

    xp = x_prompt[0]
    cos_p, sin_p = _rope_tables(jnp.arange(seq))
    (q_p, kvc_p, kvs_p, kvw_p, k2s_p, k2w_p, vts_p, vtw_p, gate_p, sz_p,
     rq_p, rk_p, rv_p, szr_p) = _even_in(xp, ne, w_e, cos_p, sin_p, gq, gk, tm)
    ident = jnp.arange(seq // PAGE_SIZE, dtype=jnp.int32)[None, :]
    ce, co = _compress(kvc_p.reshape(seq // PAGE_SIZE, PAGE_SIZE, 256), ident, w2, 16)
    cmp_p = jnp.concatenate([ce[0], co[0]], axis=0)
    kc_p = cmp_p[:, :LANES].astype(BF16)
    vtc_p = cmp_p[:, LANES:].T.astype(BF16)
    gates_t = gate_p[:, :3 * NSA_HEADS].reshape(seq, NSA_HEADS, 3).transpose(2, 1, 0)
    oc_p, sel_p = _cmp_topk_p(q_p, kc_p, vtc_p, gates_t[0], tq_p)
    os_p = _sel_p(q_p, k2s_p, vts_p, sel_p, gates_t[1], tq_p, 512)
    ow_p = _win_p(q_p, k2w_p, vtw_p, gates_t[2], tq_p)
    s0_p = jnp.zeros((1, 2, 128, 128), F32)
    mret_p, s_p = _retention(rq_p, rk_p, rv_p, szr_p, s0_p, RET_CHUNK, 1, seq // RET_CHUNK)
    y1_p = _even_out(xp, oc_p, os_p, ow_p, sz_p, mret_p, w_oe, tm)

    xs = x_sample.reshape(nb * tq, D_MODEL)
    cos_s, sin_s = _rope_tables(jnp.tile(past + jnp.arange(tq), nb))
    (q_s, kvc_s, kvs_s, kvw_s, _, _, _, _, gate_s, sz_s,
     rq_s, rk_s, rv_s, szr_s) = _even_in(xs, ne, w_e, cos_s, sin_s, gq, gk, nb * tq)
    ce_s, co_s = _compress(cache_cmp_kv[0].reshape(-1, PAGE_SIZE, 256), page_table, w2, 16)
    cmp_s = jnp.concatenate([ce_s, co_s], axis=1)
    qs_s = (q_s.reshape(nb, tq, NSA_KV_HEADS, NSA_GROUP, LANES).transpose(0, 3, 2, 1, 4)
            .reshape(nb, NSA_HEADS * tq, LANES))
    g_s = (gate_s[:, :3 * NSA_HEADS].reshape(nb, tq, NSA_KV_HEADS, NSA_GROUP, 3).transpose(0, 3, 2, 1, 4)
           .reshape(nb, NSA_HEADS * tq, 3))
    oc_s, sel_s = _cmp_topk_s(qs_s, cmp_s[..., :LANES].astype(BF16), cmp_s[..., LANES:].astype(BF16), g_s, tq, past)
    pad_new = lambda r: jnp.pad(r.reshape(nb, tq, 256), ((0, 0), (0, PAGE_SIZE - tq), (0, 0)))
    os_s, ow_s = _selwin_s(qs_s, sel_s, g_s, cache_sel_kv[0].reshape(-1, PAGE_SIZE, 256), page_table,
                           pad_new(kvs_s), cache_win_kv[0].reshape(nb, n_buf, 256), pad_new(kvw_s), tq, past, 16)
    pad_c = lambda r: jnp.pad(r.reshape(nb, tq, -1), ((0, 0), (0, RET_CHUNK - tq), (0, 0))).reshape(nb * RET_CHUNK, -1)
    mret_s, s_s = _retention(pad_c(rq_s), pad_c(rk_s), pad_c(rv_s), pad_c(szr_s),
                             state_ret[0].reshape(nb, 2, 128, 128), float(tq), nb, 1)
    mret_s = mret_s.reshape(nb, RET_CHUNK, 512)[:, :tq].reshape(nb * tq, 512)
    y1_s = _even_out(xs, _pick_kv_half(oc_s, nb, tq), _pick_kv_half(os_s, nb, tq), _pick_kv_half(ow_s, nb, tq),
                     sz_s, mret_s, w_oe, nb * tq)

    cb = conv_b[0][None, :]
    lg = ln_g[0][None, :]
    lb = ln_b[0][None, :]
    y2_p, ust_p = _odd_p(y1_p, no, w_io, conv_w[0], cb, lg, lb, w_oo, tm)
    x_tm = y1_s.reshape(nb, tq, D_MODEL).transpose(1, 0, 2).reshape(tq * nb, D_MODEL)
    st_tm = state_conv[0].transpose(1, 0, 2).reshape((CONV_WIDTH - 1) * nb, D_MODEL)
    y2_tm, u_tm = _odd_s(x_tm, st_tm, no, w_io, conv_w[0], cb, lg, lb, w_oo, nb, tq)
    y2_s = y2_tm.reshape(tq, nb, D_MODEL).transpose(1, 0, 2)
    u_s = u_tm.reshape(tq, nb, D_MODEL).transpose(1, 0, 2)

    rows6 = lambda r, b, t: r.reshape(1, b, t, 2, NSA_KV_HEADS, HEAD_DIM)
    n_win = min(WINDOW, seq)
    win_s = jnp.concatenate([cache_win_kv[0].reshape(nb, n_buf, 256), kvw_s.reshape(nb, tq, 256)], axis=1)[:, -n_buf:]
    conv_s = jnp.concatenate([state_conv[0], u_s], axis=1)[:, -(CONV_WIDTH - 1):]
    return (y2_p[None], y2_s,
            rows6(kvc_p, 1, seq), rows6(kvc_s, nb, tq),
            rows6(kvs_p, 1, seq), rows6(kvs_s, nb, tq),
            rows6(kvw_p[seq - n_win:], 1, n_win), rows6(win_s, nb, n_buf),
            s_p.reshape(1, 1, RET_HEADS, RET_DK, RET_DV), s_s.reshape(1, nb, RET_HEADS, RET_DK, RET_DV),
            ust_p[_HALO - (CONV_WIDTH - 1):][None, None], conv_s[None])
```

```python
import functools

import numpy as np
import jax
import jax.numpy as jnp
from jax import lax
from jax.experimental import pallas as pl
from jax.experimental.pallas import tpu as pltpu

F32 = jnp.float32
BF16 = jnp.bfloat16

D_MODEL = 1024
PAGE_SIZE = 128
HEAD_DIM = 64
NSA_HEADS = 8
NSA_KV_HEADS = 2
NSA_GROUP = 4
CMP_BLOCK = 32
SEL_BLOCK = 64
N_SELECT = 16
WINDOW = 512
RET_HEADS = 4
RET_DK = 64
RET_DV = 128
RET_CHUNK = 128
CONV_WIDTH = 31
ROPE_THETA = 10000.0
NEG_INF = -1e30
FORCE = 1e9
EPS = 1e-6

LOG2E = 1.4426950408889634
Q_SCALE = HEAD_DIM ** -0.5 * LOG2E

LANES = 128
VMEM_LIMIT = 56 * 1024 * 1024

_QP = 0
_KV = 1024
_ZN = 1792
_RQ = 2304
_RK = 2560
_RV = 2816
_ZR = 3328
_GT = 3840
_EVEN_COLS = 3968


def _cparams(sem):
    return pltpu.CompilerParams(dimension_semantics=sem, vmem_limit_bytes=VMEM_LIMIT)


def _sigmoid(x):
    return 1.0 / (1.0 + jnp.exp(-x))


def _nt_dot(a, b):
    return lax.dot_general(a, b, (((1,), (1,)), ((), ())), preferred_element_type=F32)


def _dot(a, b):
    return jnp.dot(a, b, preferred_element_type=F32)


def _full(shape):
    n = len(shape)
    return pl.BlockSpec(shape, lambda *_: (0,) * n)


def _even_in_kernel(x_ref, g_ref, w_ref, cos_ref, sin_ref, gq_ref, gk_ref,
                    q_ref, kvc_ref, kvs_ref, kvw_ref, k2s_ref, k2w_ref, vts_ref, vtw_ref,
                    gate_ref, sz_ref, rq_ref, rk_ref, rv_ref, szr_ref):
    x = x_ref[...]
    ms = jnp.mean(x * x, axis=-1, keepdims=True)
    xn = (x * lax.rsqrt(ms + EPS) * g_ref[...]).astype(BF16)
    cos = cos_ref[...]
    sin = sin_ref[...]
    lane = lax.broadcasted_iota(jnp.int32, (1, LANES), 1)
    lo_half = (lane & 32) == 0
    gr = lax.broadcasted_iota(jnp.int32, (LANES, LANES), 0) // HEAD_DIM
    gc = lax.broadcasted_iota(jnp.int32, (LANES, LANES), 1) // HEAD_DIM
    gmat = (gr == gc).astype(BF16)

    def proj(a, b):
        return _dot(xn, w_ref[:, a:b])

    def head_rms(h, gain):
        h2 = h * h
        hi = h2.astype(BF16)
        lo = (h2 - hi.astype(F32)).astype(BF16)
        ss = _dot(hi, gmat) + _dot(lo, gmat)
        return h * lax.rsqrt(ss * (1.0 / HEAD_DIM) + EPS) * gain

    def rope(h):
        sw = jnp.where(lo_half, pltpu.roll(h, LANES - 32, 1), pltpu.roll(h, 32, 1))
        return h * cos + sw * sin

    hq = proj(_QP, _KV)
    gq = gq_ref[...]
    for c in range(NSA_HEADS):
        hc = hq[:, c * LANES:(c + 1) * LANES]
        q_ref[:, c * LANES:(c + 1) * LANES] = (rope(head_rms(hc, gq)) * Q_SCALE).astype(BF16)

    hk = proj(_KV, _ZN)
    kv_refs = (kvc_ref, kvs_ref, kvw_ref)
    k2_refs = (None, k2s_ref, k2w_ref)
    vt_refs = (None, vts_ref, vtw_ref)
    for b in range(3):
        k = rope(head_rms(hk[:, 2 * b * LANES:(2 * b + 1) * LANES], gk_ref[b:b + 1, :]))
        v = hk[:, (2 * b + 1) * LANES:(2 * b + 2) * LANES]
        kv_refs[b][:, 0:LANES] = k
        kv_refs[b][:, LANES:2 * LANES] = v
        if k2_refs[b] is not None:
            k2_refs[b][...] = k.astype(BF16)
            vt_refs[b][...] = v.T.astype(BF16)

    hz = proj(_ZN, _RQ)
    sz_ref[...] = hz * _sigmoid(hz)
    hrq = proj(_RQ, _RK)
    hrk = proj(_RK, _RV)
    for c in range(2):
        rq_ref[:, c * LANES:(c + 1) * LANES] = rope(hrq[:, c * LANES:(c + 1) * LANES])
        rk_ref[:, c * LANES:(c + 1) * LANES] = rope(hrk[:, c * LANES:(c + 1) * LANES]) * (RET_DK ** -0.5)
    rv_ref[...] = proj(_RV, _ZR)
    hzr = proj(_ZR, _GT)
    szr_ref[...] = hzr * _sigmoid(hzr)
    gate_ref[...] = _sigmoid(proj(_GT, _EVEN_COLS))


def _even_in(x, norm_g, w_e, cos, sin, gq, gk, tm):
    t = x.shape[0]
    row = lambda n: pl.BlockSpec((tm, n), lambda i: (i, 0))
    out_shape = (
        jax.ShapeDtypeStruct((t, 1024), BF16),
        jax.ShapeDtypeStruct((t, 256), F32),
        jax.ShapeDtypeStruct((t, 256), F32),
        jax.ShapeDtypeStruct((t, 256), F32),
        jax.ShapeDtypeStruct((t, 128), BF16),
        jax.ShapeDtypeStruct((t, 128), BF16),
        jax.ShapeDtypeStruct((128, t), BF16),
        jax.ShapeDtypeStruct((128, t), BF16),
        jax.ShapeDtypeStruct((t, 128), F32),
        jax.ShapeDtypeStruct((t, 512), F32),
        jax.ShapeDtypeStruct((t, 256), F32),
        jax.ShapeDtypeStruct((t, 256), F32),
        jax.ShapeDtypeStruct((t, 512), F32),
        jax.ShapeDtypeStruct((t, 512), F32),
    )
    vt_spec = pl.BlockSpec((128, tm), lambda i: (0, i))
    out_specs = (row(1024), row(256), row(256), row(256), row(128), row(128), vt_spec, vt_spec,
                 row(128), row(512), row(256), row(256), row(512), row(512))
    return pl.pallas_call(
        _even_in_kernel,
        grid=(t // tm,),
        in_specs=[row(D_MODEL), _full((1, D_MODEL)), _full((D_MODEL, _EVEN_COLS)),
                  row(128), row(128), _full((1, 128)), _full((3, 128))],
        out_specs=out_specs,
        out_shape=out_shape,
        compiler_params=_cparams(("parallel",)),
        name="even_in",
    )(x, norm_g, w_e, cos, sin, gq, gk)


def _compress_kernel(pt_ref, *refs, pp):
    del pt_ref
    page_refs = refs[:pp]
    w_ref, oe_ref, oo_ref = refs[pp:]
    w = w_ref[...]
    for p in range(pp):
        x = page_refs[p][0]
        s = jnp.sum(x.reshape(PAGE_SIZE // CMP_BLOCK, CMP_BLOCK, 256) * w[None], axis=1)
        oe_ref[0, 2 * p:2 * p + 1, :] = s[0:1]
        oo_ref[0, 2 * p:2 * p + 1, :] = s[1:2]
        oe_ref[0, 2 * p + 1:2 * p + 2, :] = s[2:3]
        oo_ref[0, 2 * p + 1:2 * p + 2, :] = s[3:4]


def _compress(pool, page_table, w2, pp):
    nb, n_pages = page_table.shape

    def page_spec(p):
        return pl.BlockSpec((1, PAGE_SIZE, 256), lambda b, g, pt: (pt[b * n_pages + g * pp + p], 0, 0))

    out_spec = pl.BlockSpec((1, 2 * pp, 256), lambda b, g, pt: (b, g, 0))
    grid_spec = pltpu.PrefetchScalarGridSpec(
        num_scalar_prefetch=1,
        grid=(nb, n_pages // pp),
        in_specs=[page_spec(p) for p in range(pp)] + [pl.BlockSpec((CMP_BLOCK, 256), lambda b, g, pt: (0, 0))],
        out_specs=(out_spec, out_spec),
    )
    shp = jax.ShapeDtypeStruct((nb, 2 * n_pages, 256), F32)
    return pl.pallas_call(
        functools.partial(_compress_kernel, pp=pp),
        grid_spec=grid_spec,
        out_shape=(shp, shp),
        compiler_params=_cparams(("parallel", "parallel")),
        name="compress",
    )(page_table.reshape(-1), *([pool] * pp), w2)


def _cmp_lane(w):
    return (w % 2) * (LANES // 2) + w // 2


def _compress_t_kernel(pt_ref, *refs, pp):
    del pt_ref
    page_refs = refs[:pp]
    w_ref, g_ref, o_ref = refs[pp:]
    w = w_ref[...]
    his, los = [], []
    for p in range(pp):
        prod = page_refs[p][0] * w
        hi = prod.astype(BF16)
        his.append(hi)
        los.append((prod - hi.astype(F32)).astype(BF16))
    gm = g_ref[...]
    o_ref[0] = _dot(jnp.concatenate(his, axis=1), gm) + _dot(jnp.concatenate(los, axis=1), gm)


def _compress_t(pool_t, page_table, w_pos, pp):
    nb, n_pages = page_table.shape
    per_page = PAGE_SIZE // CMP_BLOCK
    assert pp * per_page == LANES
    wf = jnp.tile(jnp.broadcast_to(w_pos.transpose(0, 2, 1)[:, :, None, :],
                                   (2, NSA_KV_HEADS, HEAD_DIM, CMP_BLOCK)).reshape(256, CMP_BLOCK), (1, per_page))
    r = jnp.arange(pp * PAGE_SIZE)
    gm = (_cmp_lane(r // CMP_BLOCK)[:, None] == jnp.arange(LANES)[None, :]).astype(BF16)

    def page_spec(p):
        return pl.BlockSpec((1, 256, PAGE_SIZE), lambda b, g, pt: (pt[b * n_pages + g * pp + p], 0, 0))

    grid_spec = pltpu.PrefetchScalarGridSpec(
        num_scalar_prefetch=1,
        grid=(nb, n_pages // pp),
        in_specs=[page_spec(p) for p in range(pp)]
                 + [pl.BlockSpec((256, PAGE_SIZE), lambda b, g, pt: (0, 0)),
                    pl.BlockSpec((pp * PAGE_SIZE, LANES), lambda b, g, pt: (0, 0))],
        out_specs=pl.BlockSpec((1, 256, LANES), lambda b, g, pt: (b, 0, g)),
    )
    return pl.pallas_call(
        functools.partial(_compress_t_kernel, pp=pp),
        grid_spec=grid_spec,
        out_shape=jax.ShapeDtypeStruct((nb, 256, per_page * n_pages), F32),
        compiler_params=_cparams(("parallel", "parallel")),
        name="compress_sample",
    )(page_table.reshape(-1), *([pool_t] * pp), wf, gm)


def _stack_heads(q, k):
    return jnp.concatenate([q[:, (NSA_GROUP * k + g) * LANES:(NSA_GROUP * k + g + 1) * LANES]
                            for g in range(NSA_GROUP)], axis=0)


def _store_heads(o_ref, og, k, tq):
    for j in range(2):
        blk = jnp.concatenate([og[2 * j], og[2 * j + 1]], axis=0)
        o_ref[:, (2 * k + j) * LANES:(2 * k + j + 1) * LANES] = blk.T


def _cmp_topk_p_kernel(q_ref, kc_ref, vtc_ref, g_ref, o_ref, sel_ref, *, tq, n_half):
    i = pl.program_id(0)
    qpos = i * tq + lax.broadcasted_iota(jnp.int32, (1, tq), 1)
    r = lax.broadcasted_iota(jnp.int32, (2 * n_half, 1), 0)
    blk = jnp.where(r < n_half, 2 * r, 2 * (r - n_half) + 1)
    mask = ((blk + 1) * CMP_BLOCK - 1) <= qpos
    sb = lax.broadcasted_iota(jnp.int32, (n_half, 1), 0)
    sbf = sb.astype(F32)
    forced = (sb == qpos // SEL_BLOCK) | (sb == 0)
    future = sb * SEL_BLOCK > qpos
    q = q_ref[...]
    kc = kc_ref[...]
    gates = g_ref[...]
    scores = []
    for k in range(NSA_KV_HEADS):
        s = _nt_dot(kc, _stack_heads(q, k))
        imp = jnp.zeros((2 * n_half, tq), F32)
        ps = []
        for g in range(NSA_GROUP):
            sg = jnp.where(mask, s[:, g * tq:(g + 1) * tq], NEG_INF)
            e = jnp.exp2(sg - jnp.max(sg, axis=0, keepdims=True))
            p = jnp.where(mask, e / jnp.sum(e, axis=0, keepdims=True), 0.0)
            imp = imp + p
            ps.append(p.astype(BF16))
        ot = _dot(vtc_ref[HEAD_DIM * k:HEAD_DIM * (k + 1), :], jnp.concatenate(ps, axis=1))
        og = [ot[:, g * tq:(g + 1) * tq] * gates[NSA_GROUP * k + g:NSA_GROUP * k + g + 1, :]
              for g in range(NSA_GROUP)]
        _store_heads(o_ref, og, k, tq)
        score = jnp.where(forced, FORCE, imp[:n_half] + imp[n_half:])
        scores.append(jnp.where(future, -FORCE, score))

    def pick(_, scs):
        out = []
        for sc in scs:
            mx = jnp.max(sc, axis=0, keepdims=True)
            first = jnp.min(jnp.where(sc == mx, sbf, 1e9), axis=0, keepdims=True)
            out.append(jnp.where(sbf == first, -jnp.inf, sc))
        return tuple(out)

    scs = lax.fori_loop(0, N_SELECT, pick, tuple(scores))
    for k in range(NSA_KV_HEADS):
        sel_ref[0, k] = jnp.where(scs[k] == -jnp.inf, 0.0, NEG_INF)


def _cmp_topk_p(q, kc, vtc, gates0, tq):
    t = q.shape[0]
    n_half = kc.shape[0] // 2
    nq = t // tq
    return pl.pallas_call(
        functools.partial(_cmp_topk_p_kernel, tq=tq, n_half=n_half),
        grid=(nq,),
        in_specs=[pl.BlockSpec((tq, 1024), lambda i: (i, 0)), _full(kc.shape), _full(vtc.shape),
                  pl.BlockSpec((NSA_HEADS, tq), lambda i: (0, i))],
        out_specs=(pl.BlockSpec((tq, 512), lambda i: (i, 0)),
                   pl.BlockSpec((1, NSA_KV_HEADS, n_half, tq), lambda i: (i, 0, 0, 0))),
        out_shape=(jax.ShapeDtypeStruct((t, 512), F32),
                   jax.ShapeDtypeStruct((nq, NSA_KV_HEADS, n_half, tq), F32)),
        compiler_params=_cparams(("parallel",)),
        name="cmp_topk_prompt",
    )(q, kc, vtc, gates0)


def _sel_p_kernel(q_ref, k2_ref, vt_ref, sel_ref, g_ref, o_ref,
                  acc_ref, m_ref, l_ref, s_ref, p_ref, a_ref, qt_ref, *, tq, tk, ns):
    i = pl.program_id(0)
    qpos = i * tq + lax.broadcasted_iota(jnp.int32, (1, tq), 1)
    q = q_ref[...]
    for k in range(NSA_KV_HEADS):
        qt_ref[k] = _stack_heads(q, k).astype(F32).T.astype(BF16)
    nblk = tk // SEL_BLOCK
    assert 2 * nblk == 8 and tk % tq == 0 and ns % 2 == 0 and (k2_ref.shape[0] // tk) % ns == 0
    kiota = lax.broadcasted_iota(jnp.int32, (tk, 1), 0)
    m_ref[...] = jnp.full(m_ref.shape, NEG_INF, F32)
    l_ref[...] = jnp.zeros(l_ref.shape, F32)
    acc_ref[...] = jnp.zeros(acc_ref.shape, F32)
    p_ref[...] = jnp.zeros(p_ref.shape, BF16)
    a_ref[...] = jnp.ones(a_ref.shape, F32)
    n_tiles = (i * tq + tq + tk - 1) // tk

    max_tile = k2_ref.shape[0] // tk - 1

    def scores(kt, slot):
        kt = jnp.minimum(kt, max_tile)
        ktile = k2_ref[pl.ds(pl.multiple_of(kt * tk, tk), tk), :]
        for k in range(NSA_KV_HEADS):
            s_ref[slot, k] = _dot(ktile, qt_ref[k])

    def values(kt, slot):
        kt = jnp.clip(kt, 0, max_tile)
        vtile = vt_ref[:, pl.ds(pl.multiple_of(kt * tk, tk), tk)]
        for k in range(NSA_KV_HEADS):
            acc_ref[k] = (acc_ref[k] * a_ref[slot, k]
                          + _dot(vtile[HEAD_DIM * k:HEAD_DIM * (k + 1), :], p_ref[slot, k]))

    def softmax(j, slot):
        visible = (ns * j + slot) * tk + kiota <= qpos
        grp = pl.multiple_of((j * (ns // 2) + slot // 2) * 8, 8)
        for k in range(NSA_KV_HEADS):
            rows = sel_ref[0, k, pl.ds(grp, 8), :][(slot % 2) * nblk:(slot % 2 + 1) * nblk]
            bias = jnp.concatenate([jnp.broadcast_to(rows[r:r + 1, :], (SEL_BLOCK, tq)) for r in range(nblk)], axis=0)
            bias = jnp.where(visible, bias, NEG_INF)
            for g in range(NSA_GROUP):
                cs = slice(g * tq, (g + 1) * tq)
                sg = s_ref[slot, k, :, cs] + bias
                m_old = m_ref[k, :, cs]
                m_new = jnp.maximum(m_old, jnp.max(sg, axis=0, keepdims=True))
                alpha = jnp.exp2(m_old - m_new)
                p = jnp.exp2(sg - m_new)
                l_ref[k, :, cs] = alpha * l_ref[k, :, cs] + jnp.sum(p, axis=0, keepdims=True)
                m_ref[k, :, cs] = m_new
                a_ref[slot, k, :, cs] = alpha
                p_ref[slot, k, :, cs] = p.astype(BF16)

    scores(0, 0)

    def body(j, carry):
        for s in range(ns):
            scores(ns * j + s + 1, (s + 1) % ns)
            values(ns * j + s - 1, (s - 1) % ns)
            softmax(j, s)
        return carry

    n_steps = (n_tiles + ns - 1) // ns
    lax.fori_loop(0, n_steps, body, 0)
    values(ns * n_steps - 1, ns - 1)
    gates = g_ref[...]
    for k in range(NSA_KV_HEADS):
        ot = acc_ref[k] / l_ref[k]
        og = [ot[:, g * tq:(g + 1) * tq] * gates[NSA_GROUP * k + g:NSA_GROUP * k + g + 1, :]
              for g in range(NSA_GROUP)]
        _store_heads(o_ref, og, k, tq)


def _sel_p(q, k2, vt, sel, gates1, tq, tk, ns):
    t = q.shape[0]
    n_half = sel.shape[2]
    return pl.pallas_call(
        functools.partial(_sel_p_kernel, tq=tq, tk=tk, ns=ns),
        grid=(t // tq,),
        in_specs=[pl.BlockSpec((tq, 1024), lambda i: (i, 0)), _full(k2.shape), _full(vt.shape),
                  pl.BlockSpec((1, NSA_KV_HEADS, n_half, tq), lambda i: (i, 0, 0, 0)),
                  pl.BlockSpec((NSA_HEADS, tq), lambda i: (0, i))],
        out_specs=pl.BlockSpec((tq, 512), lambda i: (i, 0)),
        out_shape=jax.ShapeDtypeStruct((t, 512), F32),
        scratch_shapes=[pltpu.VMEM((NSA_KV_HEADS, HEAD_DIM, NSA_GROUP * tq), F32),
                        pltpu.VMEM((NSA_KV_HEADS, 1, NSA_GROUP * tq), F32),
                        pltpu.VMEM((NSA_KV_HEADS, 1, NSA_GROUP * tq), F32),
                        pltpu.VMEM((ns, NSA_KV_HEADS, tk, NSA_GROUP * tq), F32),
                        pltpu.VMEM((ns, NSA_KV_HEADS, tk, NSA_GROUP * tq), BF16),
                        pltpu.VMEM((ns, NSA_KV_HEADS, 1, NSA_GROUP * tq), F32),
                        pltpu.VMEM((NSA_KV_HEADS, LANES, NSA_GROUP * tq), BF16)],
        compiler_params=_cparams(("parallel",)),
        name="sel_prompt",
    )(q, k2, vt, sel, gates1)


def _win_p_kernel(q_ref, k2_ref, vt_ref, g_ref, o_ref, *, tq):
    i = pl.program_id(0)
    nk = WINDOW + tq
    qpos = i * tq + lax.broadcasted_iota(jnp.int32, (1, tq), 1)
    start = pl.multiple_of(jnp.maximum(i * tq - WINDOW, 0), tq)
    kpos = start + lax.broadcasted_iota(jnp.int32, (nk, 1), 0)
    ok = (kpos <= qpos) & (kpos > qpos - WINDOW)
    bias = jnp.where(ok, 0.0, NEG_INF)
    q = q_ref[...]
    ktile = k2_ref[pl.ds(start, nk), :]
    vtile = vt_ref[:, pl.ds(start, nk)]
    gates = g_ref[...]
    for k in range(NSA_KV_HEADS):
        s = _nt_dot(ktile, _stack_heads(q, k)) + jnp.concatenate([bias] * NSA_GROUP, axis=1)
        p = jnp.exp2(s - jnp.max(s, axis=0, keepdims=True))
        l = jnp.sum(p, axis=0, keepdims=True)
        ot = _dot(vtile[HEAD_DIM * k:HEAD_DIM * (k + 1), :], p.astype(BF16)) / l
        og = [ot[:, g * tq:(g + 1) * tq] * gates[NSA_GROUP * k + g:NSA_GROUP * k + g + 1, :]
              for g in range(NSA_GROUP)]
        _store_heads(o_ref, og, k, tq)


def _win_p(q, k2, vt, gates2, tq):
    t = q.shape[0]
    return pl.pallas_call(
        functools.partial(_win_p_kernel, tq=tq),
        grid=(t // tq,),
        in_specs=[pl.BlockSpec((tq, 1024), lambda i: (i, 0)), _full(k2.shape), _full(vt.shape),
                  pl.BlockSpec((NSA_HEADS, tq), lambda i: (0, i))],
        out_specs=pl.BlockSpec((tq, 512), lambda i: (i, 0)),
        out_shape=jax.ShapeDtypeStruct((t, 512), F32),
        compiler_params=_cparams(("parallel",)),
        name="win_prompt",
    )(q, k2, vt, gates2)


def _flag_block(lane):
    r = lane % LANES
    return (lane // LANES) * (LANES // 2) + r, r < LANES // 2


def _cmp_topk_s_kernel(q_ref, ct_ref, g_ref, o_ref, sel_ref, *, tq, past, n_cmp, n_sb, n_cols):
    qs = q_ref[0]
    nrow = qs.shape[0]
    ct = ct_ref[0]
    s = _dot(qs, ct[0:LANES].astype(BF16))
    qpos = past + lax.broadcasted_iota(jnp.int32, (nrow, 1), 0) % tq
    col = lax.broadcasted_iota(jnp.int32, (1, n_cmp), 1)
    r = col % LANES
    blk = (col // LANES) * LANES + 2 * (r % (LANES // 2)) + r // (LANES // 2)
    mask = ((blk + 1) * CMP_BLOCK - 1) <= qpos
    sm = jnp.where(mask, s, NEG_INF)
    e = jnp.exp2(sm - jnp.max(sm, axis=-1, keepdims=True))
    p = jnp.where(mask, e / jnp.sum(e, axis=-1, keepdims=True), 0.0)
    o_ref[0] = _nt_dot(p.astype(BF16), ct[LANES:2 * LANES].astype(BF16)) * g_ref[0][:, 0:1]

    nr8 = NSA_KV_HEADS * tq
    imp = p[0:nr8]
    for g in range(1, NSA_GROUP):
        imp = imp + p[g * nr8:(g + 1) * nr8]
    pairs = []
    for c in range(n_cmp // LANES):
        ch = imp[:, c * LANES:(c + 1) * LANES]
        pairs.append(ch + pltpu.roll(ch, LANES // 2, 1))
    imp = jnp.concatenate(pairs + [jnp.zeros((nr8, n_cols - n_cmp), F32)], axis=1)
    qpos8 = past + lax.broadcasted_iota(jnp.int32, (nr8, 1), 0) % tq
    sb, used = _flag_block(lax.broadcasted_iota(jnp.int32, (1, n_cols), 1))
    sbf = sb.astype(F32)
    score = jnp.where((sb == qpos8 // SEL_BLOCK) | (sb == 0), FORCE, imp)
    score = jnp.where(sb * SEL_BLOCK > qpos8, -FORCE, score)
    valid = used & (sb < n_sb)
    score = jnp.where(valid, score, -jnp.inf)

    def pick(_, sc):
        mx = jnp.max(sc, axis=-1, keepdims=True)
        first = jnp.min(jnp.where(sc == mx, sbf, 1e9), axis=-1, keepdims=True)
        return jnp.where(sbf == first, -jnp.inf, sc)

    sc = lax.fori_loop(0, min(N_SELECT, n_sb), pick, score)
    sel_ref[0] = jnp.where(valid & (sc == -jnp.inf), 1.0, 0.0)


def _cmp_topk_s(qs, cmp_t, gates, tq, past):
    nb, nrow, _ = qs.shape
    n_cmp = cmp_t.shape[2]
    n_sb = -(-(past + tq) // SEL_BLOCK)
    n_cols = n_cmp + LANES
    assert n_sb <= (n_cols // LANES) * (LANES // 2)
    b3 = lambda a: pl.BlockSpec((1,) + a.shape[1:], lambda b: (b, 0, 0))
    return pl.pallas_call(
        functools.partial(_cmp_topk_s_kernel, tq=tq, past=past, n_cmp=n_cmp, n_sb=n_sb, n_cols=n_cols),
        grid=(nb,),
        in_specs=[b3(qs), b3(cmp_t), b3(gates)],
        out_specs=(pl.BlockSpec((1, nrow, 128), lambda b: (b, 0, 0)),
                   pl.BlockSpec((1, NSA_KV_HEADS * tq, n_cols), lambda b: (b, 0, 0))),
        out_shape=(jax.ShapeDtypeStruct((nb, nrow, 128), F32),
                   jax.ShapeDtypeStruct((nb, NSA_KV_HEADS * tq, n_cols), F32)),
        compiler_params=_cparams(("parallel",)),
        name="cmp_topk_sample",
    )(qs, cmp_t, gates)


def _selwin_s_kernel(pt_ref, q_ref, sel_ref, g_ref, *refs, pp, tq, past):
    del pt_ref
    page_refs = refs[:pp]
    newsel_ref, winbuf_ref, newwin_ref, osel_ref, owin_ref, m_ref, l_ref, acc_ref = refs[pp:]
    g = pl.program_id(1)
    ng = pl.num_programs(1)
    qs = q_ref[0]
    nrow = qs.shape[0]
    flags = sel_ref[0]
    n_cols = flags.shape[1]
    qpos = past + lax.broadcasted_iota(jnp.int32, (nrow, 1), 0) % tq

    @pl.when(g == 0)
    def _():
        m_ref[...] = jnp.full(m_ref.shape, NEG_INF, F32)
        l_ref[...] = jnp.zeros(l_ref.shape, F32)
        acc_ref[...] = jnp.zeros(acc_ref.shape, F32)

    def update(s, ok, vts):
        sm = jnp.where(ok, s, NEG_INF)
        m_old = m_ref[...]
        m_new = jnp.maximum(m_old, jnp.max(sm, axis=-1, keepdims=True))
        alpha = jnp.exp2(m_old - m_new)
        p = jnp.where(ok, jnp.exp2(sm - m_new), 0.0)
        l_ref[...] = alpha * l_ref[...] + jnp.sum(p, axis=-1, keepdims=True)
        pb = p.astype(BF16)
        pv = _nt_dot(pb[:, 0:PAGE_SIZE], vts[0])
        for j in range(1, len(vts)):
            pv = pv + _nt_dot(pb[:, j * PAGE_SIZE:(j + 1) * PAGE_SIZE], vts[j])
        acc_ref[...] = alpha * acc_ref[...] + pv
        m_ref[...] = m_new

    def expand(first_block, n):
        ncol = lax.broadcasted_iota(jnp.int32, (1, n), 1)
        sbl, used = _flag_block(lax.broadcasted_iota(jnp.int32, (n_cols, 1), 0))
        emat = ((sbl == first_block + ncol // SEL_BLOCK) & used).astype(BF16)
        mf = _dot(flags.astype(BF16), emat)
        return jnp.concatenate([mf] * NSA_GROUP, axis=0) > 0.5

    n = pp * PAGE_SIZE
    pages = [page_refs[p][0] for p in range(pp)]
    s = jnp.concatenate([_dot(qs, pg[0:LANES].astype(BF16)) for pg in pages], axis=1)
    kpos = g * n + lax.broadcasted_iota(jnp.int32, (1, n), 1)
    update(s, expand(g * (n // SEL_BLOCK), n) & (kpos <= qpos), [pg[LANES:2 * LANES].astype(BF16) for pg in pages])

    @pl.when(g == ng - 1)
    def _():
        new = newsel_ref[0]
        kpos_n = past + lax.broadcasted_iota(jnp.int32, (1, PAGE_SIZE), 1)
        update(_dot(qs, new[0:LANES].astype(BF16)),
               expand(past // SEL_BLOCK, PAGE_SIZE) & (kpos_n <= qpos), [new[LANES:2 * LANES].astype(BF16)])
        gates = g_ref[0]
        osel_ref[0] = acc_ref[...] / l_ref[...] * gates[:, 1:2]

        wb = winbuf_ref[0]
        nw = newwin_ref[0]
        n_buf = wb.shape[1]
        kw = jnp.concatenate([wb[0:LANES], nw[0:LANES]], axis=1).astype(BF16)
        vw = jnp.concatenate([wb[LANES:2 * LANES], nw[LANES:2 * LANES]], axis=1).astype(BF16)
        wpos = past - n_buf + lax.broadcasted_iota(jnp.int32, (1, n_buf + PAGE_SIZE), 1)
        okw = (wpos <= qpos) & (wpos > qpos - WINDOW) & (wpos >= 0)
        sw = jnp.where(okw, _dot(qs, kw), NEG_INF)
        e = jnp.where(okw, jnp.exp2(sw - jnp.max(sw, axis=-1, keepdims=True)), 0.0)
        owin_ref[0] = _nt_dot(e.astype(BF16), vw) / jnp.sum(e, axis=-1, keepdims=True) * gates[:, 2:3]


def _selwin_s(qs, sel, gates, pool, page_table, newsel, winbuf, newwin, tq, past, pp):
    nb, nrow, _ = qs.shape
    n_pages = page_table.shape[1]
    bmap = lambda b, g, pt: (b, 0, 0)
    b3 = lambda a: pl.BlockSpec((1,) + a.shape[1:], bmap)

    def page_spec(p):
        return pl.BlockSpec((1, 256, PAGE_SIZE), lambda b, g, pt: (pt[b * n_pages + g * pp + p], 0, 0))

    grid_spec = pltpu.PrefetchScalarGridSpec(
        num_scalar_prefetch=1,
        grid=(nb, n_pages // pp),
        in_specs=[b3(qs), b3(sel), b3(gates)] + [page_spec(p) for p in range(pp)]
                 + [b3(newsel), b3(winbuf), b3(newwin)],
        out_specs=(pl.BlockSpec((1, nrow, 128), bmap), pl.BlockSpec((1, nrow, 128), bmap)),
        scratch_shapes=[pltpu.VMEM((nrow, 1), F32), pltpu.VMEM((nrow, 1), F32), pltpu.VMEM((nrow, 128), F32)],
    )
    shp = jax.ShapeDtypeStruct((nb, nrow, 128), F32)
    return pl.pallas_call(
        functools.partial(_selwin_s_kernel, pp=pp, tq=tq, past=past),
        grid_spec=grid_spec,
        out_shape=(shp, shp),
        compiler_params=_cparams(("parallel", "arbitrary")),
        name="selwin_sample",
    )(page_table.reshape(-1), qs, sel, gates, *([pool] * pp), newsel, winbuf, newwin)


def _ret_kernel(rq_ref, rk_ref, rv_ref, sz_ref, s0_ref, dec_ref, qd_ref, kd_ref, gc_ref,
                o_ref, sout_ref, s_ref):
    c = pl.program_id(1)

    @pl.when(c == 0)
    def _():
        s_ref[...] = s0_ref[0]

    lane = lax.broadcasted_iota(jnp.int32, (1, LANES), 1)
    for pr in range(RET_HEADS // 2):
        cs = slice(pr * LANES, (pr + 1) * LANES)
        qc = rq_ref[:, cs]
        kc = rk_ref[:, cs]
        kb = kc.astype(BF16)
        state = s_ref[pr]
        sb = state.astype(BF16)
        new_state = gc_ref[pr] * state
        for hh in range(2):
            h = 2 * pr + hh
            hm = (lane >= RET_DK * hh) & (lane < RET_DK * (hh + 1))
            qm = jnp.where(hm, qc, 0.0)
            a = _nt_dot(qm.astype(BF16), kb) * dec_ref[h]
            v = rv_ref[:, h * RET_DV:(h + 1) * RET_DV]
            vb = v.astype(BF16)
            o = _dot(a.astype(BF16), vb) + _dot((qm * qd_ref[:, cs]).astype(BF16), sb)
            km = jnp.where(hm, kc * kd_ref[:, cs], 0.0)
            new_state = new_state + _dot(km.T.astype(BF16), vb)
            mu = jnp.mean(o, axis=-1, keepdims=True)
            var = jnp.mean(jnp.square(o - mu), axis=-1, keepdims=True)
            o_ref[:, h * RET_DV:(h + 1) * RET_DV] = ((o - mu) * lax.rsqrt(var + EPS)
                                                    * sz_ref[:, h * RET_DV:(h + 1) * RET_DV])
        s_ref[pr] = new_state
        sout_ref[0, pr] = new_state


def _retention(rq, rk, rv, szr, s0, c_real, nb, nc):
    cc = RET_CHUNK
    log_g = jnp.log1p(-jnp.exp2(-5.0 - jnp.arange(RET_HEADS, dtype=F32)))
    i = jnp.arange(cc, dtype=F32)
    rel = i[:, None] - i[None, :]
    dec = jnp.where(rel >= 0, jnp.exp(log_g[:, None, None] * jnp.maximum(rel, 0.0)), 0.0)
    qd = jnp.repeat(jnp.exp(log_g[None, :] * (i[:, None] + 1.0)), RET_DK, axis=1)
    kd = jnp.repeat(jnp.exp(log_g[None, :] * (c_real - 1.0 - i[:, None])), RET_DK, axis=1)
    gc = jnp.broadcast_to(jnp.repeat(jnp.exp(log_g * c_real), RET_DK).reshape(2, 128, 1), (2, 128, 128))
    row = lambda n: pl.BlockSpec((cc, n), lambda b, c: (b * nc + c, 0))
    st = pl.BlockSpec((1, 2, 128, 128), lambda b, c: (b, 0, 0, 0))
    t = rq.shape[0]
    return pl.pallas_call(
        _ret_kernel,
        grid=(nb, nc),
        in_specs=[row(256), row(256), row(512), row(512), st, _full((RET_HEADS, cc, cc)),
                  _full((cc, 256)), _full((cc, 256)), _full((2, 128, 128))],
        out_specs=(row(512), st),
        out_shape=(jax.ShapeDtypeStruct((t, 512), F32), jax.ShapeDtypeStruct((nb, 2, 128, 128), F32)),
        scratch_shapes=[pltpu.VMEM((2, 128, 128), F32)],
        compiler_params=_cparams(("parallel", "arbitrary")),
        name="retention",
    )(rq, rk, rv, szr, s0, dec, qd, kd, gc)


def _even_out_kernel(x_ref, oc_ref, os_ref, ow_ref, sz_ref, mr_ref, w_ref, y_ref):
    mixed = ((oc_ref[...] + os_ref[...] + ow_ref[...]) * sz_ref[...]).astype(BF16)
    y_ref[...] = (x_ref[...] + _dot(mixed, w_ref[0:512, :]) + _dot(mr_ref[...].astype(BF16), w_ref[512:1024, :]))


def _even_out(x, oc, osel, ow, sz, mret, w_out, tm):
    t = x.shape[0]
    row = lambda n: pl.BlockSpec((tm, n), lambda i: (i, 0))
    return pl.pallas_call(
        _even_out_kernel,
        grid=(t // tm,),
        in_specs=[row(D_MODEL), row(512), row(512), row(512), row(512), row(512), _full((1024, D_MODEL))],
        out_specs=row(D_MODEL),
        out_shape=jax.ShapeDtypeStruct((t, D_MODEL), F32),
        compiler_params=_cparams(("parallel",)),
        name="even_out",
    )(x, oc, osel, ow, sz, mret, w_out)


_HALO = 32
_CONV_ROWS = 16


def _rms(x, g):
    return (x * lax.rsqrt(jnp.mean(x * x, axis=-1, keepdims=True) + EPS) * g).astype(BF16)


def _glu(xn, w_ref):
    return _dot(xn, w_ref[:, 0:D_MODEL]) * _sigmoid(_dot(xn, w_ref[:, D_MODEL:2 * D_MODEL]))


def _ln_gate_out(x, c, z, lg_ref, lb_ref, wout_ref):
    mu = jnp.mean(c, axis=-1, keepdims=True)
    var = jnp.mean(jnp.square(c - mu), axis=-1, keepdims=True)
    yn = (c - mu) * lax.rsqrt(var + EPS) * lg_ref[...] + lb_ref[...]
    y = (yn * _sigmoid(yn)) * (z * _sigmoid(z))
    return x + _dot(y.astype(BF16), wout_ref[...])


def _odd_p_kernel(x_ref, xh_ref, g_ref, win_ref, cw_ref, cb_ref, lg_ref, lb_ref, wout_ref,
                  y_ref, ust_ref, u_scr, c_scr, *, tm):
    i = pl.program_id(0)
    x = x_ref[...]
    g = g_ref[...]
    xn = _rms(x, g)
    u = _glu(xn, win_ref)
    z = _dot(xn, win_ref[:, 2 * D_MODEL:3 * D_MODEL])
    uh = _glu(_rms(xh_ref[...], g), win_ref)
    u_scr[0, 0:_HALO, :] = jnp.where(i > 0, uh, 0.0)
    u_scr[0, _HALO:_HALO + tm, :] = u
    u_scr[0, _HALO + tm:_HALO + tm + 8, :] = jnp.zeros((8, D_MODEL), F32)
    ust_ref[...] = u[tm - _HALO:tm, :]
    for s in range(1, 8):
        u_scr[s, 0:_HALO + tm, :] = u_scr[0, s:s + _HALO + tm, :]
    off = _HALO - (CONV_WIDTH - 1)
    cb = cb_ref[...]
    nv = _CONV_ROWS // 8
    for rc in range(tm // _CONV_ROWS):
        r0 = rc * _CONV_ROWS
        acc = jnp.broadcast_to(cb, (_CONV_ROWS, D_MODEL)).reshape(nv, 8, D_MODEL)
        for j in range(CONV_WIDTH):
            a, s = divmod(j + off, 8)
            rows = u_scr[s, r0 + 8 * a:r0 + 8 * a + _CONV_ROWS, :]
            acc = acc + cw_ref[j][None] * rows.reshape(nv, 8, D_MODEL)
        c_scr[r0:r0 + _CONV_ROWS, :] = acc.reshape(_CONV_ROWS, D_MODEL)
    y_ref[...] = _ln_gate_out(x, c_scr[...], z, lg_ref, lb_ref, wout_ref)


def _odd_p(x, norm_g, w_in, cw, cb, lg, lb, w_out, tm):
    t = x.shape[0]
    row = pl.BlockSpec((tm, D_MODEL), lambda i: (i, 0))
    halo = pl.BlockSpec((_HALO, D_MODEL), lambda i: (jnp.maximum(i * (tm // _HALO) - 1, 0), 0))
    vec = _full((1, D_MODEL))
    return pl.pallas_call(
        functools.partial(_odd_p_kernel, tm=tm),
        grid=(t // tm,),
        in_specs=[row, halo, vec, _full((D_MODEL, 3 * D_MODEL)), _full((CONV_WIDTH, 8, D_MODEL)), vec, vec, vec,
                  _full((D_MODEL, D_MODEL))],
        out_specs=(row, _full((_HALO, D_MODEL))),
        out_shape=(jax.ShapeDtypeStruct((t, D_MODEL), F32), jax.ShapeDtypeStruct((_HALO, D_MODEL), F32)),
        scratch_shapes=[pltpu.VMEM((8, _HALO + tm + 8, D_MODEL), F32), pltpu.VMEM((tm, D_MODEL), F32)],
        compiler_params=_cparams(("arbitrary",)),
        name="odd_prompt",
    )(x, x, norm_g, w_in, cw, cb, lg, lb, w_out)


def _odd_s_kernel(x_ref, st_ref, g_ref, win_ref, cw_ref, cb_ref, lg_ref, lb_ref, wout_ref,
                  y_ref, u_ref, *, nb, tq):
    x = x_ref[...]
    xn = _rms(x, g_ref[...])
    u = _glu(xn, win_ref)
    z = _dot(xn, win_ref[:, 2 * D_MODEL:3 * D_MODEL])
    u_ref[...] = u
    n_state = CONV_WIDTH - 1
    cb = cb_ref[...]
    cs = []
    for t in range(tq):
        acc = jnp.broadcast_to(cb, (nb, D_MODEL))
        for j in range(CONV_WIDTH):
            e = t + j
            slab = st_ref[e * nb:(e + 1) * nb, :] if e < n_state else u[(e - n_state) * nb:(e - n_state + 1) * nb, :]
            acc = acc + cw_ref[j:j + 1, :] * slab
        cs.append(acc)
    y_ref[...] = _ln_gate_out(x, jnp.concatenate(cs, axis=0), z, lg_ref, lb_ref, wout_ref)


def _odd_s(x_tm, state_tm, norm_g, w_in, cw, cb, lg, lb, w_out, nb, tq):
    n = x_tm.shape[0]
    return pl.pallas_call(
        functools.partial(_odd_s_kernel, nb=nb, tq=tq),
        out_shape=(jax.ShapeDtypeStruct((n, D_MODEL), F32), jax.ShapeDtypeStruct((n, D_MODEL), F32)),
        compiler_params=pltpu.CompilerParams(vmem_limit_bytes=VMEM_LIMIT),
        name="odd_sample",
    )(x_tm, state_tm, norm_g, w_in, cw, cb, lg, lb, w_out)


def _rope_tables(pos):
    half = HEAD_DIM // 2
    freq = ROPE_THETA ** (-jnp.arange(half, dtype=F32) / half)
    ang = pos.astype(F32)[:, None] * freq[None, :]
    cos = jnp.cos(ang)
    sin = jnp.sin(ang)
    return jnp.tile(cos, (1, 4)), jnp.tile(jnp.concatenate([-sin, sin], axis=1), (1, 2))


def _even_weights(w_in):
    pq, kc, ks, kw, pg, zn, rq, rk, rv, zr = jnp.split(
        w_in, np.cumsum([512, 256, 256, 256, 24, 512, 256, 256, 512])[:].tolist(), axis=-1)
    zero = jnp.zeros((D_MODEL, HEAD_DIM), w_in.dtype)
    qcols = []
    for h in range(NSA_HEADS):
        wh = pq[:, h * HEAD_DIM:(h + 1) * HEAD_DIM]
        qcols += [wh, zero] if h < NSA_GROUP else [zero, wh]
    gpad = jnp.pad(pg, ((0, 0), (0, LANES - 3 * NSA_HEADS)))
    return jnp.concatenate(qcols + [kc, ks, kw, zn, rq, rk, rv, zr, gpad], axis=-1).astype(BF16)


def _pick_kv_half(o, nb, tq):
    o6 = o.reshape(nb, NSA_GROUP, NSA_KV_HEADS, tq, NSA_KV_HEADS, HEAD_DIM)
    o5 = jnp.stack([o6[:, :, 0, :, 0], o6[:, :, 1, :, 1]], axis=2)
    return o5.transpose(0, 3, 2, 1, 4).reshape(nb * tq, NSA_HEADS * HEAD_DIM)


def kernel(x_prompt, x_sample, cache_cmp_kv, cache_sel_kv, cache_win_kv, state_ret, state_conv, page_table,
           norm_even, w_in_even, q_norm_g, k_norm_g, w_cmp_pos, w_out_even,
           norm_odd, w_in_odd, conv_w, conv_b, ln_g, ln_b, w_out_odd):
    _, seq, _ = x_prompt.shape
    nb, tq, _ = x_sample.shape
    n_pages = page_table.shape[1]
    past = n_pages * PAGE_SIZE
    n_buf = cache_win_kv.shape[2]
    tm = 256
    tq_p = 128

    w_e = _even_weights(w_in_even[0])
    w_oe = w_out_even[0].astype(BF16)
    w_io = w_in_odd[0].astype(BF16)
    w_oo = w_out_odd[0].astype(BF16)
    gq = jnp.tile(q_norm_g[0], 2)[None, :]
    gk = jnp.tile(k_norm_g[0], (1, 2))
    w2 = jnp.repeat(w_cmp_pos[0].transpose(1, 0, 2).reshape(CMP_BLOCK, 4), HEAD_DIM, axis=1)
    ne = norm_even[0][None, :]
    no = norm_odd[0][None, :]

    xp = x_prompt[0]
    cos_p, sin_p = _rope_tables(jnp.arange(seq))
    (q_p, kvc_p, kvs_p, kvw_p, k2s_p, k2w_p, vts_p, vtw_p, gate_p, sz_p,
     rq_p, rk_p, rv_p, szr_p) = _even_in(xp, ne, w_e, cos_p, sin_p, gq, gk, tm)
    ident = jnp.arange(seq // PAGE_SIZE, dtype=jnp.int32)[None, :]
    ce, co = _compress(kvc_p.reshape(seq // PAGE_SIZE, PAGE_SIZE, 256), ident, w2, 16)
    cmp_p = jnp.concatenate([ce[0], co[0]], axis=0)
    kc_p = cmp_p[:, :LANES].astype(BF16)
    vtc_p = cmp_p[:, LANES:].T.astype(BF16)
    gates_t = gate_p[:, :3 * NSA_HEADS].reshape(seq, NSA_HEADS, 3).transpose(2, 1, 0)
    oc_p, sel_p = _cmp_topk_p(q_p, kc_p, vtc_p, gates_t[0], tq_p)
    os_p = _sel_p(q_p, k2s_p, vts_p, sel_p, gates_t[1], tq_p, 256, 4)
    ow_p = _win_p(q_p, k2w_p, vtw_p, gates_t[2], tq_p)
    s0_p = jnp.zeros((1, 2, 128, 128), F32)
    mret_p, s_p = _retention(rq_p, rk_p, rv_p, szr_p, s0_p, RET_CHUNK, 1, seq // RET_CHUNK)
    y1_p = _even_out(xp, oc_p, os_p, ow_p, sz_p, mret_p, w_oe, tm)

    xs = x_sample.reshape(nb * tq, D_MODEL)
    cos_s, sin_s = _rope_tables(jnp.tile(past + jnp.arange(tq), nb))
    (q_s, kvc_s, kvs_s, kvw_s, _, _, _, _, gate_s, sz_s,
     rq_s, rk_s, rv_s, szr_s) = _even_in(xs, ne, w_e, cos_s, sin_s, gq, gk, nb * tq)
    rows_t = lambda c: c.transpose(0, 2, 3, 4, 1).reshape(c.shape[0], 256, c.shape[1])
    cmp_s = _compress_t(rows_t(cache_cmp_kv[0]), page_table, w_cmp_pos[0], LANES * CMP_BLOCK // PAGE_SIZE)
    qs_s = (q_s.reshape(nb, tq, NSA_KV_HEADS, NSA_GROUP, LANES).transpose(0, 3, 2, 1, 4)
            .reshape(nb, NSA_HEADS * tq, LANES))
    g_s = (gate_s[:, :3 * NSA_HEADS].reshape(nb, tq, NSA_KV_HEADS, NSA_GROUP, 3).transpose(0, 3, 2, 1, 4)
           .reshape(nb, NSA_HEADS * tq, 3))
    oc_s, sel_s = _cmp_topk_s(qs_s, cmp_s, g_s, tq, past)
    new_t = lambda r: jnp.pad(r.reshape(nb, tq, 256), ((0, 0), (0, PAGE_SIZE - tq), (0, 0))).transpose(0, 2, 1)
    os_s, ow_s = _selwin_s(qs_s, sel_s, g_s, rows_t(cache_sel_kv[0]), page_table,
                           new_t(kvs_s), rows_t(cache_win_kv[0]), new_t(kvw_s), tq, past, 16)
    pad_c = lambda r: jnp.pad(r.reshape(nb, tq, -1), ((0, 0), (0, RET_CHUNK - tq), (0, 0))).reshape(nb * RET_CHUNK, -1)
    mret_s, s_s = _retention(pad_c(rq_s), pad_c(rk_s), pad_c(rv_s), pad_c(szr_s),
                             state_ret[0].reshape(nb, 2, 128, 128), float(tq), nb, 1)
    mret_s = mret_s.reshape(nb, RET_CHUNK, 512)[:, :tq].reshape(nb * tq, 512)
    y1_s = _even_out(xs, _pick_kv_half(oc_s, nb, tq), _pick_kv_half(os_s, nb, tq), _pick_kv_half(ow_s, nb, tq),
                     sz_s, mret_s, w_oe, nb * tq)

    cb = conv_b[0][None, :]
    lg = ln_g[0][None, :]
    lb = ln_b[0][None, :]
    cw8 = jnp.broadcast_to(conv_w[0][:, None, :], (CONV_WIDTH, 8, D_MODEL))
    y2_p, ust_p = _odd_p(y1_p, no, w_io, cw8, cb, lg, lb, w_oo, tm)
    x_tm = y1_s.reshape(nb, tq, D_MODEL).transpose(1, 0, 2).reshape(tq * nb, D_MODEL)
    st_tm = state_conv[0].transpose(1, 0, 2).reshape((CONV_WIDTH - 1) * nb, D_MODEL)
    y2_tm, u_tm = _odd_s(x_tm, st_tm, no, w_io, conv_w[0], cb, lg, lb, w_oo, nb, tq)
    y2_s = y2_tm.reshape(tq, nb, D_MODEL).transpose(1, 0, 2)
    u_s = u_tm.reshape(tq, nb, D_MODEL).transpose(1, 0, 2)

    rows6 = lambda r, b, t: r.reshape(1, b, t, 2, NSA_KV_HEADS, HEAD_DIM)
    n_win = min(WINDOW, seq)
    win_s = jnp.concatenate([cache_win_kv[0].reshape(nb, n_buf, 256), kvw_s.reshape(nb, tq, 256)], axis=1)[:, -n_buf:]
    conv_s = jnp.concatenate([state_conv[0], u_s], axis=1)[:, -(CONV_WIDTH - 1):]
    return (y2_p[None], y2_s,
            rows6(kvc_p, 1, seq), rows6(kvc_s, nb, tq),
            rows6(kvs_p, 1, seq), rows6(kvs_s, nb, tq),
            rows6(kvw_p[seq - n_win:], 1, n_win), rows6(win_s, nb, n_buf),
            s_p.reshape(1, 1, RET_HEADS, RET_DK, RET_DV), s_s.reshape(1, nb, RET_HEADS, RET_DK, RET_DV),
            ust_p[_HALO - (CONV_WIDTH - 1):][None, None], conv_s[None])
```

```python
import functools

import numpy as np
import jax
import jax.numpy as jnp
from jax import lax
from jax.experimental import pallas as pl
from jax.experimental.pallas import tpu as pltpu

F32 = jnp.float32
BF16 = jnp.bfloat16

D_MODEL = 1024
PAGE_SIZE = 128
HEAD_DIM = 64
NSA_HEADS = 8
NSA_KV_HEADS = 2
NSA_GROUP = 4
CMP_BLOCK = 32
SEL_BLOCK = 64
N_SELECT = 16
WINDOW = 512
RET_HEADS = 4
RET_DK = 64
RET_DV = 128
RET_CHUNK = 128
CONV_WIDTH = 31
ROPE_THETA = 10000.0
NEG_INF = -1e30
FORCE = 1e9
EPS = 1e-6

LOG2E = 1.4426950408889634
Q_SCALE = HEAD_DIM ** -0.5 * LOG2E

LANES = 128
VMEM_LIMIT = 56 * 1024 * 1024

_QP = 0
_KV = 1024
_ZN = 1792
_RQ = 2304
_RK = 2560
_RV = 2816
_ZR = 3328
_GT = 3840
_EVEN_COLS = 3968


def _cparams(sem):
    return pltpu.CompilerParams(dimension_semantics=sem, vmem_limit_bytes=VMEM_LIMIT)


def _sigmoid(x):
    return 1.0 / (1.0 + jnp.exp(-x))


def _nt_dot(a, b):
    return lax.dot_general(a, b, (((1,), (1,)), ((), ())), preferred_element_type=F32)


def _dot(a, b):
    return jnp.dot(a, b, preferred_element_type=F32)


def _full(shape):
    n = len(shape)
    return pl.BlockSpec(shape, lambda *_: (0,) * n)


def _even_in_kernel(x_ref, g_ref, w_ref, cos_ref, sin_ref, gq_ref, gk_ref,
                    q_ref, kvc_ref, kvs_ref, kvw_ref, k2s_ref, k2w_ref, vts_ref, vtw_ref,
                    gate_ref, sz_ref, rq_ref, rk_ref, rv_ref, szr_ref, kn_ref):
    x = x_ref[...]
    ms = jnp.mean(x * x, axis=-1, keepdims=True)
    xn = (x * lax.rsqrt(ms + EPS) * g_ref[...]).astype(BF16)
    cos = cos_ref[...]
    sin = sin_ref[...]
    lane = lax.broadcasted_iota(jnp.int32, (1, LANES), 1)
    lo_half = (lane & 32) == 0
    gr = lax.broadcasted_iota(jnp.int32, (LANES, LANES), 0) // HEAD_DIM
    gc = lax.broadcasted_iota(jnp.int32, (LANES, LANES), 1) // HEAD_DIM
    gmat = (gr == gc).astype(BF16)

    def proj(a, b):
        return _dot(xn, w_ref[:, a:b])

    def head_rms(h, gain):
        h2 = h * h
        hi = h2.astype(BF16)
        lo = (h2 - hi.astype(F32)).astype(BF16)
        ss = _dot(hi, gmat) + _dot(lo, gmat)
        return h * lax.rsqrt(ss * (1.0 / HEAD_DIM) + EPS) * gain

    def rope(h):
        sw = jnp.where(lo_half, pltpu.roll(h, LANES - 32, 1), pltpu.roll(h, 32, 1))
        return h * cos + sw * sin

    hq = proj(_QP, _KV)
    gq = gq_ref[...]
    for c in range(NSA_HEADS):
        hc = hq[:, c * LANES:(c + 1) * LANES]
        q_ref[:, c * LANES:(c + 1) * LANES] = (rope(head_rms(hc, gq)) * Q_SCALE).astype(BF16)

    hk = proj(_KV, _ZN)
    kv_refs = (kvc_ref, kvs_ref, kvw_ref)
    k2_refs = (None, k2s_ref, k2w_ref)
    vt_refs = (None, vts_ref, vtw_ref)
    for b in range(3):
        k = rope(head_rms(hk[:, 2 * b * LANES:(2 * b + 1) * LANES], gk_ref[b:b + 1, :]))
        v = hk[:, (2 * b + 1) * LANES:(2 * b + 2) * LANES]
        kv_refs[b][:, 0:LANES] = k
        kv_refs[b][:, LANES:2 * LANES] = v
        if k2_refs[b] is not None:
            kb = k.astype(BF16)
            k2_refs[b][...] = kb
            vt_refs[b][...] = v.T.astype(BF16)
            if b == 1:
                kf = kb.astype(F32)
                k2 = kf * kf
                hi = k2.astype(BF16)
                n2 = _dot(hi, gmat) + _dot((k2 - hi.astype(F32)).astype(BF16), gmat)
                kn_ref[0] = jnp.broadcast_to(jnp.max(n2, axis=0, keepdims=True), (8, LANES)) * 1.001

    hz = proj(_ZN, _RQ)
    sz_ref[...] = hz * _sigmoid(hz)
    hrq = proj(_RQ, _RK)
    hrk = proj(_RK, _RV)
    for c in range(2):
        rq_ref[:, c * LANES:(c + 1) * LANES] = rope(hrq[:, c * LANES:(c + 1) * LANES])
        rk_ref[:, c * LANES:(c + 1) * LANES] = rope(hrk[:, c * LANES:(c + 1) * LANES]) * (RET_DK ** -0.5)
    rv_ref[...] = proj(_RV, _ZR)
    hzr = proj(_ZR, _GT)
    szr_ref[...] = hzr * _sigmoid(hzr)
    gate_ref[...] = _sigmoid(proj(_GT, _EVEN_COLS))


def _even_in(x, norm_g, w_e, cos, sin, gq, gk, tm):
    t = x.shape[0]
    row = lambda n: pl.BlockSpec((tm, n), lambda i: (i, 0))
    out_shape = (
        jax.ShapeDtypeStruct((t, 1024), BF16),
        jax.ShapeDtypeStruct((t, 256), F32),
        jax.ShapeDtypeStruct((t, 256), F32),
        jax.ShapeDtypeStruct((t, 256), F32),
        jax.ShapeDtypeStruct((t, 128), BF16),
        jax.ShapeDtypeStruct((t, 128), BF16),
        jax.ShapeDtypeStruct((128, t), BF16),
        jax.ShapeDtypeStruct((128, t), BF16),
        jax.ShapeDtypeStruct((t, 128), F32),
        jax.ShapeDtypeStruct((t, 512), F32),
        jax.ShapeDtypeStruct((t, 256), F32),
        jax.ShapeDtypeStruct((t, 256), F32),
        jax.ShapeDtypeStruct((t, 512), F32),
        jax.ShapeDtypeStruct((t, 512), F32),
        jax.ShapeDtypeStruct((t // tm, 8, 128), F32),
    )
    vt_spec = pl.BlockSpec((128, tm), lambda i: (0, i))
    out_specs = (row(1024), row(256), row(256), row(256), row(128), row(128), vt_spec, vt_spec,
                 row(128), row(512), row(256), row(256), row(512), row(512),
                 pl.BlockSpec((1, 8, 128), lambda i: (i, 0, 0)))
    return pl.pallas_call(
        _even_in_kernel,
        grid=(t // tm,),
        in_specs=[row(D_MODEL), _full((1, D_MODEL)), _full((D_MODEL, _EVEN_COLS)),
                  row(128), row(128), _full((1, 128)), _full((3, 128))],
        out_specs=out_specs,
        out_shape=out_shape,
        compiler_params=_cparams(("parallel",)),
        name="even_in",
    )(x, norm_g, w_e, cos, sin, gq, gk)


def _compress_kernel(pt_ref, *refs, pp):
    del pt_ref
    page_refs = refs[:pp]
    w_ref, oe_ref, oo_ref = refs[pp:]
    w = w_ref[...]
    for p in range(pp):
        x = page_refs[p][0]
        s = jnp.sum(x.reshape(PAGE_SIZE // CMP_BLOCK, CMP_BLOCK, 256) * w[None], axis=1)
        oe_ref[0, 2 * p:2 * p + 1, :] = s[0:1]
        oo_ref[0, 2 * p:2 * p + 1, :] = s[1:2]
        oe_ref[0, 2 * p + 1:2 * p + 2, :] = s[2:3]
        oo_ref[0, 2 * p + 1:2 * p + 2, :] = s[3:4]


def _compress(pool, page_table, w2, pp):
    nb, n_pages = page_table.shape

    def page_spec(p):
        return pl.BlockSpec((1, PAGE_SIZE, 256), lambda b, g, pt: (pt[b * n_pages + g * pp + p], 0, 0))

    out_spec = pl.BlockSpec((1, 2 * pp, 256), lambda b, g, pt: (b, g, 0))
    grid_spec = pltpu.PrefetchScalarGridSpec(
        num_scalar_prefetch=1,
        grid=(nb, n_pages // pp),
        in_specs=[page_spec(p) for p in range(pp)] + [pl.BlockSpec((CMP_BLOCK, 256), lambda b, g, pt: (0, 0))],
        out_specs=(out_spec, out_spec),
    )
    shp = jax.ShapeDtypeStruct((nb, 2 * n_pages, 256), F32)
    return pl.pallas_call(
        functools.partial(_compress_kernel, pp=pp),
        grid_spec=grid_spec,
        out_shape=(shp, shp),
        compiler_params=_cparams(("parallel", "parallel")),
        name="compress",
    )(page_table.reshape(-1), *([pool] * pp), w2)


def _cmp_lane(w):
    return (w % 2) * (LANES // 2) + w // 2


def _compress_t_kernel(pt_ref, *refs, pp):
    del pt_ref
    page_refs = refs[:pp]
    w_ref, g_ref, o_ref = refs[pp:]
    w = w_ref[...]
    prods = [(page_refs[p][0] * w).astype(BF16) for p in range(pp)]
    o_ref[0] = _dot(jnp.concatenate(prods, axis=1), g_ref[...])


def _compress_t(pool_t, page_table, w_pos, pp):
    nb, n_pages = page_table.shape
    per_page = PAGE_SIZE // CMP_BLOCK
    assert pp * per_page == LANES
    wf = jnp.tile(jnp.broadcast_to(w_pos.transpose(0, 2, 1)[:, :, None, :],
                                   (2, NSA_KV_HEADS, HEAD_DIM, CMP_BLOCK)).reshape(256, CMP_BLOCK), (1, per_page))
    r = jnp.arange(pp * PAGE_SIZE)
    gm = (_cmp_lane(r // CMP_BLOCK)[:, None] == jnp.arange(LANES)[None, :]).astype(BF16)

    def page_spec(p):
        return pl.BlockSpec((1, 256, PAGE_SIZE), lambda b, g, pt: (pt[b * n_pages + g * pp + p], 0, 0))

    grid_spec = pltpu.PrefetchScalarGridSpec(
        num_scalar_prefetch=1,
        grid=(nb, n_pages // pp),
        in_specs=[page_spec(p) for p in range(pp)]
                 + [pl.BlockSpec((256, PAGE_SIZE), lambda b, g, pt: (0, 0)),
                    pl.BlockSpec((pp * PAGE_SIZE, LANES), lambda b, g, pt: (0, 0))],
        out_specs=pl.BlockSpec((1, 256, LANES), lambda b, g, pt: (b, 0, g)),
    )
    return pl.pallas_call(
        functools.partial(_compress_t_kernel, pp=pp),
        grid_spec=grid_spec,
        out_shape=jax.ShapeDtypeStruct((nb, 256, per_page * n_pages), F32),
        compiler_params=_cparams(("parallel", "parallel")),
        name="compress_sample",
    )(page_table.reshape(-1), *([pool_t] * pp), wf, gm)


def _stack_heads(q, k):
    return jnp.concatenate([q[:, (NSA_GROUP * k + g) * LANES:(NSA_GROUP * k + g + 1) * LANES]
                            for g in range(NSA_GROUP)], axis=0)


def _store_heads(o_ref, og, k, tq):
    for j in range(2):
        blk = jnp.concatenate([og[2 * j], og[2 * j + 1]], axis=0)
        o_ref[:, (2 * k + j) * LANES:(2 * k + j + 1) * LANES] = blk.T


def _cmp_topk_p_kernel(q_ref, kc_ref, vtc_ref, g_ref, o_ref, sel_ref, *, tq, n_half):
    i = pl.program_id(0)
    qpos = i * tq + lax.broadcasted_iota(jnp.int32, (1, tq), 1)
    r = lax.broadcasted_iota(jnp.int32, (2 * n_half, 1), 0)
    blk = jnp.where(r < n_half, 2 * r, 2 * (r - n_half) + 1)
    mask = ((blk + 1) * CMP_BLOCK - 1) <= qpos
    sb = lax.broadcasted_iota(jnp.int32, (n_half, 1), 0)
    sbf = sb.astype(F32)
    forced = (sb == qpos // SEL_BLOCK) | (sb == 0)
    future = sb * SEL_BLOCK > qpos
    q = q_ref[...]
    kc = kc_ref[...]
    gates = g_ref[...]
    scores = []
    for k in range(NSA_KV_HEADS):
        s = _nt_dot(kc, _stack_heads(q, k))
        imp = jnp.zeros((2 * n_half, tq), F32)
        ps = []
        for g in range(NSA_GROUP):
            sg = jnp.where(mask, s[:, g * tq:(g + 1) * tq], NEG_INF)
            e = jnp.exp2(sg - jnp.max(sg, axis=0, keepdims=True))
            p = jnp.where(mask, e / jnp.sum(e, axis=0, keepdims=True), 0.0)
            imp = imp + p
            ps.append(p.astype(BF16))
        ot = _dot(vtc_ref[HEAD_DIM * k:HEAD_DIM * (k + 1), :], jnp.concatenate(ps, axis=1))
        og = [ot[:, g * tq:(g + 1) * tq] * gates[NSA_GROUP * k + g:NSA_GROUP * k + g + 1, :]
              for g in range(NSA_GROUP)]
        _store_heads(o_ref, og, k, tq)
        score = jnp.where(forced, FORCE, imp[:n_half] + imp[n_half:])
        scores.append(jnp.where(future, -FORCE, score))

    def pick(_, scs):
        out = []
        for sc in scs:
            mx = jnp.max(sc, axis=0, keepdims=True)
            first = jnp.min(jnp.where(sc == mx, sbf, 1e9), axis=0, keepdims=True)
            out.append(jnp.where(sbf == first, -jnp.inf, sc))
        return tuple(out)

    scs = lax.fori_loop(0, N_SELECT, pick, tuple(scores))
    for k in range(NSA_KV_HEADS):
        sel_ref[0, k] = jnp.where(scs[k] == -jnp.inf, 0.0, NEG_INF)


def _cmp_topk_p(q, kc, vtc, gates0, tq):
    t = q.shape[0]
    n_half = kc.shape[0] // 2
    nq = t // tq
    return pl.pallas_call(
        functools.partial(_cmp_topk_p_kernel, tq=tq, n_half=n_half),
        grid=(nq,),
        in_specs=[pl.BlockSpec((tq, 1024), lambda i: (i, 0)), _full(kc.shape), _full(vtc.shape),
                  pl.BlockSpec((NSA_HEADS, tq), lambda i: (0, i))],
        out_specs=(pl.BlockSpec((tq, 512), lambda i: (i, 0)),
                   pl.BlockSpec((1, NSA_KV_HEADS, n_half, tq), lambda i: (i, 0, 0, 0))),
        out_shape=(jax.ShapeDtypeStruct((t, 512), F32),
                   jax.ShapeDtypeStruct((nq, NSA_KV_HEADS, n_half, tq), F32)),
        compiler_params=_cparams(("parallel",)),
        name="cmp_topk_prompt",
    )(q, kc, vtc, gates0)


_V_ROWS = HEAD_DIM + 16
_X_ROWS = 16
_SAFE_BOUND = 60.0


def _sel_p_kernel(q_ref, k2_ref, vt_ref, kn_ref, ex_ref, sel_ref, g_ref, o_ref,
                  acc_ref, m_ref, l_ref, s_ref, p_ref, a_ref, qt_ref, *, tq, tk, ns):
    i = pl.program_id(0)
    qpos = i * tq + lax.broadcasted_iota(jnp.int32, (1, tq), 1)
    q = q_ref[...]
    qts = [_stack_heads(q, k).astype(F32).T for k in range(NSA_KV_HEADS)]
    nblk = tk // SEL_BLOCK
    assert 2 * nblk == 8 and tk % tq == 0 and ns % 2 == 0 and (k2_ref.shape[0] // tk) % ns == 0
    n_half = sel_ref.shape[2]
    kiota = lax.broadcasted_iota(jnp.int32, (tk, 1), 0)
    n_tiles = (i * tq + tq + tk - 1) // tk
    n_steps = (n_tiles + ns - 1) // ns
    max_tile = k2_ref.shape[0] // tk - 1
    gates = g_ref[...]

    def finish(numer, denom):
        for k in range(NSA_KV_HEADS):
            ot = numer(k) / denom(k)
            og = [ot[:, g * tq:(g + 1) * tq] * gates[NSA_GROUP * k + g:NSA_GROUP * k + g + 1, :]
                  for g in range(NSA_GROUP)]
            _store_heads(o_ref, og, k, tq)

    kn = jnp.max(kn_ref[...].reshape(-1, LANES), axis=0, keepdims=True)
    bounds = []
    for k in range(NSA_KV_HEADS):
        kmax2 = jnp.max(kn[:, HEAD_DIM * k:HEAD_DIM * (k + 1)], axis=1, keepdims=True)
        qn2 = jnp.sum(qts[k] * qts[k], axis=0, keepdims=True)
        bounds.append(jnp.sqrt(qn2 * kmax2) * 1.02 + 1.0)
    safe = jnp.maximum(jnp.max(bounds[0]), jnp.max(bounds[1])) < _SAFE_BOUND
    lane = lax.broadcasted_iota(jnp.int32, (1, LANES), 1)
    row8 = lax.broadcasted_iota(jnp.int32, (_X_ROWS - 8, 1), 0)
    tails = [jnp.where(row8 == 0, -b, 0.0) for b in bounds]

    def f_scores(j, s1):
        slot = s1 % ns
        kt = jnp.minimum(ns * j + s1, max_tile)
        ktile = k2_ref[pl.ds(pl.multiple_of(kt * tk, tk), tk), :]
        grp = pl.multiple_of(jnp.minimum((j * (ns // 2) + s1 // 2) * 8, n_half - 8), 8)
        for k in range(NSA_KV_HEADS):
            r0 = HEAD_DIM * (1 - k)
            rows8 = jnp.concatenate([sel_ref[0, k, pl.ds(grp, 8), :]] * NSA_GROUP, axis=1)
            qt_ref[slot, k, r0:r0 + _X_ROWS, :] = jnp.concatenate([rows8, tails[k]], axis=0).astype(BF16)
            own = (lane >= HEAD_DIM * k) & (lane < HEAD_DIM * (k + 1))
            lhs = jnp.where(own, ktile, ex_ref[s1 % 2, k])
            s_ref[slot, k] = _dot(lhs, qt_ref[slot, k])

    def f_values(kt, slot):
        kt = jnp.clip(kt, 0, max_tile)
        for k in range(NSA_KV_HEADS):
            acc_ref[k] = acc_ref[k] + _dot(vt_ref[k, :, pl.ds(pl.multiple_of(kt * tk, tk), tk)], p_ref[slot, k])

    def f_softmax(j, slot, masked):
        visible = (ns * j + slot) * tk + kiota <= qpos
        for k in range(NSA_KV_HEADS):
            for g in range(NSA_GROUP):
                cs = slice(g * tq, (g + 1) * tq)
                p = jnp.exp2(s_ref[slot, k, :, cs])
                if masked:
                    p = jnp.where(visible, p, 0.0)
                p_ref[slot, k, :, cs] = p.astype(BF16)

    def f_step(j, masked):
        for s in range(ns):
            f_scores(j, s + 1)
            f_values(ns * j + s - 1, (s - 1) % ns)
            f_softmax(j, s, masked)

    def fast_path():
        for k in range(NSA_KV_HEADS):
            w = qts[k].astype(BF16)
            for slot in range(ns):
                qt_ref[slot, k] = w
        acc_ref[...] = jnp.zeros(acc_ref.shape, F32)
        p_ref[...] = jnp.zeros(p_ref.shape, BF16)
        f_scores(0, 0)

        def body(j, carry):
            f_step(j, False)
            return carry

        lax.fori_loop(0, n_steps - 1, body, 0)
        f_step(n_steps - 1, True)
        f_values(ns * n_steps - 1, ns - 1)
        finish(lambda k: acc_ref[k, 0:HEAD_DIM], lambda k: acc_ref[k, HEAD_DIM:HEAD_DIM + 1])

    def scores(kt, slot):
        kt = jnp.minimum(kt, max_tile)
        ktile = k2_ref[pl.ds(pl.multiple_of(kt * tk, tk), tk), :]
        for k in range(NSA_KV_HEADS):
            s_ref[slot, k] = _dot(ktile, qt_ref[0, k])

    def values(kt, slot):
        kt = jnp.clip(kt, 0, max_tile)
        for k in range(NSA_KV_HEADS):
            vtile = vt_ref[k, 0:HEAD_DIM, pl.ds(pl.multiple_of(kt * tk, tk), tk)]
            acc_ref[k, 0:HEAD_DIM] = acc_ref[k, 0:HEAD_DIM] * a_ref[slot, k] + _dot(vtile, p_ref[slot, k])

    def softmax(j, slot):
        visible = (ns * j + slot) * tk + kiota <= qpos
        grp = pl.multiple_of((j * (ns // 2) + slot // 2) * 8, 8)
        for k in range(NSA_KV_HEADS):
            rows = sel_ref[0, k, pl.ds(grp, 8), :][(slot % 2) * nblk:(slot % 2 + 1) * nblk]
            bias = jnp.concatenate([jnp.broadcast_to(rows[r:r + 1, :], (SEL_BLOCK, tq)) for r in range(nblk)], axis=0)
            bias = jnp.where(visible, bias, NEG_INF)
            for g in range(NSA_GROUP):
                cs = slice(g * tq, (g + 1) * tq)
                sg = s_ref[slot, k, :, cs] + bias
                m_old = m_ref[k, :, cs]
                m_new = jnp.maximum(m_old, jnp.max(sg, axis=0, keepdims=True))
                alpha = jnp.exp2(m_old - m_new)
                p = jnp.exp2(sg - m_new)
                l_ref[k, :, cs] = alpha * l_ref[k, :, cs] + jnp.sum(p, axis=0, keepdims=True)
                m_ref[k, :, cs] = m_new
                a_ref[slot, k, :, cs] = alpha
                p_ref[slot, k, :, cs] = p.astype(BF16)

    def body(j, carry):
        for s in range(ns):
            scores(ns * j + s + 1, (s + 1) % ns)
            values(ns * j + s - 1, (s - 1) % ns)
            softmax(j, s)
        return carry

    def exact_path():
        for k in range(NSA_KV_HEADS):
            qt_ref[0, k] = qts[k].astype(BF16)
        m_ref[...] = jnp.full(m_ref.shape, NEG_INF, F32)
        l_ref[...] = jnp.zeros(l_ref.shape, F32)
        acc_ref[...] = jnp.zeros(acc_ref.shape, F32)
        p_ref[...] = jnp.zeros(p_ref.shape, BF16)
        a_ref[...] = jnp.ones(a_ref.shape, F32)
        scores(0, 0)
        lax.fori_loop(0, n_steps, body, 0)
        values(ns * n_steps - 1, ns - 1)
        finish(lambda k: acc_ref[k, 0:HEAD_DIM], lambda k: l_ref[k])

    lax.cond(safe, fast_path, exact_path)


def _sel_extra_columns(tk):
    row = np.arange(tk)[:, None]
    lane = np.arange(LANES)[None, :]
    ex = np.zeros((2, NSA_KV_HEADS, tk, LANES), np.float32)
    for par in range(2):
        for k in range(NSA_KV_HEADS):
            r0 = HEAD_DIM * (1 - k)
            ex[par, k] = (lane == r0 + par * (tk // SEL_BLOCK) + row // SEL_BLOCK) | (lane == r0 + 8)
    return jnp.asarray(ex, BF16)


def _sel_p(q, k2, vt, kn, sel, gates1, tq, tk, ns):
    t = q.shape[0]
    n_half = sel.shape[2]
    ex = _sel_extra_columns(tk)
    return pl.pallas_call(
        functools.partial(_sel_p_kernel, tq=tq, tk=tk, ns=ns),
        grid=(t // tq,),
        in_specs=[pl.BlockSpec((tq, 1024), lambda i: (i, 0)), _full(k2.shape), _full(vt.shape), _full(kn.shape),
                  _full(ex.shape),
                  pl.BlockSpec((1, NSA_KV_HEADS, n_half, tq), lambda i: (i, 0, 0, 0)),
                  pl.BlockSpec((NSA_HEADS, tq), lambda i: (0, i))],
        out_specs=pl.BlockSpec((tq, 512), lambda i: (i, 0)),
        out_shape=jax.ShapeDtypeStruct((t, 512), F32),
        scratch_shapes=[pltpu.VMEM((NSA_KV_HEADS, _V_ROWS, NSA_GROUP * tq), F32),
                        pltpu.VMEM((NSA_KV_HEADS, 1, NSA_GROUP * tq), F32),
                        pltpu.VMEM((NSA_KV_HEADS, 1, NSA_GROUP * tq), F32),
                        pltpu.VMEM((ns, NSA_KV_HEADS, tk, NSA_GROUP * tq), F32),
                        pltpu.VMEM((ns, NSA_KV_HEADS, tk, NSA_GROUP * tq), BF16),
                        pltpu.VMEM((ns, NSA_KV_HEADS, 1, NSA_GROUP * tq), F32),
                        pltpu.VMEM((ns, NSA_KV_HEADS, LANES, NSA_GROUP * tq), BF16)],
        compiler_params=_cparams(("parallel",)),
        name="sel_prompt",
    )(q, k2, vt, kn, ex, sel, gates1)


def _win_p_kernel(q_ref, k2_ref, vt_ref, g_ref, o_ref, *, tq):
    i = pl.program_id(0)
    nk = WINDOW + tq
    qpos = i * tq + lax.broadcasted_iota(jnp.int32, (1, tq), 1)
    start = pl.multiple_of(jnp.maximum(i * tq - WINDOW, 0), tq)
    kpos = start + lax.broadcasted_iota(jnp.int32, (nk, 1), 0)
    ok = (kpos <= qpos) & (kpos > qpos - WINDOW)
    bias = jnp.where(ok, 0.0, NEG_INF)
    q = q_ref[...]
    ktile = k2_ref[pl.ds(start, nk), :]
    vtile = vt_ref[:, pl.ds(start, nk)]
    gates = g_ref[...]
    for k in range(NSA_KV_HEADS):
        s = _nt_dot(ktile, _stack_heads(q, k)) + jnp.concatenate([bias] * NSA_GROUP, axis=1)
        p = jnp.exp2(s - jnp.max(s, axis=0, keepdims=True))
        l = jnp.sum(p, axis=0, keepdims=True)
        ot = _dot(vtile[HEAD_DIM * k:HEAD_DIM * (k + 1), :], p.astype(BF16)) / l
        og = [ot[:, g * tq:(g + 1) * tq] * gates[NSA_GROUP * k + g:NSA_GROUP * k + g + 1, :]
              for g in range(NSA_GROUP)]
        _store_heads(o_ref, og, k, tq)


def _win_p(q, k2, vt, gates2, tq):
    t = q.shape[0]
    return pl.pallas_call(
        functools.partial(_win_p_kernel, tq=tq),
        grid=(t // tq,),
        in_specs=[pl.BlockSpec((tq, 1024), lambda i: (i, 0)), _full(k2.shape), _full(vt.shape),
                  pl.BlockSpec((NSA_HEADS, tq), lambda i: (0, i))],
        out_specs=pl.BlockSpec((tq, 512), lambda i: (i, 0)),
        out_shape=jax.ShapeDtypeStruct((t, 512), F32),
        compiler_params=_cparams(("parallel",)),
        name="win_prompt",
    )(q, k2, vt, gates2)


def _flag_block(lane):
    r = lane % LANES
    return (lane // LANES) * (LANES // 2) + r, r < LANES // 2


def _cmp_topk_s_kernel(q_ref, ct_ref, g_ref, o_ref, sel_ref, *, tq, past, n_cmp, n_sb, n_cols):
    qs = q_ref[0]
    nrow = qs.shape[0]
    ct = ct_ref[0]
    s = _dot(qs, ct[0:LANES].astype(BF16))
    qpos = past + lax.broadcasted_iota(jnp.int32, (nrow, 1), 0) % tq
    col = lax.broadcasted_iota(jnp.int32, (1, n_cmp), 1)
    r = col % LANES
    blk = (col // LANES) * LANES + 2 * (r % (LANES // 2)) + r // (LANES // 2)
    mask = ((blk + 1) * CMP_BLOCK - 1) <= qpos
    sm = jnp.where(mask, s, NEG_INF)
    e = jnp.exp2(sm - jnp.max(sm, axis=-1, keepdims=True))
    p = jnp.where(mask, e / jnp.sum(e, axis=-1, keepdims=True), 0.0)
    o_ref[0] = _nt_dot(p.astype(BF16), ct[LANES:2 * LANES].astype(BF16)) * g_ref[0][:, 0:1]

    nr8 = NSA_KV_HEADS * tq
    imp = p[0:nr8]
    for g in range(1, NSA_GROUP):
        imp = imp + p[g * nr8:(g + 1) * nr8]
    pairs = []
    for c in range(n_cmp // LANES):
        ch = imp[:, c * LANES:(c + 1) * LANES]
        pairs.append(ch + pltpu.roll(ch, LANES // 2, 1))
    imp = jnp.concatenate(pairs + [jnp.zeros((nr8, n_cols - n_cmp), F32)], axis=1)
    qpos8 = past + lax.broadcasted_iota(jnp.int32, (nr8, 1), 0) % tq
    sb, used = _flag_block(lax.broadcasted_iota(jnp.int32, (1, n_cols), 1))
    sbf = sb.astype(F32)
    score = jnp.where((sb == qpos8 // SEL_BLOCK) | (sb == 0), FORCE, imp)
    score = jnp.where(sb * SEL_BLOCK > qpos8, -FORCE, score)
    valid = used & (sb < n_sb)
    score = jnp.where(valid, score, -jnp.inf)

    def pick(_, sc):
        mx = jnp.max(sc, axis=-1, keepdims=True)
        first = jnp.min(jnp.where(sc == mx, sbf, 1e9), axis=-1, keepdims=True)
        return jnp.where(sbf == first, -jnp.inf, sc)

    sc = lax.fori_loop(0, min(N_SELECT, n_sb), pick, score)
    sel_ref[0] = jnp.where(valid & (sc == -jnp.inf), 1.0, 0.0)


def _cmp_topk_s(qs, cmp_t, gates, tq, past):
    nb, nrow, _ = qs.shape
    n_cmp = cmp_t.shape[2]
    n_sb = -(-(past + tq) // SEL_BLOCK)
    n_cols = n_cmp + LANES
    assert n_sb <= (n_cols // LANES) * (LANES // 2)
    b3 = lambda a: pl.BlockSpec((1,) + a.shape[1:], lambda b: (b, 0, 0))
    return pl.pallas_call(
        functools.partial(_cmp_topk_s_kernel, tq=tq, past=past, n_cmp=n_cmp, n_sb=n_sb, n_cols=n_cols),
        grid=(nb,),
        in_specs=[b3(qs), b3(cmp_t), b3(gates)],
        out_specs=(pl.BlockSpec((1, nrow, 128), lambda b: (b, 0, 0)),
                   pl.BlockSpec((1, NSA_KV_HEADS * tq, n_cols), lambda b: (b, 0, 0))),
        out_shape=(jax.ShapeDtypeStruct((nb, nrow, 128), F32),
                   jax.ShapeDtypeStruct((nb, NSA_KV_HEADS * tq, n_cols), F32)),
        compiler_params=_cparams(("parallel",)),
        name="cmp_topk_sample",
    )(qs, cmp_t, gates)


def _selwin_s_kernel(pt_ref, q_ref, sel_ref, g_ref, *refs, pp, tq, past):
    del pt_ref
    page_refs = refs[:pp]
    newsel_ref, winbuf_ref, newwin_ref, em_ref, osel_ref, owin_ref, m_ref, l_ref, acc_ref = refs[pp:]
    g = pl.program_id(1)
    ng = pl.num_programs(1)
    qs = q_ref[0]
    nrow = qs.shape[0]
    flags = sel_ref[0]
    n_cols = flags.shape[1]
    qpos = past + lax.broadcasted_iota(jnp.int32, (nrow, 1), 0) % tq

    @pl.when(g == 0)
    def _():
        m_ref[...] = jnp.full(m_ref.shape, NEG_INF, F32)
        l_ref[...] = jnp.zeros(l_ref.shape, F32)
        acc_ref[...] = jnp.zeros(acc_ref.shape, F32)

    def update(s, ok, vts):
        sm = jnp.where(ok, s, NEG_INF)
        m_old = m_ref[...]
        m_new = jnp.maximum(m_old, jnp.max(sm, axis=-1, keepdims=True))
        alpha = jnp.exp2(m_old - m_new)
        p = jnp.where(ok, jnp.exp2(sm - m_new), 0.0)
        l_ref[...] = alpha * l_ref[...] + jnp.sum(p, axis=-1, keepdims=True)
        acc_ref[...] = alpha * acc_ref[...] + _nt_dot(p.astype(BF16), vts)
        m_ref[...] = m_new

    def expand(first_block, n):
        ncol = lax.broadcasted_iota(jnp.int32, (1, n), 1)
        sbl, used = _flag_block(lax.broadcasted_iota(jnp.int32, (n_cols, 1), 0))
        emat = ((sbl == first_block + ncol // SEL_BLOCK) & used).astype(BF16)
        mf = _dot(flags.astype(BF16), emat)
        return jnp.concatenate([mf] * NSA_GROUP, axis=0) > 0.5

    n = pp * PAGE_SIZE
    pages = [page_refs[p][0] for p in range(pp)]
    kt_all = jnp.concatenate([pg[0:LANES].astype(BF16) for pg in pages], axis=1)
    vt_all = jnp.concatenate([pg[LANES:2 * LANES].astype(BF16) for pg in pages], axis=1)
    per_chunk = (LANES // 2) * SEL_BLOCK // n
    chunk = sel_ref[0, :, pl.ds(pl.multiple_of((g // per_chunk) * LANES, LANES), LANES)]
    mf = _dot(chunk.astype(BF16), em_ref[g % per_chunk])
    kpos = g * n + lax.broadcasted_iota(jnp.int32, (1, n), 1)
    update(_dot(qs, kt_all), (jnp.concatenate([mf] * NSA_GROUP, axis=0) > 0.5) & (kpos <= qpos), vt_all)

    @pl.when(g == ng - 1)
    def _():
        new = newsel_ref[0]
        kpos_n = past + lax.broadcasted_iota(jnp.int32, (1, PAGE_SIZE), 1)
        update(_dot(qs, new[0:LANES].astype(BF16)),
               expand(past // SEL_BLOCK, PAGE_SIZE) & (kpos_n <= qpos), new[LANES:2 * LANES].astype(BF16))
        gates = g_ref[0]
        osel_ref[0] = acc_ref[...] / l_ref[...] * gates[:, 1:2]

        wb = winbuf_ref[0]
        nw = newwin_ref[0]
        n_buf = wb.shape[1]
        kw = jnp.concatenate([wb[0:LANES], nw[0:LANES]], axis=1).astype(BF16)
        vw = jnp.concatenate([wb[LANES:2 * LANES], nw[LANES:2 * LANES]], axis=1).astype(BF16)
        wpos = past - n_buf + lax.broadcasted_iota(jnp.int32, (1, n_buf + PAGE_SIZE), 1)
        okw = (wpos <= qpos) & (wpos > qpos - WINDOW) & (wpos >= 0)
        sw = jnp.where(okw, _dot(qs, kw), NEG_INF)
        e = jnp.where(okw, jnp.exp2(sw - jnp.max(sw, axis=-1, keepdims=True)), 0.0)
        owin_ref[0] = _nt_dot(e.astype(BF16), vw) / jnp.sum(e, axis=-1, keepdims=True) * gates[:, 2:3]


def _selwin_s(qs, sel, gates, pool, page_table, newsel, winbuf, newwin, tq, past, pp):
    nb, nrow, _ = qs.shape
    n_pages = page_table.shape[1]
    bmap = lambda b, g, pt: (b, 0, 0)
    b3 = lambda a: pl.BlockSpec((1,) + a.shape[1:], bmap)

    def page_spec(p):
        return pl.BlockSpec((1, 256, PAGE_SIZE), lambda b, g, pt: (pt[b * n_pages + g * pp + p], 0, 0))

    n = pp * PAGE_SIZE
    per_chunk = (LANES // 2) * SEL_BLOCK // n
    em = jnp.asarray(np.arange(LANES)[None, :, None] == (n // SEL_BLOCK) * np.arange(per_chunk)[:, None, None]
                     + np.arange(n)[None, None, :] // SEL_BLOCK, BF16)
    grid_spec = pltpu.PrefetchScalarGridSpec(
        num_scalar_prefetch=1,
        grid=(nb, n_pages // pp),
        in_specs=[b3(qs), b3(sel), b3(gates)] + [page_spec(p) for p in range(pp)]
                 + [b3(newsel), b3(winbuf), b3(newwin), pl.BlockSpec(em.shape, lambda b, g, pt: (0, 0, 0))],
        out_specs=(pl.BlockSpec((1, nrow, 128), bmap), pl.BlockSpec((1, nrow, 128), bmap)),
        scratch_shapes=[pltpu.VMEM((nrow, 1), F32), pltpu.VMEM((nrow, 1), F32), pltpu.VMEM((nrow, 128), F32)],
    )
    shp = jax.ShapeDtypeStruct((nb, nrow, 128), F32)
    return pl.pallas_call(
        functools.partial(_selwin_s_kernel, pp=pp, tq=tq, past=past),
        grid_spec=grid_spec,
        out_shape=(shp, shp),
        compiler_params=_cparams(("parallel", "arbitrary")),
        name="selwin_sample",
    )(page_table.reshape(-1), qs, sel, gates, *([pool] * pp), newsel, winbuf, newwin, em)


def _ret_kernel(rq_ref, rk_ref, rv_ref, sz_ref, s0_ref, dec_ref, qd_ref, kd_ref, gc_ref,
                o_ref, sout_ref, s_ref):
    c = pl.program_id(1)

    @pl.when(c == 0)
    def _():
        s_ref[...] = s0_ref[0]

    lane = lax.broadcasted_iota(jnp.int32, (1, LANES), 1)
    for pr in range(RET_HEADS // 2):
        cs = slice(pr * LANES, (pr + 1) * LANES)
        qc = rq_ref[:, cs]
        kc = rk_ref[:, cs]
        kb = kc.astype(BF16)
        state = s_ref[pr]
        sb = state.astype(BF16)
        new_state = gc_ref[pr] * state
        for hh in range(2):
            h = 2 * pr + hh
            hm = (lane >= RET_DK * hh) & (lane < RET_DK * (hh + 1))
            qm = jnp.where(hm, qc, 0.0)
            a = _nt_dot(qm.astype(BF16), kb) * dec_ref[h]
            v = rv_ref[:, h * RET_DV:(h + 1) * RET_DV]
            vb = v.astype(BF16)
            o = _dot(a.astype(BF16), vb) + _dot((qm * qd_ref[:, cs]).astype(BF16), sb)
            km = jnp.where(hm, kc * kd_ref[:, cs], 0.0)
            new_state = new_state + _dot(km.T.astype(BF16), vb)
            mu = jnp.mean(o, axis=-1, keepdims=True)
            var = jnp.mean(jnp.square(o - mu), axis=-1, keepdims=True)
            o_ref[:, h * RET_DV:(h + 1) * RET_DV] = ((o - mu) * lax.rsqrt(var + EPS)
                                                    * sz_ref[:, h * RET_DV:(h + 1) * RET_DV])
        s_ref[pr] = new_state
        sout_ref[0, pr] = new_state


def _retention(rq, rk, rv, szr, s0, c_real, nb, nc):
    cc = RET_CHUNK
    log_g = jnp.log1p(-jnp.exp2(-5.0 - jnp.arange(RET_HEADS, dtype=F32)))
    i = jnp.arange(cc, dtype=F32)
    rel = i[:, None] - i[None, :]
    dec = jnp.where(rel >= 0, jnp.exp(log_g[:, None, None] * jnp.maximum(rel, 0.0)), 0.0)
    qd = jnp.repeat(jnp.exp(log_g[None, :] * (i[:, None] + 1.0)), RET_DK, axis=1)
    kd = jnp.repeat(jnp.exp(log_g[None, :] * (c_real - 1.0 - i[:, None])), RET_DK, axis=1)
    gc = jnp.broadcast_to(jnp.repeat(jnp.exp(log_g * c_real), RET_DK).reshape(2, 128, 1), (2, 128, 128))
    row = lambda n: pl.BlockSpec((cc, n), lambda b, c: (b * nc + c, 0))
    st = pl.BlockSpec((1, 2, 128, 128), lambda b, c: (b, 0, 0, 0))
    t = rq.shape[0]
    return pl.pallas_call(
        _ret_kernel,
        grid=(nb, nc),
        in_specs=[row(256), row(256), row(512), row(512), st, _full((RET_HEADS, cc, cc)),
                  _full((cc, 256)), _full((cc, 256)), _full((2, 128, 128))],
        out_specs=(row(512), st),
        out_shape=(jax.ShapeDtypeStruct((t, 512), F32), jax.ShapeDtypeStruct((nb, 2, 128, 128), F32)),
        scratch_shapes=[pltpu.VMEM((2, 128, 128), F32)],
        compiler_params=_cparams(("parallel", "arbitrary")),
        name="retention",
    )(rq, rk, rv, szr, s0, dec, qd, kd, gc)


def _even_out_kernel(x_ref, oc_ref, os_ref, ow_ref, sz_ref, mr_ref, w_ref, y_ref):
    mixed = ((oc_ref[...] + os_ref[...] + ow_ref[...]) * sz_ref[...]).astype(BF16)
    y_ref[...] = (x_ref[...] + _dot(mixed, w_ref[0:512, :]) + _dot(mr_ref[...].astype(BF16), w_ref[512:1024, :]))


def _even_out(x, oc, osel, ow, sz, mret, w_out, tm):
    t = x.shape[0]
    row = lambda n: pl.BlockSpec((tm, n), lambda i: (i, 0))
    return pl.pallas_call(
        _even_out_kernel,
        grid=(t // tm,),
        in_specs=[row(D_MODEL), row(512), row(512), row(512), row(512), row(512), _full((1024, D_MODEL))],
        out_specs=row(D_MODEL),
        out_shape=jax.ShapeDtypeStruct((t, D_MODEL), F32),
        compiler_params=_cparams(("parallel",)),
        name="even_out",
    )(x, oc, osel, ow, sz, mret, w_out)


_HALO = 32
_CONV_ROWS = 16


def _rms(x, g):
    return (x * lax.rsqrt(jnp.mean(x * x, axis=-1, keepdims=True) + EPS) * g).astype(BF16)


def _glu(xn, w_ref):
    return _dot(xn, w_ref[:, 0:D_MODEL]) * _sigmoid(_dot(xn, w_ref[:, D_MODEL:2 * D_MODEL]))


def _ln_gate_out(x, c, z, lg_ref, lb_ref, wout_ref):
    mu = jnp.mean(c, axis=-1, keepdims=True)
    var = jnp.mean(jnp.square(c - mu), axis=-1, keepdims=True)
    yn = (c - mu) * lax.rsqrt(var + EPS) * lg_ref[...] + lb_ref[...]
    y = (yn * _sigmoid(yn)) * (z * _sigmoid(z))
    return x + _dot(y.astype(BF16), wout_ref[...])


def _odd_p_kernel(x_ref, xh_ref, g_ref, win_ref, cw_ref, cb_ref, lg_ref, lb_ref, wout_ref,
                  y_ref, ust_ref, u_scr, c_scr, *, tm):
    i = pl.program_id(0)
    x = x_ref[...]
    g = g_ref[...]
    xn = _rms(x, g)
    u = _glu(xn, win_ref)
    z = _dot(xn, win_ref[:, 2 * D_MODEL:3 * D_MODEL])
    uh = _glu(_rms(xh_ref[...], g), win_ref)
    u_scr[0, 0:_HALO, :] = jnp.where(i > 0, uh, 0.0)
    u_scr[0, _HALO:_HALO + tm, :] = u
    u_scr[0, _HALO + tm:_HALO + tm + 8, :] = jnp.zeros((8, D_MODEL), F32)
    ust_ref[...] = u[tm - _HALO:tm, :]
    for s in range(1, 8):
        u_scr[s, 0:_HALO + tm, :] = u_scr[0, s:s + _HALO + tm, :]
    off = _HALO - (CONV_WIDTH - 1)
    cb = cb_ref[...]
    nv = _CONV_ROWS // 8
    for rc in range(tm // _CONV_ROWS):
        r0 = rc * _CONV_ROWS
        acc = jnp.broadcast_to(cb, (_CONV_ROWS, D_MODEL)).reshape(nv, 8, D_MODEL)
        for j in range(CONV_WIDTH):
            a, s = divmod(j + off, 8)
            rows = u_scr[s, r0 + 8 * a:r0 + 8 * a + _CONV_ROWS, :]
            acc = acc + cw_ref[j][None] * rows.reshape(nv, 8, D_MODEL)
        c_scr[r0:r0 + _CONV_ROWS, :] = acc.reshape(_CONV_ROWS, D_MODEL)
    y_ref[...] = _ln_gate_out(x, c_scr[...], z, lg_ref, lb_ref, wout_ref)


def _odd_p(x, norm_g, w_in, cw, cb, lg, lb, w_out, tm):
    t = x.shape[0]
    row = pl.BlockSpec((tm, D_MODEL), lambda i: (i, 0))
    halo = pl.BlockSpec((_HALO, D_MODEL), lambda i: (jnp.maximum(i * (tm // _HALO) - 1, 0), 0))
    vec = _full((1, D_MODEL))
    return pl.pallas_call(
        functools.partial(_odd_p_kernel, tm=tm),
        grid=(t // tm,),
        in_specs=[row, halo, vec, _full((D_MODEL, 3 * D_MODEL)), _full((CONV_WIDTH, 8, D_MODEL)), vec, vec, vec,
                  _full((D_MODEL, D_MODEL))],
        out_specs=(row, _full((_HALO, D_MODEL))),
        out_shape=(jax.ShapeDtypeStruct((t, D_MODEL), F32), jax.ShapeDtypeStruct((_HALO, D_MODEL), F32)),
        scratch_shapes=[pltpu.VMEM((8, _HALO + tm + 8, D_MODEL), F32), pltpu.VMEM((tm, D_MODEL), F32)],
        compiler_params=_cparams(("arbitrary",)),
        name="odd_prompt",
    )(x, x, norm_g, w_in, cw, cb, lg, lb, w_out)


def _odd_s_kernel(x_ref, st_ref, g_ref, win_ref, cw_ref, cb_ref, lg_ref, lb_ref, wout_ref,
                  y_ref, u_ref, *, nb, tq):
    x = x_ref[...]
    xn = _rms(x, g_ref[...])
    u = _glu(xn, win_ref)
    z = _dot(xn, win_ref[:, 2 * D_MODEL:3 * D_MODEL])
    u_ref[...] = u
    n_state = CONV_WIDTH - 1
    cb = cb_ref[...]
    cs = []
    for t in range(tq):
        acc = jnp.broadcast_to(cb, (nb, D_MODEL))
        for j in range(CONV_WIDTH):
            e = t + j
            slab = st_ref[e * nb:(e + 1) * nb, :] if e < n_state else u[(e - n_state) * nb:(e - n_state + 1) * nb, :]
            acc = acc + cw_ref[j:j + 1, :] * slab
        cs.append(acc)
    y_ref[...] = _ln_gate_out(x, jnp.concatenate(cs, axis=0), z, lg_ref, lb_ref, wout_ref)


def _odd_s(x_tm, state_tm, norm_g, w_in, cw, cb, lg, lb, w_out, nb, tq):
    n = x_tm.shape[0]
    return pl.pallas_call(
        functools.partial(_odd_s_kernel, nb=nb, tq=tq),
        out_shape=(jax.ShapeDtypeStruct((n, D_MODEL), F32), jax.ShapeDtypeStruct((n, D_MODEL), F32)),
        compiler_params=pltpu.CompilerParams(vmem_limit_bytes=VMEM_LIMIT),
        name="odd_sample",
    )(x_tm, state_tm, norm_g, w_in, cw, cb, lg, lb, w_out)


def _rope_tables(pos):
    half = HEAD_DIM // 2
    freq = ROPE_THETA ** (-jnp.arange(half, dtype=F32) / half)
    ang = pos.astype(F32)[:, None] * freq[None, :]
    cos = jnp.cos(ang)
    sin = jnp.sin(ang)
    return jnp.tile(cos, (1, 4)), jnp.tile(jnp.concatenate([-sin, sin], axis=1), (1, 2))


def _even_weights(w_in):
    pq, kc, ks, kw, pg, zn, rq, rk, rv, zr = jnp.split(
        w_in, np.cumsum([512, 256, 256, 256, 24, 512, 256, 256, 512])[:].tolist(), axis=-1)
    zero = jnp.zeros((D_MODEL, HEAD_DIM), w_in.dtype)
    qcols = []
    for h in range(NSA_HEADS):
        wh = pq[:, h * HEAD_DIM:(h + 1) * HEAD_DIM]
        qcols += [wh, zero] if h < NSA_GROUP else [zero, wh]
    gpad = jnp.pad(pg, ((0, 0), (0, LANES - 3 * NSA_HEADS)))
    return jnp.concatenate(qcols + [kc, ks, kw, zn, rq, rk, rv, zr, gpad], axis=-1).astype(BF16)


def _pick_kv_half(o, nb, tq):
    o6 = o.reshape(nb, NSA_GROUP, NSA_KV_HEADS, tq, NSA_KV_HEADS, HEAD_DIM)
    o5 = jnp.stack([o6[:, :, 0, :, 0], o6[:, :, 1, :, 1]], axis=2)
    return o5.transpose(0, 3, 2, 1, 4).reshape(nb * tq, NSA_HEADS * HEAD_DIM)


def kernel(x_prompt, x_sample, cache_cmp_kv, cache_sel_kv, cache_win_kv, state_ret, state_conv, page_table,
           norm_even, w_in_even, q_norm_g, k_norm_g, w_cmp_pos, w_out_even,
           norm_odd, w_in_odd, conv_w, conv_b, ln_g, ln_b, w_out_odd):
    _, seq, _ = x_prompt.shape
    nb, tq, _ = x_sample.shape
    n_pages = page_table.shape[1]
    past = n_pages * PAGE_SIZE
    n_buf = cache_win_kv.shape[2]
    tm = 256
    tq_p = 128

    w_e = _even_weights(w_in_even[0])
    w_oe = w_out_even[0].astype(BF16)
    w_io = w_in_odd[0].astype(BF16)
    w_oo = w_out_odd[0].astype(BF16)
    gq = jnp.tile(q_norm_g[0], 2)[None, :]
    gk = jnp.tile(k_norm_g[0], (1, 2))
    w2 = jnp.repeat(w_cmp_pos[0].transpose(1, 0, 2).reshape(CMP_BLOCK, 4), HEAD_DIM, axis=1)
    ne = norm_even[0][None, :]
    no = norm_odd[0][None, :]

    xp = x_prompt[0]
    cos_p, sin_p = _rope_tables(jnp.arange(seq))
    (q_p, kvc_p, kvs_p, kvw_p, k2s_p, k2w_p, vts_p, vtw_p, gate_p, sz_p,
     rq_p, rk_p, rv_p, szr_p, kn_p) = _even_in(xp, ne, w_e, cos_p, sin_p, gq, gk, tm)
    ident = jnp.arange(seq // PAGE_SIZE, dtype=jnp.int32)[None, :]
    ce, co = _compress(kvc_p.reshape(seq // PAGE_SIZE, PAGE_SIZE, 256), ident, w2, 16)
    cmp_p = jnp.concatenate([ce[0], co[0]], axis=0)
    kc_p = cmp_p[:, :LANES].astype(BF16)
    vtc_p = cmp_p[:, LANES:].T.astype(BF16)
    gates_t = gate_p[:, :3 * NSA_HEADS].reshape(seq, NSA_HEADS, 3).transpose(2, 1, 0)
    oc_p, sel_p = _cmp_topk_p(q_p, kc_p, vtc_p, gates_t[0], tq_p)
    ones_rows = jnp.concatenate([jnp.ones((NSA_KV_HEADS, 1, seq), BF16),
                                 jnp.zeros((NSA_KV_HEADS, _V_ROWS - HEAD_DIM - 1, seq), BF16)], axis=1)
    vt_ext = jnp.concatenate([vts_p.reshape(NSA_KV_HEADS, HEAD_DIM, seq), ones_rows], axis=1)
    os_p = _sel_p(q_p, k2s_p, vt_ext, kn_p, sel_p, gates_t[1], tq_p, 256, 4)
    ow_p = _win_p(q_p, k2w_p, vtw_p, gates_t[2], tq_p)
    s0_p = jnp.zeros((1, 2, 128, 128), F32)
    mret_p, s_p = _retention(rq_p, rk_p, rv_p, szr_p, s0_p, RET_CHUNK, 1, seq // RET_CHUNK)
    y1_p = _even_out(xp, oc_p, os_p, ow_p, sz_p, mret_p, w_oe, tm)

    xs = x_sample.reshape(nb * tq, D_MODEL)
    cos_s, sin_s = _rope_tables(jnp.tile(past + jnp.arange(tq), nb))
    (q_s, kvc_s, kvs_s, kvw_s, _, _, _, _, gate_s, sz_s,
     rq_s, rk_s, rv_s, szr_s, _) = _even_in(xs, ne, w_e, cos_s, sin_s, gq, gk, nb * tq)
    rows_t = lambda c: c.transpose(0, 2, 3, 4, 1).reshape(c.shape[0], 256, c.shape[1])
    cmp_s = _compress_t(rows_t(cache_cmp_kv[0]), page_table, w_cmp_pos[0], LANES * CMP_BLOCK // PAGE_SIZE)
    qs_s = (q_s.reshape(nb, tq, NSA_KV_HEADS, NSA_GROUP, LANES).transpose(0, 3, 2, 1, 4)
            .reshape(nb, NSA_HEADS * tq, LANES))
    g_s = (gate_s[:, :3 * NSA_HEADS].reshape(nb, tq, NSA_KV_HEADS, NSA_GROUP, 3).transpose(0, 3, 2, 1, 4)
           .reshape(nb, NSA_HEADS * tq, 3))
    oc_s, sel_s = _cmp_topk_s(qs_s, cmp_s, g_s, tq, past)
    new_t = lambda r: jnp.pad(r.reshape(nb, tq, 256), ((0, 0), (0, PAGE_SIZE - tq), (0, 0))).transpose(0, 2, 1)
    os_s, ow_s = _selwin_s(qs_s, sel_s, g_s, rows_t(cache_sel_kv[0]), page_table,
                           new_t(kvs_s), rows_t(cache_win_kv[0]), new_t(kvw_s), tq, past, 16)
    pad_c = lambda r: jnp.pad(r.reshape(nb, tq, -1), ((0, 0), (0, RET_CHUNK - tq), (0, 0))).reshape(nb * RET_CHUNK, -1)
    mret_s, s_s = _retention(pad_c(rq_s), pad_c(rk_s), pad_c(rv_s), pad_c(szr_s),
                             state_ret[0].reshape(nb, 2, 128, 128), float(tq), nb, 1)
    mret_s = mret_s.reshape(nb, RET_CHUNK, 512)[:, :tq].reshape(nb * tq, 512)
    y1_s = _even_out(xs, _pick_kv_half(oc_s, nb, tq), _pick_kv_half(os_s, nb, tq), _pick_kv_half(ow_s, nb, tq),
                     sz_s, mret_s, w_oe, nb * tq)

    cb = conv_b[0][None, :]
    lg = ln_g[0][None, :]
    lb = ln_b[0][None, :]
    cw8 = jnp.broadcast_to(conv_w[0][:, None, :], (CONV_WIDTH, 8, D_MODEL))
    y2_p, ust_p = _odd_p(y1_p, no, w_io, cw8, cb, lg, lb, w_oo, tm)
    x_tm = y1_s.reshape(nb, tq, D_MODEL).transpose(1, 0, 2).reshape(tq * nb, D_MODEL)
    st_tm = state_conv[0].transpose(1, 0, 2).reshape((CONV_WIDTH - 1) * nb, D_MODEL)
    y2_tm, u_tm = _odd_s(x_tm, st_tm, no, w_io, conv_w[0], cb, lg, lb, w_oo, nb, tq)
    y2_s = y2_tm.reshape(tq, nb, D_MODEL).transpose(1, 0, 2)
    u_s = u_tm.reshape(tq, nb, D_MODEL).transpose(1, 0, 2)

    rows6 = lambda r, b, t: r.reshape(1, b, t, 2, NSA_KV_HEADS, HEAD_DIM)
    n_win = min(WINDOW, seq)
    win_s = jnp.concatenate([cache_win_kv[0].reshape(nb, n_buf, 256), kvw_s.reshape(nb, tq, 256)], axis=1)[:, -n_buf:]
    conv_s = jnp.concatenate([state_conv[0], u_s], axis=1)[:, -(CONV_WIDTH - 1):]
    return (y2_p[None], y2_s,
            rows6(kvc_p, 1, seq), rows6(kvc_s, nb, tq),
            rows6(kvs_p, 1, seq), rows6(kvs_s, nb, tq),
            rows6(kvw_p[seq - n_win:], 1, n_win), rows6(win_s, nb, n_buf),
            s_p.reshape(1, 1, RET_HEADS, RET_DK, RET_DV), s_s.reshape(1, nb, RET_HEADS, RET_DK, RET_DV),
            ust_p[_HALO - (CONV_WIDTH - 1):][None, None], conv_s[None])
```

```python
import functools

import numpy as np
import jax
import jax.numpy as jnp
from jax import lax
from jax.experimental import pallas as pl
from jax.experimental.pallas import tpu as pltpu

F32 = jnp.float32
BF16 = jnp.bfloat16

D_MODEL = 1024
PAGE_SIZE = 128
HEAD_DIM = 64
NSA_HEADS = 8
NSA_KV_HEADS = 2
NSA_GROUP = 4
CMP_BLOCK = 32
SEL_BLOCK = 64
N_SELECT = 16
WINDOW = 512
RET_HEADS = 4
RET_DK = 64
RET_DV = 128
RET_CHUNK = 128
CONV_WIDTH = 31
ROPE_THETA = 10000.0
NEG_INF = -1e30
FORCE = 1e9
EPS = 1e-6

LOG2E = 1.4426950408889634
Q_SCALE = HEAD_DIM ** -0.5 * LOG2E

LANES = 128
VMEM_LIMIT = 56 * 1024 * 1024

_QP = 0
_KV = 1024
_ZN = 1792
_RQ = 2304
_RK = 2560
_RV = 2816
_ZR = 3328
_GT = 3840
_EVEN_COLS = 3968


def _cparams(sem):
    return pltpu.CompilerParams(dimension_semantics=sem, vmem_limit_bytes=VMEM_LIMIT)


def _sigmoid(x):
    return 1.0 / (1.0 + jnp.exp(-x))


def _nt_dot(a, b):
    return lax.dot_general(a, b, (((1,), (1,)), ((), ())), preferred_element_type=F32)


def _dot(a, b):
    return jnp.dot(a, b, preferred_element_type=F32)


def _full(shape):
    n = len(shape)
    return pl.BlockSpec(shape, lambda *_: (0,) * n)


def _even_in_kernel(x_ref, g_ref, w_ref, cos_ref, sin_ref, gq_ref, gk_ref,
                    q_ref, kvc_ref, kvs_ref, kvw_ref, k2s_ref, k2w_ref, vts_ref, vtw_ref,
                    gate_ref, sz_ref, rq_ref, rk_ref, rv_ref, szr_ref, kn_ref):
    x = x_ref[...]
    ms = jnp.mean(x * x, axis=-1, keepdims=True)
    xn = (x * lax.rsqrt(ms + EPS) * g_ref[...]).astype(BF16)
    cos = cos_ref[...]
    sin = sin_ref[...]
    lane = lax.broadcasted_iota(jnp.int32, (1, LANES), 1)
    lo_half = (lane & 32) == 0
    gr = lax.broadcasted_iota(jnp.int32, (LANES, LANES), 0) // HEAD_DIM
    gc = lax.broadcasted_iota(jnp.int32, (LANES, LANES), 1) // HEAD_DIM
    gmat = (gr == gc).astype(BF16)

    def proj(a, b):
        return _dot(xn, w_ref[:, a:b])

    def head_rms(h, gain):
        h2 = h * h
        hi = h2.astype(BF16)
        lo = (h2 - hi.astype(F32)).astype(BF16)
        ss = _dot(hi, gmat) + _dot(lo, gmat)
        return h * lax.rsqrt(ss * (1.0 / HEAD_DIM) + EPS) * gain

    def rope(h):
        sw = jnp.where(lo_half, pltpu.roll(h, LANES - 32, 1), pltpu.roll(h, 32, 1))
        return h * cos + sw * sin

    hq = proj(_QP, _KV)
    gq = gq_ref[...]
    for c in range(NSA_HEADS):
        hc = hq[:, c * LANES:(c + 1) * LANES]
        q_ref[:, c * LANES:(c + 1) * LANES] = (rope(head_rms(hc, gq)) * Q_SCALE).astype(BF16)

    hk = proj(_KV, _ZN)
    kv_refs = (kvc_ref, kvs_ref, kvw_ref)
    k2_refs = (None, k2s_ref, k2w_ref)
    vt_refs = (None, vts_ref, vtw_ref)
    for b in range(3):
        k = rope(head_rms(hk[:, 2 * b * LANES:(2 * b + 1) * LANES], gk_ref[b:b + 1, :]))
        v = hk[:, (2 * b + 1) * LANES:(2 * b + 2) * LANES]
        kv_refs[b][:, 0:LANES] = k
        kv_refs[b][:, LANES:2 * LANES] = v
        if k2_refs[b] is not None:
            kb = k.astype(BF16)
            k2_refs[b][...] = kb
            vt_refs[b][...] = v.T.astype(BF16)
            if b == 1:
                kf = kb.astype(F32)
                k2 = kf * kf
                hi = k2.astype(BF16)
                n2 = _dot(hi, gmat) + _dot((k2 - hi.astype(F32)).astype(BF16), gmat)
                kn_ref[0] = jnp.broadcast_to(jnp.max(n2, axis=0, keepdims=True), (8, LANES)) * 1.001

    hz = proj(_ZN, _RQ)
    sz_ref[...] = hz * _sigmoid(hz)
    hrq = proj(_RQ, _RK)
    hrk = proj(_RK, _RV)
    for c in range(2):
        rq_ref[:, c * LANES:(c + 1) * LANES] = rope(hrq[:, c * LANES:(c + 1) * LANES])
        rk_ref[:, c * LANES:(c + 1) * LANES] = rope(hrk[:, c * LANES:(c + 1) * LANES]) * (RET_DK ** -0.5)
    rv_ref[...] = proj(_RV, _ZR)
    hzr = proj(_ZR, _GT)
    szr_ref[...] = hzr * _sigmoid(hzr)
    gate_ref[...] = _sigmoid(proj(_GT, _EVEN_COLS))


def _even_in(x, norm_g, w_e, cos, sin, gq, gk, tm):
    t = x.shape[0]
    row = lambda n: pl.BlockSpec((tm, n), lambda i: (i, 0))
    out_shape = (
        jax.ShapeDtypeStruct((t, 1024), BF16),
        jax.ShapeDtypeStruct((t, 256), F32),
        jax.ShapeDtypeStruct((t, 256), F32),
        jax.ShapeDtypeStruct((t, 256), F32),
        jax.ShapeDtypeStruct((t, 128), BF16),
        jax.ShapeDtypeStruct((t, 128), BF16),
        jax.ShapeDtypeStruct((128, t), BF16),
        jax.ShapeDtypeStruct((128, t), BF16),
        jax.ShapeDtypeStruct((t, 128), F32),
        jax.ShapeDtypeStruct((t, 512), F32),
        jax.ShapeDtypeStruct((t, 256), F32),
        jax.ShapeDtypeStruct((t, 256), F32),
        jax.ShapeDtypeStruct((t, 512), F32),
        jax.ShapeDtypeStruct((t, 512), F32),
        jax.ShapeDtypeStruct((t // tm, 8, 128), F32),
    )
    vt_spec = pl.BlockSpec((128, tm), lambda i: (0, i))
    out_specs = (row(1024), row(256), row(256), row(256), row(128), row(128), vt_spec, vt_spec,
                 row(128), row(512), row(256), row(256), row(512), row(512),
                 pl.BlockSpec((1, 8, 128), lambda i: (i, 0, 0)))
    return pl.pallas_call(
        _even_in_kernel,
        grid=(t // tm,),
        in_specs=[row(D_MODEL), _full((1, D_MODEL)), _full((D_MODEL, _EVEN_COLS)),
                  row(128), row(128), _full((1, 128)), _full((3, 128))],
        out_specs=out_specs,
        out_shape=out_shape,
        compiler_params=_cparams(("parallel",)),
        name="even_in",
    )(x, norm_g, w_e, cos, sin, gq, gk)


def _compress_kernel(pt_ref, *refs, pp):
    del pt_ref
    page_refs = refs[:pp]
    w_ref, oe_ref, oo_ref = refs[pp:]
    w = w_ref[...]
    for p in range(pp):
        x = page_refs[p][0]
        s = jnp.sum(x.reshape(PAGE_SIZE // CMP_BLOCK, CMP_BLOCK, 256) * w[None], axis=1)
        oe_ref[0, 2 * p:2 * p + 1, :] = s[0:1]
        oo_ref[0, 2 * p:2 * p + 1, :] = s[1:2]
        oe_ref[0, 2 * p + 1:2 * p + 2, :] = s[2:3]
        oo_ref[0, 2 * p + 1:2 * p + 2, :] = s[3:4]


def _compress(pool, page_table, w2, pp):
    nb, n_pages = page_table.shape

    def page_spec(p):
        return pl.BlockSpec((1, PAGE_SIZE, 256), lambda b, g, pt: (pt[b * n_pages + g * pp + p], 0, 0))

    out_spec = pl.BlockSpec((1, 2 * pp, 256), lambda b, g, pt: (b, g, 0))
    grid_spec = pltpu.PrefetchScalarGridSpec(
        num_scalar_prefetch=1,
        grid=(nb, n_pages // pp),
        in_specs=[page_spec(p) for p in range(pp)] + [pl.BlockSpec((CMP_BLOCK, 256), lambda b, g, pt: (0, 0))],
        out_specs=(out_spec, out_spec),
    )
    shp = jax.ShapeDtypeStruct((nb, 2 * n_pages, 256), F32)
    return pl.pallas_call(
        functools.partial(_compress_kernel, pp=pp),
        grid_spec=grid_spec,
        out_shape=(shp, shp),
        compiler_params=_cparams(("parallel", "parallel")),
        name="compress",
    )(page_table.reshape(-1), *([pool] * pp), w2)


def _cmp_lane(w):
    return (w % 2) * (LANES // 2) + w // 2


def _compress_t_kernel(pt_ref, *refs, pp):
    del pt_ref
    page_refs = refs[:pp]
    w_ref, g_ref, o_ref = refs[pp:]
    w = w_ref[...]
    prods = [(page_refs[p][0] * w).astype(BF16) for p in range(pp)]
    o_ref[0] = _dot(jnp.concatenate(prods, axis=1), g_ref[...])


def _compress_t(pool_t, page_table, w_pos, pp):
    nb, n_pages = page_table.shape
    per_page = PAGE_SIZE // CMP_BLOCK
    assert pp * per_page == LANES
    wf = jnp.tile(jnp.broadcast_to(w_pos.transpose(0, 2, 1)[:, :, None, :],
                                   (2, NSA_KV_HEADS, HEAD_DIM, CMP_BLOCK)).reshape(256, CMP_BLOCK), (1, per_page))
    r = jnp.arange(pp * PAGE_SIZE)
    gm = (_cmp_lane(r // CMP_BLOCK)[:, None] == jnp.arange(LANES)[None, :]).astype(BF16)

    def page_spec(p):
        return pl.BlockSpec((1, 256, PAGE_SIZE), lambda b, g, pt: (pt[b * n_pages + g * pp + p], 0, 0))

    grid_spec = pltpu.PrefetchScalarGridSpec(
        num_scalar_prefetch=1,
        grid=(nb, n_pages // pp),
        in_specs=[page_spec(p) for p in range(pp)]
                 + [pl.BlockSpec((256, PAGE_SIZE), lambda b, g, pt: (0, 0)),
                    pl.BlockSpec((pp * PAGE_SIZE, LANES), lambda b, g, pt: (0, 0))],
        out_specs=pl.BlockSpec((1, 256, LANES), lambda b, g, pt: (b, 0, g)),
    )
    return pl.pallas_call(
        functools.partial(_compress_t_kernel, pp=pp),
        grid_spec=grid_spec,
        out_shape=jax.ShapeDtypeStruct((nb, 256, per_page * n_pages), F32),
        compiler_params=_cparams(("parallel", "parallel")),
        name="compress_sample",
    )(page_table.reshape(-1), *([pool_t] * pp), wf, gm)


def _stack_heads(q, k):
    return jnp.concatenate([q[:, (NSA_GROUP * k + g) * LANES:(NSA_GROUP * k + g + 1) * LANES]
                            for g in range(NSA_GROUP)], axis=0)


def _store_heads(o_ref, og, k, tq):
    for j in range(2):
        blk = jnp.concatenate([og[2 * j], og[2 * j + 1]], axis=0)
        o_ref[:, (2 * k + j) * LANES:(2 * k + j + 1) * LANES] = blk.T


def _cmp_topk_p_kernel(q_ref, kc_ref, vtc_ref, g_ref, o_ref, sel_ref, *, tq, n_half):
    i = pl.program_id(0)
    qpos = i * tq + lax.broadcasted_iota(jnp.int32, (1, tq), 1)
    r = lax.broadcasted_iota(jnp.int32, (2 * n_half, 1), 0)
    blk = jnp.where(r < n_half, 2 * r, 2 * (r - n_half) + 1)
    mask = ((blk + 1) * CMP_BLOCK - 1) <= qpos
    sb = lax.broadcasted_iota(jnp.int32, (n_half, 1), 0)
    sbf = sb.astype(F32)
    forced = (sb == qpos // SEL_BLOCK) | (sb == 0)
    future = sb * SEL_BLOCK > qpos
    q = q_ref[...]
    kc = kc_ref[...]
    gates = g_ref[...]
    scores = []
    for k in range(NSA_KV_HEADS):
        s = _nt_dot(kc, _stack_heads(q, k))
        imp = jnp.zeros((2 * n_half, tq), F32)
        ps = []
        for g in range(NSA_GROUP):
            sg = jnp.where(mask, s[:, g * tq:(g + 1) * tq], NEG_INF)
            e = jnp.exp2(sg - jnp.max(sg, axis=0, keepdims=True))
            p = jnp.where(mask, e / jnp.sum(e, axis=0, keepdims=True), 0.0)
            imp = imp + p
            ps.append(p.astype(BF16))
        ot = _dot(vtc_ref[HEAD_DIM * k:HEAD_DIM * (k + 1), :], jnp.concatenate(ps, axis=1))
        og = [ot[:, g * tq:(g + 1) * tq] * gates[NSA_GROUP * k + g:NSA_GROUP * k + g + 1, :]
              for g in range(NSA_GROUP)]
        _store_heads(o_ref, og, k, tq)
        score = jnp.where(forced, FORCE, imp[:n_half] + imp[n_half:])
        scores.append(jnp.where(future, -FORCE, score))

    def pick(_, scs):
        out = []
        for sc in scs:
            mx = jnp.max(sc, axis=0, keepdims=True)
            first = jnp.min(jnp.where(sc == mx, sbf, 1e9), axis=0, keepdims=True)
            out.append(jnp.where(sbf == first, -jnp.inf, sc))
        return tuple(out)

    scs = lax.fori_loop(0, N_SELECT, pick, tuple(scores))
    for k in range(NSA_KV_HEADS):
        sel_ref[0, k] = jnp.where(scs[k] == -jnp.inf, 0.0, NEG_INF)


def _cmp_topk_p(q, kc, vtc, gates0, tq):
    t = q.shape[0]
    n_half = kc.shape[0] // 2
    nq = t // tq
    return pl.pallas_call(
        functools.partial(_cmp_topk_p_kernel, tq=tq, n_half=n_half),
        grid=(nq,),
        in_specs=[pl.BlockSpec((tq, 1024), lambda i: (i, 0)), _full(kc.shape), _full(vtc.shape),
                  pl.BlockSpec((NSA_HEADS, tq), lambda i: (0, i))],
        out_specs=(pl.BlockSpec((tq, 512), lambda i: (i, 0)),
                   pl.BlockSpec((1, NSA_KV_HEADS, n_half, tq), lambda i: (i, 0, 0, 0))),
        out_shape=(jax.ShapeDtypeStruct((t, 512), F32),
                   jax.ShapeDtypeStruct((nq, NSA_KV_HEADS, n_half, tq), F32)),
        compiler_params=_cparams(("parallel",)),
        name="cmp_topk_prompt",
    )(q, kc, vtc, gates0)


_V_ROWS = HEAD_DIM + 16
_X_ROWS = 16
_SAFE_BOUND = 60.0


def _sel_p_kernel(q_ref, k2_ref, vt_ref, kn_ref, ex_ref, sel_ref, g_ref, oc_ref, ow_ref, sz_ref, o_ref,
                  acc_ref, m_ref, l_ref, s_ref, p_ref, a_ref, qt_ref, os_ref, *, tq, tk, ns):
    i = pl.program_id(0)
    qpos = i * tq + lax.broadcasted_iota(jnp.int32, (1, tq), 1)
    q = q_ref[...]
    qts = [_stack_heads(q, k).astype(F32).T for k in range(NSA_KV_HEADS)]
    nblk = tk // SEL_BLOCK
    assert 2 * nblk == 8 and tk % tq == 0 and ns % 2 == 0 and (k2_ref.shape[0] // tk) % ns == 0
    n_half = sel_ref.shape[2]
    kiota = lax.broadcasted_iota(jnp.int32, (tk, 1), 0)
    n_tiles = (i * tq + tq + tk - 1) // tk
    n_steps = (n_tiles + ns - 1) // ns
    max_tile = k2_ref.shape[0] // tk - 1
    gates = g_ref[...]

    def finish(numer, denom):
        for k in range(NSA_KV_HEADS):
            ot = numer(k) / denom(k)
            og = [ot[:, g * tq:(g + 1) * tq] * gates[NSA_GROUP * k + g:NSA_GROUP * k + g + 1, :]
                  for g in range(NSA_GROUP)]
            _store_heads(os_ref, og, k, tq)
        o_ref[...] = ((oc_ref[...] + os_ref[...] + ow_ref[...]) * sz_ref[...]).astype(BF16)

    kn = jnp.max(kn_ref[...].reshape(-1, LANES), axis=0, keepdims=True)
    bounds = []
    for k in range(NSA_KV_HEADS):
        kmax2 = jnp.max(kn[:, HEAD_DIM * k:HEAD_DIM * (k + 1)], axis=1, keepdims=True)
        qn2 = jnp.sum(qts[k] * qts[k], axis=0, keepdims=True)
        bounds.append(jnp.sqrt(qn2 * kmax2) * 1.02 + 1.0)
    safe = jnp.maximum(jnp.max(bounds[0]), jnp.max(bounds[1])) < _SAFE_BOUND
    lane = lax.broadcasted_iota(jnp.int32, (1, LANES), 1)
    row8 = lax.broadcasted_iota(jnp.int32, (_X_ROWS - 8, 1), 0)
    tails = [jnp.where(row8 == 0, -b, 0.0) for b in bounds]

    def f_scores(j, s1):
        slot = s1 % ns
        kt = jnp.minimum(ns * j + s1, max_tile)
        ktile = k2_ref[pl.ds(pl.multiple_of(kt * tk, tk), tk), :]
        grp = pl.multiple_of(jnp.minimum((j * (ns // 2) + s1 // 2) * 8, n_half - 8), 8)
        for k in range(NSA_KV_HEADS):
            r0 = HEAD_DIM * (1 - k)
            rows8 = jnp.concatenate([sel_ref[0, k, pl.ds(grp, 8), :]] * NSA_GROUP, axis=1)
            qt_ref[slot, k, r0:r0 + _X_ROWS, :] = jnp.concatenate([rows8, tails[k]], axis=0).astype(BF16)
            own = (lane >= HEAD_DIM * k) & (lane < HEAD_DIM * (k + 1))
            lhs = jnp.where(own, ktile, ex_ref[s1 % 2, k])
            s_ref[slot, k] = _dot(lhs, qt_ref[slot, k])

    def f_values(kt, slot):
        kt = jnp.clip(kt, 0, max_tile)
        for k in range(NSA_KV_HEADS):
            acc_ref[k] = acc_ref[k] + _dot(vt_ref[k, :, pl.ds(pl.multiple_of(kt * tk, tk), tk)], p_ref[slot, k])

    def f_softmax(j, slot, masked):
        visible = (ns * j + slot) * tk + kiota <= qpos
        for k in range(NSA_KV_HEADS):
            for g in range(NSA_GROUP):
                cs = slice(g * tq, (g + 1) * tq)
                p = jnp.exp2(s_ref[slot, k, :, cs])
                if masked:
                    p = jnp.where(visible, p, 0.0)
                p_ref[slot, k, :, cs] = p.astype(BF16)

    def f_step(j, masked):
        for s in range(ns):
            f_scores(j, s + 1)
            f_values(ns * j + s - 1, (s - 1) % ns)
            f_softmax(j, s, masked)

    def fast_path():
        for k in range(NSA_KV_HEADS):
            w = qts[k].astype(BF16)
            for slot in range(ns):
                qt_ref[slot, k] = w
        acc_ref[...] = jnp.zeros(acc_ref.shape, F32)
        p_ref[...] = jnp.zeros(p_ref.shape, BF16)
        f_scores(0, 0)

        def body(j, carry):
            f_step(j, False)
            return carry

        lax.fori_loop(0, n_steps - 1, body, 0)
        f_step(n_steps - 1, True)
        f_values(ns * n_steps - 1, ns - 1)
        finish(lambda k: acc_ref[k, 0:HEAD_DIM], lambda k: acc_ref[k, HEAD_DIM:HEAD_DIM + 1])

    def scores(kt, slot):
        kt = jnp.minimum(kt, max_tile)
        ktile = k2_ref[pl.ds(pl.multiple_of(kt * tk, tk), tk), :]
        for k in range(NSA_KV_HEADS):
            s_ref[slot, k] = _dot(ktile, qt_ref[0, k])

    def values(kt, slot):
        kt = jnp.clip(kt, 0, max_tile)
        for k in range(NSA_KV_HEADS):
            vtile = vt_ref[k, 0:HEAD_DIM, pl.ds(pl.multiple_of(kt * tk, tk), tk)]
            acc_ref[k, 0:HEAD_DIM] = acc_ref[k, 0:HEAD_DIM] * a_ref[slot, k] + _dot(vtile, p_ref[slot, k])

    def softmax(j, slot):
        visible = (ns * j + slot) * tk + kiota <= qpos
        grp = pl.multiple_of((j * (ns // 2) + slot // 2) * 8, 8)
        for k in range(NSA_KV_HEADS):
            rows = sel_ref[0, k, pl.ds(grp, 8), :][(slot % 2) * nblk:(slot % 2 + 1) * nblk]
            bias = jnp.concatenate([jnp.broadcast_to(rows[r:r + 1, :], (SEL_BLOCK, tq)) for r in range(nblk)], axis=0)
            bias = jnp.where(visible, bias, NEG_INF)
            for g in range(NSA_GROUP):
                cs = slice(g * tq, (g + 1) * tq)
                sg = s_ref[slot, k, :, cs] + bias
                m_old = m_ref[k, :, cs]
                m_new = jnp.maximum(m_old, jnp.max(sg, axis=0, keepdims=True))
                alpha = jnp.exp2(m_old - m_new)
                p = jnp.exp2(sg - m_new)
                l_ref[k, :, cs] = alpha * l_ref[k, :, cs] + jnp.sum(p, axis=0, keepdims=True)
                m_ref[k, :, cs] = m_new
                a_ref[slot, k, :, cs] = alpha
                p_ref[slot, k, :, cs] = p.astype(BF16)

    def body(j, carry):
        for s in range(ns):
            scores(ns * j + s + 1, (s + 1) % ns)
            values(ns * j + s - 1, (s - 1) % ns)
            softmax(j, s)
        return carry

    def exact_path():
        for k in range(NSA_KV_HEADS):
            qt_ref[0, k] = qts[k].astype(BF16)
        m_ref[...] = jnp.full(m_ref.shape, NEG_INF, F32)
        l_ref[...] = jnp.zeros(l_ref.shape, F32)
        acc_ref[...] = jnp.zeros(acc_ref.shape, F32)
        p_ref[...] = jnp.zeros(p_ref.shape, BF16)
        a_ref[...] = jnp.ones(a_ref.shape, F32)
        scores(0, 0)
        lax.fori_loop(0, n_steps, body, 0)
        values(ns * n_steps - 1, ns - 1)
        finish(lambda k: acc_ref[k, 0:HEAD_DIM], lambda k: l_ref[k])

    lax.cond(safe, fast_path, exact_path)


def _sel_extra_columns(tk):
    row = np.arange(tk)[:, None]
    lane = np.arange(LANES)[None, :]
    ex = np.zeros((2, NSA_KV_HEADS, tk, LANES), np.float32)
    for par in range(2):
        for k in range(NSA_KV_HEADS):
            r0 = HEAD_DIM * (1 - k)
            ex[par, k] = (lane == r0 + par * (tk // SEL_BLOCK) + row // SEL_BLOCK) | (lane == r0 + 8)
    return jnp.asarray(ex, BF16)


def _sel_p(q, k2, vt, kn, sel, gates1, oc, ow, sz, tq, tk, ns):
    t = q.shape[0]
    n_half = sel.shape[2]
    ex = _sel_extra_columns(tk)
    return pl.pallas_call(
        functools.partial(_sel_p_kernel, tq=tq, tk=tk, ns=ns),
        grid=(t // tq,),
        in_specs=[pl.BlockSpec((tq, 1024), lambda i: (i, 0)), _full(k2.shape), _full(vt.shape), _full(kn.shape),
                  _full(ex.shape),
                  pl.BlockSpec((1, NSA_KV_HEADS, n_half, tq), lambda i: (i, 0, 0, 0)),
                  pl.BlockSpec((NSA_HEADS, tq), lambda i: (0, i))] + [pl.BlockSpec((tq, 512), lambda i: (i, 0))] * 3,
        out_specs=pl.BlockSpec((tq, 512), lambda i: (i, 0)),
        out_shape=jax.ShapeDtypeStruct((t, 512), BF16),
        scratch_shapes=[pltpu.VMEM((NSA_KV_HEADS, _V_ROWS, NSA_GROUP * tq), F32),
                        pltpu.VMEM((NSA_KV_HEADS, 1, NSA_GROUP * tq), F32),
                        pltpu.VMEM((NSA_KV_HEADS, 1, NSA_GROUP * tq), F32),
                        pltpu.VMEM((ns, NSA_KV_HEADS, tk, NSA_GROUP * tq), F32),
                        pltpu.VMEM((ns, NSA_KV_HEADS, tk, NSA_GROUP * tq), BF16),
                        pltpu.VMEM((ns, NSA_KV_HEADS, 1, NSA_GROUP * tq), F32),
                        pltpu.VMEM((ns, NSA_KV_HEADS, LANES, NSA_GROUP * tq), BF16),
                        pltpu.VMEM((tq, 512), F32)],
        compiler_params=_cparams(("parallel",)),
        name="sel_prompt",
    )(q, k2, vt, kn, ex, sel, gates1, oc, ow, sz)


def _win_p_kernel(q_ref, k2_ref, vt_ref, g_ref, o_ref, *, tq):
    i = pl.program_id(0)
    nk = WINDOW + tq
    qpos = i * tq + lax.broadcasted_iota(jnp.int32, (1, tq), 1)
    start = pl.multiple_of(jnp.maximum(i * tq - WINDOW, 0), tq)
    kpos = start + lax.broadcasted_iota(jnp.int32, (nk, 1), 0)
    ok = (kpos <= qpos) & (kpos > qpos - WINDOW)
    bias = jnp.where(ok, 0.0, NEG_INF)
    bias4 = jnp.concatenate([bias] * NSA_GROUP, axis=1)
    q = q_ref[...]
    ktile = k2_ref[pl.ds(start, nk), :]
    gates = g_ref[...]

    def finish(k, ot):
        og = [ot[:, g * tq:(g + 1) * tq] * gates[NSA_GROUP * k + g:NSA_GROUP * k + g + 1, :]
              for g in range(NSA_GROUP)]
        _store_heads(o_ref, og, k, tq)

    kf = ktile.astype(F32)
    kmax2 = jnp.max(jnp.sum(kf * kf, axis=1, keepdims=True), axis=0, keepdims=True)
    qts = [_stack_heads(q, k).astype(F32).T for k in range(NSA_KV_HEADS)]
    bounds = [jnp.sqrt(jnp.sum(t * t, axis=0, keepdims=True) * kmax2) * 1.02 + 1.0 for t in qts]
    safe = jnp.maximum(jnp.max(bounds[0]), jnp.max(bounds[1])) < _SAFE_BOUND

    def fast_path():
        lane = lax.broadcasted_iota(jnp.int32, (1, LANES), 1)
        rowx = lax.broadcasted_iota(jnp.int32, (LANES, 1), 0)
        for k in range(NSA_KV_HEADS):
            r0 = HEAD_DIM * (1 - k)
            own = (lane >= HEAD_DIM * k) & (lane < HEAD_DIM * (k + 1))
            lhs = jnp.where(own, ktile, jnp.where(lane == r0, 1.0, 0.0).astype(BF16))
            w = jnp.where(rowx == r0, -bounds[k], qts[k]).astype(BF16)
            p = jnp.exp2(_dot(lhs, w) + bias4).astype(BF16)
            acc = _dot(vt_ref[k, :, pl.ds(start, nk)], p)
            finish(k, acc[0:HEAD_DIM] / acc[HEAD_DIM:HEAD_DIM + 1])

    def exact_path():
        for k in range(NSA_KV_HEADS):
            s = _nt_dot(ktile, _stack_heads(q, k)) + bias4
            p = jnp.exp2(s - jnp.max(s, axis=0, keepdims=True))
            l = jnp.sum(p, axis=0, keepdims=True)
            finish(k, _dot(vt_ref[k, 0:HEAD_DIM, pl.ds(start, nk)], p.astype(BF16)) / l)

    lax.cond(safe, fast_path, exact_path)


def _win_p(q, k2, vt, gates2, tq):
    t = q.shape[0]
    return pl.pallas_call(
        functools.partial(_win_p_kernel, tq=tq),
        grid=(t // tq,),
        in_specs=[pl.BlockSpec((tq, 1024), lambda i: (i, 0)), _full(k2.shape), _full(vt.shape),
                  pl.BlockSpec((NSA_HEADS, tq), lambda i: (0, i))],
        out_specs=pl.BlockSpec((tq, 512), lambda i: (i, 0)),
        out_shape=jax.ShapeDtypeStruct((t, 512), F32),
        compiler_params=_cparams(("parallel",)),
        name="win_prompt",
    )(q, k2, vt, gates2)


def _flag_block(lane):
    r = lane % LANES
    return (lane // LANES) * (LANES // 2) + r, r < LANES // 2


def _cmp_topk_s_kernel(q_ref, ct_ref, g_ref, o_ref, sel_ref, *, tq, past, n_cmp, n_sb, n_cols):
    bs, nrow, _ = q_ref.shape
    nr8 = NSA_KV_HEADS * tq
    qpos = past + lax.broadcasted_iota(jnp.int32, (nrow, 1), 0) % tq
    col = lax.broadcasted_iota(jnp.int32, (1, n_cmp), 1)
    r = col % LANES
    blk = (col // LANES) * LANES + 2 * (r % (LANES // 2)) + r // (LANES // 2)
    mask = ((blk + 1) * CMP_BLOCK - 1) <= qpos
    imps = []
    for j in range(bs):
        qs = q_ref[j]
        ct = ct_ref[j]
        sm = jnp.where(mask, _dot(qs, ct[0:LANES].astype(BF16)), NEG_INF)
        e = jnp.exp2(sm - jnp.max(sm, axis=-1, keepdims=True))
        p = jnp.where(mask, e / jnp.sum(e, axis=-1, keepdims=True), 0.0)
        o_ref[j] = _nt_dot(p.astype(BF16), ct[LANES:2 * LANES].astype(BF16)) * g_ref[j][:, 0:1]
        imp = p[0:nr8]
        for g in range(1, NSA_GROUP):
            imp = imp + p[g * nr8:(g + 1) * nr8]
        pairs = []
        for c in range(n_cmp // LANES):
            ch = imp[:, c * LANES:(c + 1) * LANES]
            pairs.append(ch + pltpu.roll(ch, LANES // 2, 1))
        imps.append(jnp.concatenate(pairs + [jnp.zeros((nr8, n_cols - n_cmp), F32)], axis=1))
    imp = jnp.concatenate(imps, axis=0)
    qpos8 = past + lax.broadcasted_iota(jnp.int32, (bs * nr8, 1), 0) % tq
    sb, used = _flag_block(lax.broadcasted_iota(jnp.int32, (1, n_cols), 1))
    sbf = sb.astype(F32)
    score = jnp.where((sb == qpos8 // SEL_BLOCK) | (sb == 0), FORCE, imp)
    score = jnp.where(sb * SEL_BLOCK > qpos8, -FORCE, score)
    valid = used & (sb < n_sb)
    score = jnp.where(valid, score, -jnp.inf)

    def pick(_, sc):
        mx = jnp.max(sc, axis=-1, keepdims=True)
        first = jnp.min(jnp.where(sc == mx, sbf, 1e9), axis=-1, keepdims=True)
        return jnp.where(sbf == first, -jnp.inf, sc)

    sc = lax.fori_loop(0, min(N_SELECT, n_sb), pick, score)
    chosen = jnp.where(valid & (sc == -jnp.inf), 1.0, 0.0)
    for j in range(bs):
        sel_ref[j] = chosen[j * nr8:(j + 1) * nr8]


def _cmp_topk_s(qs, cmp_t, gates, tq, past, bs):
    nb, nrow, _ = qs.shape
    n_cmp = cmp_t.shape[2]
    n_sb = -(-(past + tq) // SEL_BLOCK)
    n_cols = n_cmp + LANES
    assert n_sb <= (n_cols // LANES) * (LANES // 2) and nb % bs == 0
    b3 = lambda a: pl.BlockSpec((bs,) + a.shape[1:], lambda b: (b, 0, 0))
    return pl.pallas_call(
        functools.partial(_cmp_topk_s_kernel, tq=tq, past=past, n_cmp=n_cmp, n_sb=n_sb, n_cols=n_cols),
        grid=(nb // bs,),
        in_specs=[b3(qs), b3(cmp_t), b3(gates)],
        out_specs=(pl.BlockSpec((bs, nrow, 128), lambda b: (b, 0, 0)),
                   pl.BlockSpec((bs, NSA_KV_HEADS * tq, n_cols), lambda b: (b, 0, 0))),
        out_shape=(jax.ShapeDtypeStruct((nb, nrow, 128), F32),
                   jax.ShapeDtypeStruct((nb, NSA_KV_HEADS * tq, n_cols), F32)),
        compiler_params=_cparams(("parallel",)),
        name="cmp_topk_sample",
    )(qs, cmp_t, gates)


def _selwin_s_kernel(pt_ref, q_ref, sel_ref, g_ref, *refs, pp, tq, past):
    del pt_ref
    page_refs = refs[:pp]
    newsel_ref, winbuf_ref, newwin_ref, em_ref, osel_ref, owin_ref, m_ref, l_ref, acc_ref = refs[pp:]
    g = pl.program_id(1)
    ng = pl.num_programs(1)
    qs = q_ref[0]
    nrow = qs.shape[0]
    flags = sel_ref[0]
    n_cols = flags.shape[1]
    qpos = past + lax.broadcasted_iota(jnp.int32, (nrow, 1), 0) % tq

    @pl.when(g == 0)
    def _():
        m_ref[...] = jnp.full(m_ref.shape, NEG_INF, F32)
        l_ref[...] = jnp.zeros(l_ref.shape, F32)
        acc_ref[...] = jnp.zeros(acc_ref.shape, F32)

    def update(s, ok, vts):
        sm = jnp.where(ok, s, NEG_INF)
        m_old = m_ref[...]
        m_new = jnp.maximum(m_old, jnp.max(sm, axis=-1, keepdims=True))
        alpha = jnp.exp2(m_old - m_new)
        p = jnp.where(ok, jnp.exp2(sm - m_new), 0.0)
        l_ref[...] = alpha * l_ref[...] + jnp.sum(p, axis=-1, keepdims=True)
        acc_ref[...] = alpha * acc_ref[...] + _nt_dot(p.astype(BF16), vts)
        m_ref[...] = m_new

    def expand(first_block, n):
        ncol = lax.broadcasted_iota(jnp.int32, (1, n), 1)
        sbl, used = _flag_block(lax.broadcasted_iota(jnp.int32, (n_cols, 1), 0))
        emat = ((sbl == first_block + ncol // SEL_BLOCK) & used).astype(BF16)
        mf = _dot(flags.astype(BF16), emat)
        return jnp.concatenate([mf] * NSA_GROUP, axis=0) > 0.5

    n = pp * PAGE_SIZE
    pages = [page_refs[p][0] for p in range(pp)]
    kt_all = jnp.concatenate([pg[0:LANES].astype(BF16) for pg in pages], axis=1)
    vt_all = jnp.concatenate([pg[LANES:2 * LANES].astype(BF16) for pg in pages], axis=1)
    per_chunk = (LANES // 2) * SEL_BLOCK // n
    chunk = sel_ref[0, :, pl.ds(pl.multiple_of((g // per_chunk) * LANES, LANES), LANES)]
    mf = _dot(chunk.astype(BF16), em_ref[g % per_chunk])
    kpos = g * n + lax.broadcasted_iota(jnp.int32, (1, n), 1)
    update(_dot(qs, kt_all), (jnp.concatenate([mf] * NSA_GROUP, axis=0) > 0.5) & (kpos <= qpos), vt_all)

    @pl.when(g == ng - 1)
    def _():
        new = newsel_ref[0]
        kpos_n = past + lax.broadcasted_iota(jnp.int32, (1, PAGE_SIZE), 1)
        update(_dot(qs, new[0:LANES].astype(BF16)),
               expand(past // SEL_BLOCK, PAGE_SIZE) & (kpos_n <= qpos), new[LANES:2 * LANES].astype(BF16))
        gates = g_ref[0]
        osel_ref[0] = acc_ref[...] / l_ref[...] * gates[:, 1:2]

        wb = winbuf_ref[0]
        nw = newwin_ref[0]
        n_buf = wb.shape[1]
        kw = jnp.concatenate([wb[0:LANES], nw[0:LANES]], axis=1).astype(BF16)
        vw = jnp.concatenate([wb[LANES:2 * LANES], nw[LANES:2 * LANES]], axis=1).astype(BF16)
        wpos = past - n_buf + lax.broadcasted_iota(jnp.int32, (1, n_buf + PAGE_SIZE), 1)
        okw = (wpos <= qpos) & (wpos > qpos - WINDOW) & (wpos >= 0)
        sw = jnp.where(okw, _dot(qs, kw), NEG_INF)
        e = jnp.where(okw, jnp.exp2(sw - jnp.max(sw, axis=-1, keepdims=True)), 0.0)
        owin_ref[0] = _nt_dot(e.astype(BF16), vw) / jnp.sum(e, axis=-1, keepdims=True) * gates[:, 2:3]


def _selwin_s(qs, sel, gates, pool, page_table, newsel, winbuf, newwin, tq, past, pp):
    nb, nrow, _ = qs.shape
    n_pages = page_table.shape[1]
    bmap = lambda b, g, pt: (b, 0, 0)
    b3 = lambda a: pl.BlockSpec((1,) + a.shape[1:], bmap)

    def page_spec(p):
        return pl.BlockSpec((1, 256, PAGE_SIZE), lambda b, g, pt: (pt[b * n_pages + g * pp + p], 0, 0))

    n = pp * PAGE_SIZE
    per_chunk = (LANES // 2) * SEL_BLOCK // n
    em = jnp.asarray(np.arange(LANES)[None, :, None] == (n // SEL_BLOCK) * np.arange(per_chunk)[:, None, None]
                     + np.arange(n)[None, None, :] // SEL_BLOCK, BF16)
    grid_spec = pltpu.PrefetchScalarGridSpec(
        num_scalar_prefetch=1,
        grid=(nb, n_pages // pp),
        in_specs=[b3(qs), b3(sel), b3(gates)] + [page_spec(p) for p in range(pp)]
                 + [b3(newsel), b3(winbuf), b3(newwin), pl.BlockSpec(em.shape, lambda b, g, pt: (0, 0, 0))],
        out_specs=(pl.BlockSpec((1, nrow, 128), bmap), pl.BlockSpec((1, nrow, 128), bmap)),
        scratch_shapes=[pltpu.VMEM((nrow, 1), F32), pltpu.VMEM((nrow, 1), F32), pltpu.VMEM((nrow, 128), F32)],
    )
    shp = jax.ShapeDtypeStruct((nb, nrow, 128), F32)
    return pl.pallas_call(
        functools.partial(_selwin_s_kernel, pp=pp, tq=tq, past=past),
        grid_spec=grid_spec,
        out_shape=(shp, shp),
        compiler_params=_cparams(("parallel", "arbitrary")),
        name="selwin_sample",
    )(page_table.reshape(-1), qs, sel, gates, *([pool] * pp), newsel, winbuf, newwin, em)


def _ret_kernel(rq_ref, rk_ref, rv_ref, sz_ref, s0_ref, dec_ref, qd_ref, kd_ref, gc_ref,
                o_ref, sout_ref, s_ref):
    c = pl.program_id(1)

    @pl.when(c == 0)
    def _():
        s_ref[...] = s0_ref[0]

    lane = lax.broadcasted_iota(jnp.int32, (1, LANES), 1)
    for pr in range(RET_HEADS // 2):
        cs = slice(pr * LANES, (pr + 1) * LANES)
        qc = rq_ref[:, cs]
        kc = rk_ref[:, cs]
        kb = kc.astype(BF16)
        state = s_ref[pr]
        sb = state.astype(BF16)
        new_state = gc_ref[pr] * state
        for hh in range(2):
            h = 2 * pr + hh
            hm = (lane >= RET_DK * hh) & (lane < RET_DK * (hh + 1))
            qm = jnp.where(hm, qc, 0.0)
            a = _nt_dot(qm.astype(BF16), kb) * dec_ref[h]
            v = rv_ref[:, h * RET_DV:(h + 1) * RET_DV]
            vb = v.astype(BF16)
            o = _dot(a.astype(BF16), vb) + _dot((qm * qd_ref[:, cs]).astype(BF16), sb)
            km = jnp.where(hm, kc * kd_ref[:, cs], 0.0)
            new_state = new_state + _dot(km.T.astype(BF16), vb)
            mu = jnp.mean(o, axis=-1, keepdims=True)
            var = jnp.mean(jnp.square(o - mu), axis=-1, keepdims=True)
            o_ref[:, h * RET_DV:(h + 1) * RET_DV] = ((o - mu) * lax.rsqrt(var + EPS)
                                                    * sz_ref[:, h * RET_DV:(h + 1) * RET_DV])
        s_ref[pr] = new_state
        sout_ref[0, pr] = new_state


def _retention(rq, rk, rv, szr, s0, c_real, nb, nc, cc):
    log_g = jnp.log1p(-jnp.exp2(-5.0 - jnp.arange(RET_HEADS, dtype=F32)))
    i = jnp.arange(cc, dtype=F32)
    rel = i[:, None] - i[None, :]
    dec = jnp.where(rel >= 0, jnp.exp(log_g[:, None, None] * jnp.maximum(rel, 0.0)), 0.0)
    qd = jnp.repeat(jnp.exp(log_g[None, :] * (i[:, None] + 1.0)), RET_DK, axis=1)
    kd = jnp.repeat(jnp.exp(log_g[None, :] * (c_real - 1.0 - i[:, None])), RET_DK, axis=1)
    gc = jnp.broadcast_to(jnp.repeat(jnp.exp(log_g * c_real), RET_DK).reshape(2, 128, 1), (2, 128, 128))
    row = lambda n: pl.BlockSpec((cc, n), lambda b, c: (b * nc + c, 0))
    st = pl.BlockSpec((1, 2, 128, 128), lambda b, c: (b, 0, 0, 0))
    t = rq.shape[0]
    return pl.pallas_call(
        _ret_kernel,
        grid=(nb, nc),
        in_specs=[row(256), row(256), row(512), row(512), st, _full((RET_HEADS, cc, cc)),
                  _full((cc, 256)), _full((cc, 256)), _full((2, 128, 128))],
        out_specs=(row(512), st),
        out_shape=(jax.ShapeDtypeStruct((t, 512), F32), jax.ShapeDtypeStruct((nb, 2, 128, 128), F32)),
        scratch_shapes=[pltpu.VMEM((2, 128, 128), F32)],
        compiler_params=_cparams(("parallel", "arbitrary")),
        name="retention",
    )(rq, rk, rv, szr, s0, dec, qd, kd, gc)


def _even_out_kernel(x_ref, *refs):
    mr_ref, w_ref, y_ref = refs[-3:]
    if len(refs) == 4:
        mixed = refs[0][...]
    else:
        oc_ref, os_ref, ow_ref, sz_ref = refs[:4]
        mixed = ((oc_ref[...] + os_ref[...] + ow_ref[...]) * sz_ref[...]).astype(BF16)
    y_ref[...] = (x_ref[...] + _dot(mixed, w_ref[0:512, :]) + _dot(mr_ref[...].astype(BF16), w_ref[512:1024, :]))


def _even_out(x, nsa_parts, mret, w_out, tm):
    t = x.shape[0]
    row = lambda n: pl.BlockSpec((tm, n), lambda i: (i, 0))
    return pl.pallas_call(
        _even_out_kernel,
        grid=(t // tm,),
        in_specs=[row(D_MODEL)] + [row(512)] * (len(nsa_parts) + 1) + [_full((1024, D_MODEL))],
        out_specs=row(D_MODEL),
        out_shape=jax.ShapeDtypeStruct((t, D_MODEL), F32),
        compiler_params=_cparams(("parallel",)),
        name="even_out",
    )(x, *nsa_parts, mret, w_out)


_HALO = 32
_CONV_ROWS = 16


def _rms(x, g):
    return (x * lax.rsqrt(jnp.mean(x * x, axis=-1, keepdims=True) + EPS) * g).astype(BF16)


def _glu(xn, w_ref):
    return _dot(xn, w_ref[:, 0:D_MODEL]) * _sigmoid(_dot(xn, w_ref[:, D_MODEL:2 * D_MODEL]))


def _ln_gate_out(x, c, z, lg_ref, lb_ref, wout_ref):
    mu = jnp.mean(c, axis=-1, keepdims=True)
    var = jnp.mean(jnp.square(c - mu), axis=-1, keepdims=True)
    yn = (c - mu) * lax.rsqrt(var + EPS) * lg_ref[...] + lb_ref[...]
    y = (yn * _sigmoid(yn)) * (z * _sigmoid(z))
    return x + _dot(y.astype(BF16), wout_ref[...])


def _odd_p_kernel(x_ref, xh_ref, g_ref, win_ref, cw_ref, cb_ref, lg_ref, lb_ref, wout_ref,
                  y_ref, ust_ref, u_scr, c_scr, *, tm):
    i = pl.program_id(0)
    x = x_ref[...]
    g = g_ref[...]
    xn = _rms(x, g)
    u = _glu(xn, win_ref)
    z = _dot(xn, win_ref[:, 2 * D_MODEL:3 * D_MODEL])
    uh = _glu(_rms(xh_ref[...], g), win_ref)
    u_scr[0, 0:_HALO, :] = jnp.where(i > 0, uh, 0.0)
    u_scr[0, _HALO:_HALO + tm, :] = u
    u_scr[0, _HALO + tm:_HALO + tm + 8, :] = jnp.zeros((8, D_MODEL), F32)
    ust_ref[...] = u[tm - _HALO:tm, :]
    for s in range(1, 8):
        u_scr[s, 0:_HALO + tm, :] = u_scr[0, s:s + _HALO + tm, :]
    off = _HALO - (CONV_WIDTH - 1)
    cb = cb_ref[...]
    nv = _CONV_ROWS // 8
    for rc in range(tm // _CONV_ROWS):
        r0 = rc * _CONV_ROWS
        acc = jnp.broadcast_to(cb, (_CONV_ROWS, D_MODEL)).reshape(nv, 8, D_MODEL)
        for j in range(CONV_WIDTH):
            a, s = divmod(j + off, 8)
            rows = u_scr[s, r0 + 8 * a:r0 + 8 * a + _CONV_ROWS, :]
            acc = acc + cw_ref[j][None] * rows.reshape(nv, 8, D_MODEL)
        c_scr[r0:r0 + _CONV_ROWS, :] = acc.reshape(_CONV_ROWS, D_MODEL)
    y_ref[...] = _ln_gate_out(x, c_scr[...], z, lg_ref, lb_ref, wout_ref)


def _odd_p(x, norm_g, w_in, cw, cb, lg, lb, w_out, tm):
    t = x.shape[0]
    row = pl.BlockSpec((tm, D_MODEL), lambda i: (i, 0))
    halo = pl.BlockSpec((_HALO, D_MODEL), lambda i: (jnp.maximum(i * (tm // _HALO) - 1, 0), 0))
    vec = _full((1, D_MODEL))
    return pl.pallas_call(
        functools.partial(_odd_p_kernel, tm=tm),
        grid=(t // tm,),
        in_specs=[row, halo, vec, _full((D_MODEL, 3 * D_MODEL)), _full((CONV_WIDTH, 8, D_MODEL)), vec, vec, vec,
                  _full((D_MODEL, D_MODEL))],
        out_specs=(row, _full((_HALO, D_MODEL))),
        out_shape=(jax.ShapeDtypeStruct((t, D_MODEL), F32), jax.ShapeDtypeStruct((_HALO, D_MODEL), F32)),
        scratch_shapes=[pltpu.VMEM((8, _HALO + tm + 8, D_MODEL), F32), pltpu.VMEM((tm, D_MODEL), F32)],
        compiler_params=_cparams(("arbitrary",)),
        name="odd_prompt",
    )(x, x, norm_g, w_in, cw, cb, lg, lb, w_out)


def _odd_s_kernel(x_ref, st_ref, g_ref, win_ref, cw_ref, cb_ref, lg_ref, lb_ref, wout_ref,
                  y_ref, u_ref, *, nb, tq):
    x = x_ref[...]
    xn = _rms(x, g_ref[...])
    u = _glu(xn, win_ref)
    z = _dot(xn, win_ref[:, 2 * D_MODEL:3 * D_MODEL])
    u_ref[...] = u
    n_state = CONV_WIDTH - 1
    cb = cb_ref[...]
    cs = []
    for t in range(tq):
        acc = jnp.broadcast_to(cb, (nb, D_MODEL))
        for j in range(CONV_WIDTH):
            e = t + j
            slab = st_ref[e * nb:(e + 1) * nb, :] if e < n_state else u[(e - n_state) * nb:(e - n_state + 1) * nb, :]
            acc = acc + cw_ref[j:j + 1, :] * slab
        cs.append(acc)
    y_ref[...] = _ln_gate_out(x, jnp.concatenate(cs, axis=0), z, lg_ref, lb_ref, wout_ref)


def _odd_s(x_tm, state_tm, norm_g, w_in, cw, cb, lg, lb, w_out, nb, tq):
    n = x_tm.shape[0]
    return pl.pallas_call(
        functools.partial(_odd_s_kernel, nb=nb, tq=tq),
        out_shape=(jax.ShapeDtypeStruct((n, D_MODEL), F32), jax.ShapeDtypeStruct((n, D_MODEL), F32)),
        compiler_params=pltpu.CompilerParams(vmem_limit_bytes=VMEM_LIMIT),
        name="odd_sample",
    )(x_tm, state_tm, norm_g, w_in, cw, cb, lg, lb, w_out)


def _rope_tables(pos):
    half = HEAD_DIM // 2
    freq = ROPE_THETA ** (-jnp.arange(half, dtype=F32) / half)
    ang = pos.astype(F32)[:, None] * freq[None, :]
    cos = jnp.cos(ang)
    sin = jnp.sin(ang)
    return jnp.tile(cos, (1, 4)), jnp.tile(jnp.concatenate([-sin, sin], axis=1), (1, 2))


def _even_weights(w_in):
    pq, kc, ks, kw, pg, zn, rq, rk, rv, zr = jnp.split(
        w_in, np.cumsum([512, 256, 256, 256, 24, 512, 256, 256, 512])[:].tolist(), axis=-1)
    zero = jnp.zeros((D_MODEL, HEAD_DIM), w_in.dtype)
    qcols = []
    for h in range(NSA_HEADS):
        wh = pq[:, h * HEAD_DIM:(h + 1) * HEAD_DIM]
        qcols += [wh, zero] if h < NSA_GROUP else [zero, wh]
    gpad = jnp.pad(pg, ((0, 0), (0, LANES - 3 * NSA_HEADS)))
    return jnp.concatenate(qcols + [kc, ks, kw, zn, rq, rk, rv, zr, gpad], axis=-1).astype(BF16)


def _pick_kv_half(o, nb, tq):
    o6 = o.reshape(nb, NSA_GROUP, NSA_KV_HEADS, tq, NSA_KV_HEADS, HEAD_DIM)
    o5 = jnp.stack([o6[:, :, 0, :, 0], o6[:, :, 1, :, 1]], axis=2)
    return o5.transpose(0, 3, 2, 1, 4).reshape(nb * tq, NSA_HEADS * HEAD_DIM)


def kernel(x_prompt, x_sample, cache_cmp_kv, cache_sel_kv, cache_win_kv, state_ret, state_conv, page_table,
           norm_even, w_in_even, q_norm_g, k_norm_g, w_cmp_pos, w_out_even,
           norm_odd, w_in_odd, conv_w, conv_b, ln_g, ln_b, w_out_odd):
    _, seq, _ = x_prompt.shape
    nb, tq, _ = x_sample.shape
    n_pages = page_table.shape[1]
    past = n_pages * PAGE_SIZE
    n_buf = cache_win_kv.shape[2]
    tm = 256
    tq_p = 128

    w_e = _even_weights(w_in_even[0])
    w_oe = w_out_even[0].astype(BF16)
    w_io = w_in_odd[0].astype(BF16)
    w_oo = w_out_odd[0].astype(BF16)
    gq = jnp.tile(q_norm_g[0], 2)[None, :]
    gk = jnp.tile(k_norm_g[0], (1, 2))
    w2 = jnp.repeat(w_cmp_pos[0].transpose(1, 0, 2).reshape(CMP_BLOCK, 4), HEAD_DIM, axis=1)
    ne = norm_even[0][None, :]
    no = norm_odd[0][None, :]

    xp = x_prompt[0]
    cos_p, sin_p = _rope_tables(jnp.arange(seq))
    (q_p, kvc_p, kvs_p, kvw_p, k2s_p, k2w_p, vts_p, vtw_p, gate_p, sz_p,
     rq_p, rk_p, rv_p, szr_p, kn_p) = _even_in(xp, ne, w_e, cos_p, sin_p, gq, gk, tm)
    ident = jnp.arange(seq // PAGE_SIZE, dtype=jnp.int32)[None, :]
    ce, co = _compress(kvc_p.reshape(seq // PAGE_SIZE, PAGE_SIZE, 256), ident, w2, 16)
    cmp_p = jnp.concatenate([ce[0], co[0]], axis=0)
    kc_p = cmp_p[:, :LANES].astype(BF16)
    vtc_p = cmp_p[:, LANES:].T.astype(BF16)
    gates_t = gate_p[:, :3 * NSA_HEADS].reshape(seq, NSA_HEADS, 3).transpose(2, 1, 0)
    oc_p, sel_p = _cmp_topk_p(q_p, kc_p, vtc_p, gates_t[0], tq_p)
    ones_rows = jnp.concatenate([jnp.ones((NSA_KV_HEADS, 1, seq), BF16),
                                 jnp.zeros((NSA_KV_HEADS, _V_ROWS - HEAD_DIM - 1, seq), BF16)], axis=1)
    with_ones = lambda vt: jnp.concatenate([vt.reshape(NSA_KV_HEADS, HEAD_DIM, seq), ones_rows], axis=1)
    ow_p = _win_p(q_p, k2w_p, with_ones(vtw_p), gates_t[2], tq_p)
    mix_p = _sel_p(q_p, k2s_p, with_ones(vts_p), kn_p, sel_p, gates_t[1], oc_p, ow_p, sz_p, tq_p, 256, 4)
    s0_p = jnp.zeros((1, 2, 128, 128), F32)
    ret_c = 4 * RET_CHUNK
    mret_p, s_p = _retention(rq_p, rk_p, rv_p, szr_p, s0_p, ret_c, 1, seq // ret_c, ret_c)
    y1_p = _even_out(xp, (mix_p,), mret_p, w_oe, tm)

    xs = x_sample.reshape(nb * tq, D_MODEL)
    cos_s, sin_s = _rope_tables(jnp.tile(past + jnp.arange(tq), nb))
    (q_s, kvc_s, kvs_s, kvw_s, _, _, _, _, gate_s, sz_s,
     rq_s, rk_s, rv_s, szr_s, _) = _even_in(xs, ne, w_e, cos_s, sin_s, gq, gk, nb * tq)
    rows_t = lambda c: c.transpose(0, 2, 3, 4, 1).reshape(c.shape[0], 256, c.shape[1])
    cmp_s = _compress_t(rows_t(cache_cmp_kv[0]), page_table, w_cmp_pos[0], LANES * CMP_BLOCK // PAGE_SIZE)
    qs_s = (q_s.reshape(nb, tq, NSA_KV_HEADS, NSA_GROUP, LANES).transpose(0, 3, 2, 1, 4)
            .reshape(nb, NSA_HEADS * tq, LANES))
    g_s = (gate_s[:, :3 * NSA_HEADS].reshape(nb, tq, NSA_KV_HEADS, NSA_GROUP, 3).transpose(0, 3, 2, 1, 4)
           .reshape(nb, NSA_HEADS * tq, 3))
    oc_s, sel_s = _cmp_topk_s(qs_s, cmp_s, g_s, tq, past, 4 if nb % 4 == 0 else 1)
    new_t = lambda r: jnp.pad(r.reshape(nb, tq, 256), ((0, 0), (0, PAGE_SIZE - tq), (0, 0))).transpose(0, 2, 1)
    os_s, ow_s = _selwin_s(qs_s, sel_s, g_s, rows_t(cache_sel_kv[0]), page_table,
                           new_t(kvs_s), rows_t(cache_win_kv[0]), new_t(kvw_s), tq, past, 16)
    pad_c = lambda r: jnp.pad(r.reshape(nb, tq, -1), ((0, 0), (0, RET_CHUNK - tq), (0, 0))).reshape(nb * RET_CHUNK, -1)
    mret_s, s_s = _retention(pad_c(rq_s), pad_c(rk_s), pad_c(rv_s), pad_c(szr_s),
                             state_ret[0].reshape(nb, 2, 128, 128), float(tq), nb, 1, RET_CHUNK)
    mret_s = mret_s.reshape(nb, RET_CHUNK, 512)[:, :tq].reshape(nb * tq, 512)
    y1_s = _even_out(xs, (_pick_kv_half(oc_s, nb, tq), _pick_kv_half(os_s, nb, tq), _pick_kv_half(ow_s, nb, tq),
                          sz_s), mret_s, w_oe, nb * tq)

    cb = conv_b[0][None, :]
    lg = ln_g[0][None, :]
    lb = ln_b[0][None, :]
    cw8 = jnp.broadcast_to(conv_w[0][:, None, :], (CONV_WIDTH, 8, D_MODEL))
    y2_p, ust_p = _odd_p(y1_p, no, w_io, cw8, cb, lg, lb, w_oo, tm)
    x_tm = y1_s.reshape(nb, tq, D_MODEL).transpose(1, 0, 2).reshape(tq * nb, D_MODEL)
    st_tm = state_conv[0].transpose(1, 0, 2).reshape((CONV_WIDTH - 1) * nb, D_MODEL)
    y2_tm, u_tm = _odd_s(x_tm, st_tm, no, w_io, conv_w[0], cb, lg, lb, w_oo, nb, tq)
    y2_s = y2_tm.reshape(tq, nb, D_MODEL).transpose(1, 0, 2)
    u_s = u_tm.reshape(tq, nb, D_MODEL).transpose(1, 0, 2)

    rows6 = lambda r, b, t: r.reshape(1, b, t, 2, NSA_KV_HEADS, HEAD_DIM)
    n_win = min(WINDOW, seq)
    win_s = jnp.concatenate([cache_win_kv[0].reshape(nb, n_buf, 256), kvw_s.reshape(nb, tq, 256)], axis=1)[:, -n_buf:]
    conv_s = jnp.concatenate([state_conv[0], u_s], axis=1)[:, -(CONV_WIDTH - 1):]
    return (y2_p[None], y2_s,
            rows6(kvc_p, 1, seq), rows6(kvc_s, nb, tq),
            rows6(kvs_p, 1, seq), rows6(kvs_s, nb, tq),
            rows6(kvw_p[seq - n_win:], 1, n_win), rows6(win_s, nb, n_buf),
            s_p.reshape(1, 1, RET_HEADS, RET_DK, RET_DV), s_s.reshape(1, nb, RET_HEADS, RET_DK, RET_DV),
            ust_p[_HALO - (CONV_WIDTH - 1):][None, None], conv_s[None])
```

```python
import functools

import numpy as np
import jax
import jax.numpy as jnp
from jax import lax
from jax.experimental import pallas as pl
from jax.experimental.pallas import tpu as pltpu

F32 = jnp.float32
BF16 = jnp.bfloat16

D_MODEL = 1024
PAGE_SIZE = 128
HEAD_DIM = 64
NSA_HEADS = 8
NSA_KV_HEADS = 2
NSA_GROUP = 4
CMP_BLOCK = 32
SEL_BLOCK = 64
N_SELECT = 16
WINDOW = 512
RET_HEADS = 4
RET_DK = 64
RET_DV = 128
RET_CHUNK = 128
CONV_WIDTH = 31
ROPE_THETA = 10000.0
NEG_INF = -1e30
FORCE = 1e9
EPS = 1e-6

LOG2E = 1.4426950408889634
Q_SCALE = HEAD_DIM ** -0.5 * LOG2E

LANES = 128
VMEM_LIMIT = 56 * 1024 * 1024

_QP = 0
_KV = 1024
_ZN = 1792
_RQ = 2304
_RK = 2560
_RV = 2816
_ZR = 3328
_GT = 3840
_EVEN_COLS = 3968


def _cparams(sem):
    return pltpu.CompilerParams(dimension_semantics=sem, vmem_limit_bytes=VMEM_LIMIT)


def _sigmoid(x):
    return 1.0 / (1.0 + jnp.exp(-x))


def _nt_dot(a, b):
    return lax.dot_general(a, b, (((1,), (1,)), ((), ())), preferred_element_type=F32)


def _dot(a, b):
    return jnp.dot(a, b, preferred_element_type=F32)


def _full(shape):
    n = len(shape)
    return pl.BlockSpec(shape, lambda *_: (0,) * n)


def _even_in_kernel(x_ref, g_ref, w_ref, cos_ref, sin_ref, gq_ref, gk_ref,
                    q_ref, kvc_ref, kvs_ref, kvw_ref, k2s_ref, k2w_ref, vts_ref, vtw_ref,
                    gate_ref, sz_ref, rq_ref, rk_ref, rv_ref, szr_ref, kn_ref):
    x = x_ref[...]
    ms = jnp.mean(x * x, axis=-1, keepdims=True)
    xn = (x * lax.rsqrt(ms + EPS) * g_ref[...]).astype(BF16)
    cos = cos_ref[...]
    sin = sin_ref[...]
    lane = lax.broadcasted_iota(jnp.int32, (1, LANES), 1)
    lo_half = (lane & 32) == 0
    gr = lax.broadcasted_iota(jnp.int32, (LANES, LANES), 0) // HEAD_DIM
    gc = lax.broadcasted_iota(jnp.int32, (LANES, LANES), 1) // HEAD_DIM
    gmat = (gr == gc).astype(BF16)

    def proj(a, b):
        return _dot(xn, w_ref[:, a:b])

    def head_rms(h, gain):
        h2 = h * h
        hi = h2.astype(BF16)
        lo = (h2 - hi.astype(F32)).astype(BF16)
        ss = _dot(hi, gmat) + _dot(lo, gmat)
        return h * lax.rsqrt(ss * (1.0 / HEAD_DIM) + EPS) * gain

    def rope(h):
        sw = jnp.where(lo_half, pltpu.roll(h, LANES - 32, 1), pltpu.roll(h, 32, 1))
        return h * cos + sw * sin

    hq = proj(_QP, _KV)
    gq = gq_ref[...]
    for c in range(NSA_HEADS):
        hc = hq[:, c * LANES:(c + 1) * LANES]
        q_ref[:, c * LANES:(c + 1) * LANES] = (rope(head_rms(hc, gq)) * Q_SCALE).astype(BF16)

    hk = proj(_KV, _ZN)
    kv_refs = (kvc_ref, kvs_ref, kvw_ref)
    k2_refs = (None, k2s_ref, k2w_ref)
    vt_refs = (None, vts_ref, vtw_ref)
    for b in range(3):
        k = rope(head_rms(hk[:, 2 * b * LANES:(2 * b + 1) * LANES], gk_ref[b:b + 1, :]))
        v = hk[:, (2 * b + 1) * LANES:(2 * b + 2) * LANES]
        kv_refs[b][:, 0:LANES] = k
        kv_refs[b][:, LANES:2 * LANES] = v
        if k2_refs[b] is not None:
            kb = k.astype(BF16)
            k2_refs[b][...] = kb
            vt = v.T.astype(BF16)
            tail = jnp.where(lax.broadcasted_iota(jnp.int32, (_V_ROWS - HEAD_DIM, 1), 0) == 0, 1.0, 0.0)
            tail = jnp.broadcast_to(tail, (_V_ROWS - HEAD_DIM, vt.shape[1])).astype(BF16)
            for kh in range(NSA_KV_HEADS):
                vt_refs[b][kh, 0:HEAD_DIM, :] = vt[HEAD_DIM * kh:HEAD_DIM * (kh + 1), :]
                vt_refs[b][kh, HEAD_DIM:_V_ROWS, :] = tail
            if b == 1:
                kf = kb.astype(F32)
                k2 = kf * kf
                hi = k2.astype(BF16)
                n2 = _dot(hi, gmat) + _dot((k2 - hi.astype(F32)).astype(BF16), gmat)
                kn_ref[0] = jnp.broadcast_to(jnp.max(n2, axis=0, keepdims=True), (8, LANES)) * 1.001

    hz = proj(_ZN, _RQ)
    sz_ref[...] = hz * _sigmoid(hz)
    hrq = proj(_RQ, _RK)
    hrk = proj(_RK, _RV)
    for c in range(2):
        rq_ref[:, c * LANES:(c + 1) * LANES] = rope(hrq[:, c * LANES:(c + 1) * LANES])
        rk_ref[:, c * LANES:(c + 1) * LANES] = rope(hrk[:, c * LANES:(c + 1) * LANES]) * (RET_DK ** -0.5)
    rv_ref[...] = proj(_RV, _ZR)
    hzr = proj(_ZR, _GT)
    szr_ref[...] = hzr * _sigmoid(hzr)
    gate_ref[...] = _sigmoid(proj(_GT, _EVEN_COLS))


def _even_in(x, norm_g, w_e, cos, sin, gq, gk, tm):
    t = x.shape[0]
    row = lambda n: pl.BlockSpec((tm, n), lambda i: (i, 0))
    out_shape = (
        jax.ShapeDtypeStruct((t, 1024), BF16),
        jax.ShapeDtypeStruct((t, 256), F32),
        jax.ShapeDtypeStruct((t, 256), F32),
        jax.ShapeDtypeStruct((t, 256), F32),
        jax.ShapeDtypeStruct((t, 128), BF16),
        jax.ShapeDtypeStruct((t, 128), BF16),
        jax.ShapeDtypeStruct((NSA_KV_HEADS, _V_ROWS, t), BF16),
        jax.ShapeDtypeStruct((NSA_KV_HEADS, _V_ROWS, t), BF16),
        jax.ShapeDtypeStruct((t, 128), F32),
        jax.ShapeDtypeStruct((t, 512), F32),
        jax.ShapeDtypeStruct((t, 256), F32),
        jax.ShapeDtypeStruct((t, 256), F32),
        jax.ShapeDtypeStruct((t, 512), F32),
        jax.ShapeDtypeStruct((t, 512), F32),
        jax.ShapeDtypeStruct((t // tm, 8, 128), F32),
    )
    vt_spec = pl.BlockSpec((NSA_KV_HEADS, _V_ROWS, tm), lambda i: (0, 0, i))
    out_specs = (row(1024), row(256), row(256), row(256), row(128), row(128), vt_spec, vt_spec,
                 row(128), row(512), row(256), row(256), row(512), row(512),
                 pl.BlockSpec((1, 8, 128), lambda i: (i, 0, 0)))
    return pl.pallas_call(
        _even_in_kernel,
        grid=(t // tm,),
        in_specs=[row(D_MODEL), _full((1, D_MODEL)), _full((D_MODEL, _EVEN_COLS)),
                  row(128), row(128), _full((1, 128)), _full((3, 128))],
        out_specs=out_specs,
        out_shape=out_shape,
        compiler_params=_cparams(("parallel",)),
        name="even_in",
    )(x, norm_g, w_e, cos, sin, gq, gk)


def _compress_kernel(pt_ref, *refs, pp):
    del pt_ref
    page_refs = refs[:pp]
    w_ref, oe_ref, oo_ref = refs[pp:]
    w = w_ref[...]
    for p in range(pp):
        x = page_refs[p][0]
        s = jnp.sum(x.reshape(PAGE_SIZE // CMP_BLOCK, CMP_BLOCK, 256) * w[None], axis=1)
        oe_ref[0, 2 * p:2 * p + 1, :] = s[0:1]
        oo_ref[0, 2 * p:2 * p + 1, :] = s[1:2]
        oe_ref[0, 2 * p + 1:2 * p + 2, :] = s[2:3]
        oo_ref[0, 2 * p + 1:2 * p + 2, :] = s[3:4]


def _compress(pool, page_table, w2, pp):
    nb, n_pages = page_table.shape

    def page_spec(p):
        return pl.BlockSpec((1, PAGE_SIZE, 256), lambda b, g, pt: (pt[b * n_pages + g * pp + p], 0, 0))

    out_spec = pl.BlockSpec((1, 2 * pp, 256), lambda b, g, pt: (b, g, 0))
    grid_spec = pltpu.PrefetchScalarGridSpec(
        num_scalar_prefetch=1,
        grid=(nb, n_pages // pp),
        in_specs=[page_spec(p) for p in range(pp)] + [pl.BlockSpec((CMP_BLOCK, 256), lambda b, g, pt: (0, 0))],
        out_specs=(out_spec, out_spec),
    )
    shp = jax.ShapeDtypeStruct((nb, 2 * n_pages, 256), F32)
    return pl.pallas_call(
        functools.partial(_compress_kernel, pp=pp),
        grid_spec=grid_spec,
        out_shape=(shp, shp),
        compiler_params=_cparams(("parallel", "parallel")),
        name="compress",
    )(page_table.reshape(-1), *([pool] * pp), w2)


def _cmp_lane(w):
    return (w % 2) * (LANES // 2) + w // 2


def _compress_t_kernel(pt_ref, *refs, pp):
    del pt_ref
    page_refs = refs[:pp]
    w_ref, g_ref, o_ref = refs[pp:]
    w = w_ref[...]
    prods = [(page_refs[p][0] * w).astype(BF16) for p in range(pp)]
    o_ref[0] = _dot(jnp.concatenate(prods, axis=1), g_ref[...])


def _compress_t(pool_t, page_table, w_pos, pp):
    nb, n_pages = page_table.shape
    per_page = PAGE_SIZE // CMP_BLOCK
    assert pp * per_page == LANES
    wf = jnp.tile(jnp.broadcast_to(w_pos.transpose(0, 2, 1)[:, :, None, :],
                                   (2, NSA_KV_HEADS, HEAD_DIM, CMP_BLOCK)).reshape(256, CMP_BLOCK), (1, per_page))
    r = jnp.arange(pp * PAGE_SIZE)
    gm = (_cmp_lane(r // CMP_BLOCK)[:, None] == jnp.arange(LANES)[None, :]).astype(BF16)

    def page_spec(p):
        return pl.BlockSpec((1, 256, PAGE_SIZE), lambda b, g, pt: (pt[b * n_pages + g * pp + p], 0, 0))

    grid_spec = pltpu.PrefetchScalarGridSpec(
        num_scalar_prefetch=1,
        grid=(nb, n_pages // pp),
        in_specs=[page_spec(p) for p in range(pp)]
                 + [pl.BlockSpec((256, PAGE_SIZE), lambda b, g, pt: (0, 0)),
                    pl.BlockSpec((pp * PAGE_SIZE, LANES), lambda b, g, pt: (0, 0))],
        out_specs=pl.BlockSpec((1, 256, LANES), lambda b, g, pt: (b, 0, g)),
    )
    return pl.pallas_call(
        functools.partial(_compress_t_kernel, pp=pp),
        grid_spec=grid_spec,
        out_shape=jax.ShapeDtypeStruct((nb, 256, per_page * n_pages), F32),
        compiler_params=_cparams(("parallel", "parallel")),
        name="compress_sample",
    )(page_table.reshape(-1), *([pool_t] * pp), wf, gm)


def _stack_heads(q, k):
    return jnp.concatenate([q[:, (NSA_GROUP * k + g) * LANES:(NSA_GROUP * k + g + 1) * LANES]
                            for g in range(NSA_GROUP)], axis=0)


def _store_heads(o_ref, og, k, tq):
    for j in range(2):
        blk = jnp.concatenate([og[2 * j], og[2 * j + 1]], axis=0)
        o_ref[:, (2 * k + j) * LANES:(2 * k + j + 1) * LANES] = blk.T


_CMP_VARIANTS = 4


def _cmp_topk_p_kernel(q_ref, kc_ref, vtc_ref, g_ref, o_ref, sel_ref, sc_ref, *, tq, n_half, nq):
    i = pl.program_id(0)
    qpos = i * tq + lax.broadcasted_iota(jnp.int32, (1, tq), 1)
    sb = lax.broadcasted_iota(jnp.int32, (n_half, 1), 0)
    sbf = sb.astype(F32)
    forced = (sb == qpos // SEL_BLOCK) | (sb == 0)
    future = sb * SEL_BLOCK > qpos
    q = q_ref[...]
    gates = g_ref[...]

    def attend(nh):
        r = lax.broadcasted_iota(jnp.int32, (2 * nh, 1), 0)
        blk = jnp.where(r < nh, 2 * r, 2 * (r - nh) + 1)
        mask = ((blk + 1) * CMP_BLOCK - 1) <= qpos
        kc = jnp.concatenate([kc_ref[0:nh, :], kc_ref[n_half:n_half + nh, :]], axis=0)
        for k in range(NSA_KV_HEADS):
            s = _nt_dot(kc, _stack_heads(q, k))
            imp = jnp.zeros((2 * nh, tq), F32)
            ps = []
            for g in range(NSA_GROUP):
                sg = jnp.where(mask, s[:, g * tq:(g + 1) * tq], NEG_INF)
                e = jnp.exp2(sg - jnp.max(sg, axis=0, keepdims=True))
                p = jnp.where(mask, e / jnp.sum(e, axis=0, keepdims=True), 0.0)
                imp = imp + p
                ps.append(p.astype(BF16))
            pt = jnp.concatenate(ps, axis=1)
            vk = slice(HEAD_DIM * k, HEAD_DIM * (k + 1))
            ot = (_dot(vtc_ref[vk, 0:nh], pt[0:nh]) + _dot(vtc_ref[vk, n_half:n_half + nh], pt[nh:2 * nh]))
            og = [ot[:, g * tq:(g + 1) * tq] * gates[NSA_GROUP * k + g:NSA_GROUP * k + g + 1, :]
                  for g in range(NSA_GROUP)]
            _store_heads(o_ref, og, k, tq)
            imp2 = imp[:nh] + imp[nh:]
            if nh < n_half:
                imp2 = jnp.concatenate([imp2, jnp.zeros((n_half - nh, tq), F32)], axis=0)
            sc_ref[k] = jnp.where(forced, -jnp.inf, jnp.where(future, -FORCE, imp2))

    per_var = nq // _CMP_VARIANTS
    for c in range(_CMP_VARIANTS):
        pl.when(i // per_var == c)(functools.partial(attend, (c + 1) * n_half // _CMP_VARIANTS))

    n_forced = 1 + (qpos >= SEL_BLOCK).astype(jnp.int32)
    need = min(N_SELECT, n_half) - n_forced

    def pick(rnd, scs):
        out = []
        for sc in scs:
            mx = jnp.max(sc, axis=0, keepdims=True)
            first = jnp.min(jnp.where(sc == mx, sbf, 1e9), axis=0, keepdims=True)
            out.append(jnp.where((sbf == first) & (rnd < need), -jnp.inf, sc))
        return tuple(out)

    n_rounds = min(N_SELECT, n_half) - 1 - (i > 0).astype(jnp.int32)
    scs = lax.fori_loop(0, n_rounds, pick, (sc_ref[0], sc_ref[1]))
    for k in range(NSA_KV_HEADS):
        sel_ref[0, k] = jnp.where(scs[k] == -jnp.inf, 0.0, NEG_INF)


def _cmp_topk_p(q, kc, vtc, gates0, tq):
    t = q.shape[0]
    n_half = kc.shape[0] // 2
    nq = t // tq
    assert nq % _CMP_VARIANTS == 0 and (n_half // _CMP_VARIANTS) % 8 == 0
    assert n_half * SEL_BLOCK >= t
    return pl.pallas_call(
        functools.partial(_cmp_topk_p_kernel, tq=tq, n_half=n_half, nq=nq),
        grid=(nq,),
        in_specs=[pl.BlockSpec((tq, 1024), lambda i: (i, 0)), _full(kc.shape), _full(vtc.shape),
                  pl.BlockSpec((NSA_HEADS, tq), lambda i: (0, i))],
        out_specs=(pl.BlockSpec((tq, 512), lambda i: (i, 0)),
                   pl.BlockSpec((1, NSA_KV_HEADS, n_half, tq), lambda i: (i, 0, 0, 0))),
        out_shape=(jax.ShapeDtypeStruct((t, 512), F32),
                   jax.ShapeDtypeStruct((nq, NSA_KV_HEADS, n_half, tq), F32)),
        scratch_shapes=[pltpu.VMEM((NSA_KV_HEADS, n_half, tq), F32)],
        compiler_params=_cparams(("parallel",)),
        name="cmp_topk_prompt",
    )(q, kc, vtc, gates0)


_V_ROWS = HEAD_DIM + 16
_X_ROWS = 16
_SAFE_BOUND = 60.0


def _sel_p_kernel(q_ref, k2_ref, vt_ref, kn_ref, ex_ref, sel_ref, g_ref, oc_ref, ow_ref, sz_ref, o_ref,
                  acc_ref, m_ref, l_ref, s_ref, p_ref, a_ref, qt_ref, os_ref, *, tq, tk, ns):
    i = pl.program_id(0)
    qpos = i * tq + lax.broadcasted_iota(jnp.int32, (1, tq), 1)
    q = q_ref[...]
    qts = [_stack_heads(q, k).astype(F32).T for k in range(NSA_KV_HEADS)]
    nblk = tk // SEL_BLOCK
    assert 2 * nblk == 8 and tk % tq == 0 and ns % 2 == 0 and (k2_ref.shape[0] // tk) % ns == 0
    n_half = sel_ref.shape[2]
    kiota = lax.broadcasted_iota(jnp.int32, (tk, 1), 0)
    n_tiles = (i * tq + tq + tk - 1) // tk
    n_steps = (n_tiles + ns - 1) // ns
    max_tile = k2_ref.shape[0] // tk - 1
    gates = g_ref[...]

    def finish(numer, denom):
        for k in range(NSA_KV_HEADS):
            ot = numer(k) / denom(k)
            og = [ot[:, g * tq:(g + 1) * tq] * gates[NSA_GROUP * k + g:NSA_GROUP * k + g + 1, :]
                  for g in range(NSA_GROUP)]
            _store_heads(os_ref, og, k, tq)
        o_ref[...] = ((oc_ref[...] + os_ref[...] + ow_ref[...]) * sz_ref[...]).astype(BF16)

    kn = jnp.max(kn_ref[...].reshape(-1, LANES), axis=0, keepdims=True)
    bounds = []
    for k in range(NSA_KV_HEADS):
        kmax2 = jnp.max(kn[:, HEAD_DIM * k:HEAD_DIM * (k + 1)], axis=1, keepdims=True)
        qn2 = jnp.sum(qts[k] * qts[k], axis=0, keepdims=True)
        bounds.append(jnp.sqrt(qn2 * kmax2) * 1.02 + 1.0)
    safe = jnp.maximum(jnp.max(bounds[0]), jnp.max(bounds[1])) < _SAFE_BOUND
    lane = lax.broadcasted_iota(jnp.int32, (1, LANES), 1)
    row8 = lax.broadcasted_iota(jnp.int32, (_X_ROWS - 8, 1), 0)
    tails = [jnp.where(row8 == 0, -b, 0.0) for b in bounds]

    def f_scores(j, s1):
        slot = s1 % ns
        kt = jnp.minimum(ns * j + s1, max_tile)
        ktile = k2_ref[pl.ds(pl.multiple_of(kt * tk, tk), tk), :]
        grp = pl.multiple_of(jnp.minimum((j * (ns // 2) + s1 // 2) * 8, n_half - 8), 8)
        for k in range(NSA_KV_HEADS):
            r0 = HEAD_DIM * (1 - k)
            rows8 = jnp.concatenate([sel_ref[0, k, pl.ds(grp, 8), :]] * NSA_GROUP, axis=1)
            qt_ref[slot, k, r0:r0 + _X_ROWS, :] = jnp.concatenate([rows8, tails[k]], axis=0).astype(BF16)
            own = (lane >= HEAD_DIM * k) & (lane < HEAD_DIM * (k + 1))
            lhs = jnp.where(own, ktile, ex_ref[s1 % 2, k])
            s_ref[slot, k] = _dot(lhs, qt_ref[slot, k])

    def f_values(kt, slot):
        kt = jnp.clip(kt, 0, max_tile)
        for k in range(NSA_KV_HEADS):
            acc_ref[k] = acc_ref[k] + _dot(vt_ref[k, :, pl.ds(pl.multiple_of(kt * tk, tk), tk)], p_ref[slot, k])

    def f_softmax(j, slot, masked):
        visible = (ns * j + slot) * tk + kiota <= qpos
        for k in range(NSA_KV_HEADS):
            for g in range(NSA_GROUP):
                cs = slice(g * tq, (g + 1) * tq)
                p = jnp.exp2(s_ref[slot, k, :, cs])
                if masked:
                    p = jnp.where(visible, p, 0.0)
                p_ref[slot, k, :, cs] = p.astype(BF16)

    def f_step(j, masked):
        for s in range(ns):
            f_softmax(j, s, masked)
            f_values(ns * j + s - 1, (s - 1) % ns)
            f_scores(j, s + 1)

    def fast_path():
        for k in range(NSA_KV_HEADS):
            w = qts[k].astype(BF16)
            for slot in range(ns):
                qt_ref[slot, k] = w
        acc_ref[...] = jnp.zeros(acc_ref.shape, F32)
        p_ref[...] = jnp.zeros(p_ref.shape, BF16)
        f_scores(0, 0)

        def body(j, carry):
            f_step(j, False)
            return carry

        lax.fori_loop(0, n_steps - 1, body, 0)
        f_step(n_steps - 1, True)
        f_values(ns * n_steps - 1, ns - 1)
        finish(lambda k: acc_ref[k, 0:HEAD_DIM], lambda k: acc_ref[k, HEAD_DIM:HEAD_DIM + 1])

    def scores(kt, slot):
        kt = jnp.minimum(kt, max_tile)
        ktile = k2_ref[pl.ds(pl.multiple_of(kt * tk, tk), tk), :]
        for k in range(NSA_KV_HEADS):
            s_ref[slot, k] = _dot(ktile, qt_ref[0, k])

    def values(kt, slot):
        kt = jnp.clip(kt, 0, max_tile)
        for k in range(NSA_KV_HEADS):
            vtile = vt_ref[k, 0:HEAD_DIM, pl.ds(pl.multiple_of(kt * tk, tk), tk)]
            acc_ref[k, 0:HEAD_DIM] = acc_ref[k, 0:HEAD_DIM] * a_ref[slot, k] + _dot(vtile, p_ref[slot, k])

    def softmax(j, slot):
        visible = (ns * j + slot) * tk + kiota <= qpos
        grp = pl.multiple_of((j * (ns // 2) + slot // 2) * 8, 8)
        for k in range(NSA_KV_HEADS):
            rows = sel_ref[0, k, pl.ds(grp, 8), :][(slot % 2) * nblk:(slot % 2 + 1) * nblk]
            bias = jnp.concatenate([jnp.broadcast_to(rows[r:r + 1, :], (SEL_BLOCK, tq)) for r in range(nblk)], axis=0)
            bias = jnp.where(visible, bias, NEG_INF)
            for g in range(NSA_GROUP):
                cs = slice(g * tq, (g + 1) * tq)
                sg = s_ref[slot, k, :, cs] + bias
                m_old = m_ref[k, :, cs]
                m_new = jnp.maximum(m_old, jnp.max(sg, axis=0, keepdims=True))
                alpha = jnp.exp2(m_old - m_new)
                p = jnp.exp2(sg - m_new)
                l_ref[k, :, cs] = alpha * l_ref[k, :, cs] + jnp.sum(p, axis=0, keepdims=True)
                m_ref[k, :, cs] = m_new
                a_ref[slot, k, :, cs] = alpha
                p_ref[slot, k, :, cs] = p.astype(BF16)

    def body(j, carry):
        for s in range(ns):
            scores(ns * j + s + 1, (s + 1) % ns)
            values(ns * j + s - 1, (s - 1) % ns)
            softmax(j, s)
        return carry

    def exact_path():
        for k in range(NSA_KV_HEADS):
            qt_ref[0, k] = qts[k].astype(BF16)
        m_ref[...] = jnp.full(m_ref.shape, NEG_INF, F32)
        l_ref[...] = jnp.zeros(l_ref.shape, F32)
        acc_ref[...] = jnp.zeros(acc_ref.shape, F32)
        p_ref[...] = jnp.zeros(p_ref.shape, BF16)
        a_ref[...] = jnp.ones(a_ref.shape, F32)
        scores(0, 0)
        lax.fori_loop(0, n_steps, body, 0)
        values(ns * n_steps - 1, ns - 1)
        finish(lambda k: acc_ref[k, 0:HEAD_DIM], lambda k: l_ref[k])

    lax.cond(safe, fast_path, exact_path)


def _sel_extra_columns(tk):
    row = np.arange(tk)[:, None]
    lane = np.arange(LANES)[None, :]
    ex = np.zeros((2, NSA_KV_HEADS, tk, LANES), np.float32)
    for par in range(2):
        for k in range(NSA_KV_HEADS):
            r0 = HEAD_DIM * (1 - k)
            ex[par, k] = (lane == r0 + par * (tk // SEL_BLOCK) + row // SEL_BLOCK) | (lane == r0 + 8)
    return jnp.asarray(ex, BF16)


def _sel_p(q, k2, vt, kn, sel, gates1, oc, ow, sz, tq, tk, ns):
    t = q.shape[0]
    n_half = sel.shape[2]
    ex = _sel_extra_columns(tk)
    return pl.pallas_call(
        functools.partial(_sel_p_kernel, tq=tq, tk=tk, ns=ns),
        grid=(t // tq,),
        in_specs=[pl.BlockSpec((tq, 1024), lambda i: (i, 0)), _full(k2.shape), _full(vt.shape), _full(kn.shape),
                  _full(ex.shape),
                  pl.BlockSpec((1, NSA_KV_HEADS, n_half, tq), lambda i: (i, 0, 0, 0)),
                  pl.BlockSpec((NSA_HEADS, tq), lambda i: (0, i))] + [pl.BlockSpec((tq, 512), lambda i: (i, 0))] * 3,
        out_specs=pl.BlockSpec((tq, 512), lambda i: (i, 0)),
        out_shape=jax.ShapeDtypeStruct((t, 512), BF16),
        scratch_shapes=[pltpu.VMEM((NSA_KV_HEADS, _V_ROWS, NSA_GROUP * tq), F32),
                        pltpu.VMEM((NSA_KV_HEADS, 1, NSA_GROUP * tq), F32),
                        pltpu.VMEM((NSA_KV_HEADS, 1, NSA_GROUP * tq), F32),
                        pltpu.VMEM((ns, NSA_KV_HEADS, tk, NSA_GROUP * tq), F32),
                        pltpu.VMEM((ns, NSA_KV_HEADS, tk, NSA_GROUP * tq), BF16),
                        pltpu.VMEM((ns, NSA_KV_HEADS, 1, NSA_GROUP * tq), F32),
                        pltpu.VMEM((ns, NSA_KV_HEADS, LANES, NSA_GROUP * tq), BF16),
                        pltpu.VMEM((tq, 512), F32)],
        compiler_params=_cparams(("parallel",)),
        name="sel_prompt",
    )(q, k2, vt, kn, ex, sel, gates1, oc, ow, sz)


def _win_p_kernel(q_ref, k2_ref, vt_ref, g_ref, o_ref, *, tq):
    i = pl.program_id(0)
    nk = WINDOW + tq
    qpos = i * tq + lax.broadcasted_iota(jnp.int32, (1, tq), 1)
    start = pl.multiple_of(jnp.maximum(i * tq - WINDOW, 0), tq)
    kpos = start + lax.broadcasted_iota(jnp.int32, (nk, 1), 0)
    ok = (kpos <= qpos) & (kpos > qpos - WINDOW)
    bias = jnp.where(ok, 0.0, NEG_INF)
    bias4 = jnp.concatenate([bias] * NSA_GROUP, axis=1)
    q = q_ref[...]
    ktile = k2_ref[pl.ds(start, nk), :]
    gates = g_ref[...]

    def finish(k, ot):
        og = [ot[:, g * tq:(g + 1) * tq] * gates[NSA_GROUP * k + g:NSA_GROUP * k + g + 1, :]
              for g in range(NSA_GROUP)]
        _store_heads(o_ref, og, k, tq)

    kf = ktile.astype(F32)
    kmax2 = jnp.max(jnp.sum(kf * kf, axis=1, keepdims=True), axis=0, keepdims=True)
    qts = [_stack_heads(q, k).astype(F32).T for k in range(NSA_KV_HEADS)]
    bounds = [jnp.sqrt(jnp.sum(t * t, axis=0, keepdims=True) * kmax2) * 1.02 + 1.0 for t in qts]
    safe = jnp.maximum(jnp.max(bounds[0]), jnp.max(bounds[1])) < _SAFE_BOUND

    def fast_path():
        lane = lax.broadcasted_iota(jnp.int32, (1, LANES), 1)
        rowx = lax.broadcasted_iota(jnp.int32, (LANES, 1), 0)
        for k in range(NSA_KV_HEADS):
            r0 = HEAD_DIM * (1 - k)
            own = (lane >= HEAD_DIM * k) & (lane < HEAD_DIM * (k + 1))
            lhs = jnp.where(own, ktile, jnp.where(lane == r0, 1.0, 0.0).astype(BF16))
            w = jnp.where(rowx == r0, -bounds[k], qts[k]).astype(BF16)
            p = jnp.exp2(_dot(lhs, w) + bias4).astype(BF16)
            acc = _dot(vt_ref[k, :, pl.ds(start, nk)], p)
            finish(k, acc[0:HEAD_DIM] / acc[HEAD_DIM:HEAD_DIM + 1])

    def exact_path():
        for k in range(NSA_KV_HEADS):
            s = _nt_dot(ktile, _stack_heads(q, k)) + bias4
            p = jnp.exp2(s - jnp.max(s, axis=0, keepdims=True))
            l = jnp.sum(p, axis=0, keepdims=True)
            finish(k, _dot(vt_ref[k, 0:HEAD_DIM, pl.ds(start, nk)], p.astype(BF16)) / l)

    lax.cond(safe, fast_path, exact_path)


def _win_p(q, k2, vt, gates2, tq):
    t = q.shape[0]
    return pl.pallas_call(
        functools.partial(_win_p_kernel, tq=tq),
        grid=(t // tq,),
        in_specs=[pl.BlockSpec((tq, 1024), lambda i: (i, 0)), _full(k2.shape), _full(vt.shape),
                  pl.BlockSpec((NSA_HEADS, tq), lambda i: (0, i))],
        out_specs=pl.BlockSpec((tq, 512), lambda i: (i, 0)),
        out_shape=jax.ShapeDtypeStruct((t, 512), F32),
        compiler_params=_cparams(("parallel",)),
        name="win_prompt",
    )(q, k2, vt, gates2)


def _flag_block(lane):
    r = lane % LANES
    return (lane // LANES) * (LANES // 2) + r, r < LANES // 2


def _cmp_topk_s_kernel(q_ref, ct_ref, g_ref, o_ref, sel_ref, *, tq, past, n_cmp, n_sb, n_cols):
    bs, nrow, _ = q_ref.shape
    nr8 = NSA_KV_HEADS * tq
    qpos = past + lax.broadcasted_iota(jnp.int32, (nrow, 1), 0) % tq
    col = lax.broadcasted_iota(jnp.int32, (1, n_cmp), 1)
    r = col % LANES
    blk = (col // LANES) * LANES + 2 * (r % (LANES // 2)) + r // (LANES // 2)
    mask = ((blk + 1) * CMP_BLOCK - 1) <= qpos
    imps = []
    for j in range(bs):
        qs = q_ref[j]
        ct = ct_ref[j]
        sm = jnp.where(mask, _dot(qs, ct[0:LANES].astype(BF16)), NEG_INF)
        e = jnp.exp2(sm - jnp.max(sm, axis=-1, keepdims=True))
        p = jnp.where(mask, e / jnp.sum(e, axis=-1, keepdims=True), 0.0)
        o_ref[j] = _nt_dot(p.astype(BF16), ct[LANES:2 * LANES].astype(BF16)) * g_ref[j][:, 0:1]
        imp = p[0:nr8]
        for g in range(1, NSA_GROUP):
            imp = imp + p[g * nr8:(g + 1) * nr8]
        pairs = []
        for c in range(n_cmp // LANES):
            ch = imp[:, c * LANES:(c + 1) * LANES]
            pairs.append(ch + pltpu.roll(ch, LANES // 2, 1))
        imps.append(jnp.concatenate(pairs + [jnp.zeros((nr8, n_cols - n_cmp), F32)], axis=1))
    imp = jnp.concatenate(imps, axis=0)
    qpos8 = past + lax.broadcasted_iota(jnp.int32, (bs * nr8, 1), 0) % tq
    sb, used = _flag_block(lax.broadcasted_iota(jnp.int32, (1, n_cols), 1))
    sbf = sb.astype(F32)
    score = jnp.where((sb == qpos8 // SEL_BLOCK) | (sb == 0), FORCE, imp)
    score = jnp.where(sb * SEL_BLOCK > qpos8, -FORCE, score)
    valid = used & (sb < n_sb)
    score = jnp.where(valid, score, -jnp.inf)

    def pick(_, sc):
        mx = jnp.max(sc, axis=-1, keepdims=True)
        first = jnp.min(jnp.where(sc == mx, sbf, 1e9), axis=-1, keepdims=True)
        return jnp.where(sbf == first, -jnp.inf, sc)

    sc = lax.fori_loop(0, min(N_SELECT, n_sb), pick, score)
    chosen = jnp.where(valid & (sc == -jnp.inf), 1.0, 0.0)
    for j in range(bs):
        sel_ref[j] = chosen[j * nr8:(j + 1) * nr8]


def _cmp_topk_s(qs, cmp_t, gates, tq, past, bs):
    nb, nrow, _ = qs.shape
    n_cmp = cmp_t.shape[2]
    n_sb = -(-(past + tq) // SEL_BLOCK)
    n_cols = n_cmp + LANES
    assert n_sb <= (n_cols // LANES) * (LANES // 2) and nb % bs == 0
    b3 = lambda a: pl.BlockSpec((bs,) + a.shape[1:], lambda b: (b, 0, 0))
    return pl.pallas_call(
        functools.partial(_cmp_topk_s_kernel, tq=tq, past=past, n_cmp=n_cmp, n_sb=n_sb, n_cols=n_cols),
        grid=(nb // bs,),
        in_specs=[b3(qs), b3(cmp_t), b3(gates)],
        out_specs=(pl.BlockSpec((bs, nrow, 128), lambda b: (b, 0, 0)),
                   pl.BlockSpec((bs, NSA_KV_HEADS * tq, n_cols), lambda b: (b, 0, 0))),
        out_shape=(jax.ShapeDtypeStruct((nb, nrow, 128), F32),
                   jax.ShapeDtypeStruct((nb, NSA_KV_HEADS * tq, n_cols), F32)),
        compiler_params=_cparams(("parallel",)),
        name="cmp_topk_sample",
    )(qs, cmp_t, gates)


def _selwin_s_kernel(pt_ref, q_ref, sel_ref, g_ref, *refs, pp, tq, past):
    del pt_ref
    page_refs = refs[:pp]
    newsel_ref, winbuf_ref, newwin_ref, em_ref, osel_ref, owin_ref, m_ref, l_ref, acc_ref = refs[pp:]
    g = pl.program_id(1)
    ng = pl.num_programs(1)
    qs = q_ref[0]
    nrow = qs.shape[0]
    flags = sel_ref[0]
    n_cols = flags.shape[1]
    qpos = past + lax.broadcasted_iota(jnp.int32, (nrow, 1), 0) % tq

    @pl.when(g == 0)
    def _():
        m_ref[...] = jnp.full(m_ref.shape, NEG_INF, F32)
        l_ref[...] = jnp.zeros(l_ref.shape, F32)
        acc_ref[...] = jnp.zeros(acc_ref.shape, F32)

    def update(s, ok, vts):
        sm = s if ok is None else jnp.where(ok, s, NEG_INF)
        m_old = m_ref[...]
        m_new = jnp.maximum(m_old, jnp.max(sm, axis=-1, keepdims=True))
        alpha = jnp.exp2(m_old - m_new)
        p = jnp.exp2(sm - m_new)
        if ok is not None:
            p = jnp.where(ok, p, 0.0)
        l_ref[...] = alpha * l_ref[...] + jnp.sum(p, axis=-1, keepdims=True)
        acc_ref[...] = alpha * acc_ref[...] + _nt_dot(p.astype(BF16), vts)
        m_ref[...] = m_new

    def expand(first_block, n):
        ncol = lax.broadcasted_iota(jnp.int32, (1, n), 1)
        sbl, used = _flag_block(lax.broadcasted_iota(jnp.int32, (n_cols, 1), 0))
        emat = ((sbl == first_block + ncol // SEL_BLOCK) & used).astype(BF16)
        mf = _dot(flags.astype(BF16), emat)
        return jnp.concatenate([mf] * NSA_GROUP, axis=0) > 0.5

    n = pp * PAGE_SIZE
    pages = [page_refs[p][0] for p in range(pp)]
    kt_all = jnp.concatenate([pg[0:LANES].astype(BF16) for pg in pages], axis=1)
    vt_all = jnp.concatenate([pg[LANES:2 * LANES].astype(BF16) for pg in pages], axis=1)
    nsb = n // SEL_BLOCK
    chunk = sel_ref[0, :, pl.ds(pl.multiple_of((g // 2) * LANES, LANES), LANES)]
    cb = jnp.where(chunk > 0.5, 0.0, NEG_INF)
    cb = jnp.where(g % 2 == 0, cb, pltpu.roll(cb, LANES - nsb, 1))
    cb = jnp.where(lax.broadcasted_iota(jnp.int32, (1, LANES), 1) < nsb, cb, 0.0)
    lhs = jnp.concatenate([qs, jnp.concatenate([cb] * NSA_GROUP, axis=0).astype(BF16)], axis=1)
    update(_dot(lhs, jnp.concatenate([kt_all, em_ref[...]], axis=0)), None, vt_all)

    @pl.when(g == ng - 1)
    def _():
        new = newsel_ref[0]
        kpos_n = past + lax.broadcasted_iota(jnp.int32, (1, PAGE_SIZE), 1)
        update(_dot(qs, new[0:LANES].astype(BF16)),
               expand(past // SEL_BLOCK, PAGE_SIZE) & (kpos_n <= qpos), new[LANES:2 * LANES].astype(BF16))
        gates = g_ref[0]
        osel_ref[0] = acc_ref[...] / l_ref[...] * gates[:, 1:2]

        wb = winbuf_ref[0]
        nw = newwin_ref[0]
        n_buf = wb.shape[1]
        kw = jnp.concatenate([wb[0:LANES], nw[0:LANES]], axis=1).astype(BF16)
        vw = jnp.concatenate([wb[LANES:2 * LANES], nw[LANES:2 * LANES]], axis=1).astype(BF16)
        wpos = past - n_buf + lax.broadcasted_iota(jnp.int32, (1, n_buf + PAGE_SIZE), 1)
        okw = (wpos <= qpos) & (wpos > qpos - WINDOW) & (wpos >= 0)
        sw = jnp.where(okw, _dot(qs, kw), NEG_INF)
        e = jnp.where(okw, jnp.exp2(sw - jnp.max(sw, axis=-1, keepdims=True)), 0.0)
        owin_ref[0] = _nt_dot(e.astype(BF16), vw) / jnp.sum(e, axis=-1, keepdims=True) * gates[:, 2:3]


def _selwin_s(qs, sel, gates, pool, page_table, newsel, winbuf, newwin, tq, past, pp):
    nb, nrow, _ = qs.shape
    n_pages = page_table.shape[1]
    bmap = lambda b, g, pt: (b, 0, 0)
    b3 = lambda a: pl.BlockSpec((1,) + a.shape[1:], bmap)

    def page_spec(p):
        return pl.BlockSpec((1, 256, PAGE_SIZE), lambda b, g, pt: (pt[b * n_pages + g * pp + p], 0, 0))

    n = pp * PAGE_SIZE
    assert 4 * (n // SEL_BLOCK) == LANES
    em = jnp.asarray(np.arange(LANES)[:, None] == np.arange(n)[None, :] // SEL_BLOCK, BF16)
    grid_spec = pltpu.PrefetchScalarGridSpec(
        num_scalar_prefetch=1,
        grid=(nb, n_pages // pp),
        in_specs=[b3(qs), b3(sel), b3(gates)] + [page_spec(p) for p in range(pp)]
                 + [b3(newsel), b3(winbuf), b3(newwin), pl.BlockSpec(em.shape, lambda b, g, pt: (0, 0))],
        out_specs=(pl.BlockSpec((1, nrow, 128), bmap), pl.BlockSpec((1, nrow, 128), bmap)),
        scratch_shapes=[pltpu.VMEM((nrow, 1), F32), pltpu.VMEM((nrow, 1), F32), pltpu.VMEM((nrow, 128), F32)],
    )
    shp = jax.ShapeDtypeStruct((nb, nrow, 128), F32)
    return pl.pallas_call(
        functools.partial(_selwin_s_kernel, pp=pp, tq=tq, past=past),
        grid_spec=grid_spec,
        out_shape=(shp, shp),
        compiler_params=_cparams(("parallel", "arbitrary")),
        name="selwin_sample",
    )(page_table.reshape(-1), qs, sel, gates, *([pool] * pp), newsel, winbuf, newwin, em)


def _ret_kernel(rq_ref, rk_ref, rv_ref, sz_ref, s0_ref, dec_ref, qd_ref, kd_ref, gc_ref,
                o_ref, sout_ref, s_ref):
    c = pl.program_id(1)

    @pl.when(c == 0)
    def _():
        s_ref[...] = s0_ref[0]

    lane = lax.broadcasted_iota(jnp.int32, (1, LANES), 1)
    for pr in range(RET_HEADS // 2):
        cs = slice(pr * LANES, (pr + 1) * LANES)
        qc = rq_ref[:, cs]
        kc = rk_ref[:, cs]
        kb = kc.astype(BF16)
        state = s_ref[pr]
        sb = state.astype(BF16)
        new_state = gc_ref[pr] * state
        for hh in range(2):
            h = 2 * pr + hh
            hm = (lane >= RET_DK * hh) & (lane < RET_DK * (hh + 1))
            qm = jnp.where(hm, qc, 0.0)
            a = _nt_dot(qm.astype(BF16), kb) * dec_ref[h]
            v = rv_ref[:, h * RET_DV:(h + 1) * RET_DV]
            vb = v.astype(BF16)
            o = _dot(a.astype(BF16), vb) + _dot((qm * qd_ref[:, cs]).astype(BF16), sb)
            km = jnp.where(hm, kc * kd_ref[:, cs], 0.0)
            new_state = new_state + _dot(km.T.astype(BF16), vb)
            mu = jnp.mean(o, axis=-1, keepdims=True)
            var = jnp.mean(jnp.square(o - mu), axis=-1, keepdims=True)
            o_ref[:, h * RET_DV:(h + 1) * RET_DV] = ((o - mu) * lax.rsqrt(var + EPS)
                                                    * sz_ref[:, h * RET_DV:(h + 1) * RET_DV])
        s_ref[pr] = new_state
        sout_ref[0, pr] = new_state


def _retention(rq, rk, rv, szr, s0, c_real, nb, nc, cc):
    log_g = jnp.log1p(-jnp.exp2(-5.0 - jnp.arange(RET_HEADS, dtype=F32)))
    i = jnp.arange(cc, dtype=F32)
    rel = i[:, None] - i[None, :]
    dec = jnp.where(rel >= 0, jnp.exp(log_g[:, None, None] * jnp.maximum(rel, 0.0)), 0.0)
    qd = jnp.repeat(jnp.exp(log_g[None, :] * (i[:, None] + 1.0)), RET_DK, axis=1)
    kd = jnp.repeat(jnp.exp(log_g[None, :] * (c_real - 1.0 - i[:, None])), RET_DK, axis=1)
    gc = jnp.broadcast_to(jnp.repeat(jnp.exp(log_g * c_real), RET_DK).reshape(2, 128, 1), (2, 128, 128))
    row = lambda n: pl.BlockSpec((cc, n), lambda b, c: (b * nc + c, 0))
    st = pl.BlockSpec((1, 2, 128, 128), lambda b, c: (b, 0, 0, 0))
    t = rq.shape[0]
    return pl.pallas_call(
        _ret_kernel,
        grid=(nb, nc),
        in_specs=[row(256), row(256), row(512), row(512), st, _full((RET_HEADS, cc, cc)),
                  _full((cc, 256)), _full((cc, 256)), _full((2, 128, 128))],
        out_specs=(row(512), st),
        out_shape=(jax.ShapeDtypeStruct((t, 512), F32), jax.ShapeDtypeStruct((nb, 2, 128, 128), F32)),
        scratch_shapes=[pltpu.VMEM((2, 128, 128), F32)],
        compiler_params=_cparams(("parallel", "arbitrary")),
        name="retention",
    )(rq, rk, rv, szr, s0, dec, qd, kd, gc)


def _even_out_kernel(x_ref, *refs):
    mr_ref, w_ref, y_ref = refs[-3:]
    if len(refs) == 4:
        mixed = refs[0][...]
    else:
        oc_ref, os_ref, ow_ref, sz_ref = refs[:4]
        mixed = ((oc_ref[...] + os_ref[...] + ow_ref[...]) * sz_ref[...]).astype(BF16)
    y_ref[...] = (x_ref[...] + _dot(mixed, w_ref[0:512, :]) + _dot(mr_ref[...].astype(BF16), w_ref[512:1024, :]))


def _even_out(x, nsa_parts, mret, w_out, tm):
    t = x.shape[0]
    row = lambda n: pl.BlockSpec((tm, n), lambda i: (i, 0))
    return pl.pallas_call(
        _even_out_kernel,
        grid=(t // tm,),
        in_specs=[row(D_MODEL)] + [row(512)] * (len(nsa_parts) + 1) + [_full((1024, D_MODEL))],
        out_specs=row(D_MODEL),
        out_shape=jax.ShapeDtypeStruct((t, D_MODEL), F32),
        compiler_params=_cparams(("parallel",)),
        name="even_out",
    )(x, *nsa_parts, mret, w_out)


_HALO = 32
_CONV_ROWS = 16


def _rms(x, g):
    return (x * lax.rsqrt(jnp.mean(x * x, axis=-1, keepdims=True) + EPS) * g).astype(BF16)


def _glu(xn, w_ref):
    return _dot(xn, w_ref[:, 0:D_MODEL]) * _sigmoid(_dot(xn, w_ref[:, D_MODEL:2 * D_MODEL]))


def _ln_gate_out(x, c, z, lg_ref, lb_ref, wout_ref):
    mu = jnp.mean(c, axis=-1, keepdims=True)
    var = jnp.mean(jnp.square(c - mu), axis=-1, keepdims=True)
    yn = (c - mu) * lax.rsqrt(var + EPS) * lg_ref[...] + lb_ref[...]
    y = (yn * _sigmoid(yn)) * (z * _sigmoid(z))
    return x + _dot(y.astype(BF16), wout_ref[...])


def _odd_p_kernel(x_ref, xh_ref, g_ref, win_ref, cw_ref, cb_ref, lg_ref, lb_ref, wout_ref,
                  y_ref, ust_ref, u_scr, c_scr, *, tm):
    i = pl.program_id(0)
    x = x_ref[...]
    g = g_ref[...]
    xn = _rms(x, g)
    u = _glu(xn, win_ref)
    z = _dot(xn, win_ref[:, 2 * D_MODEL:3 * D_MODEL])
    uh = _glu(_rms(xh_ref[...], g), win_ref)
    u_scr[0, 0:_HALO, :] = jnp.where(i > 0, uh, 0.0)
    u_scr[0, _HALO:_HALO + tm, :] = u
    u_scr[0, _HALO + tm:_HALO + tm + 8, :] = jnp.zeros((8, D_MODEL), F32)
    ust_ref[...] = u[tm - _HALO:tm, :]
    for s in range(1, 8):
        u_scr[s, 0:_HALO + tm, :] = u_scr[0, s:s + _HALO + tm, :]
    off = _HALO - (CONV_WIDTH - 1)
    cb = cb_ref[...]
    nv = _CONV_ROWS // 8
    for rc in range(tm // _CONV_ROWS):
        r0 = rc * _CONV_ROWS
        acc = jnp.broadcast_to(cb, (_CONV_ROWS, D_MODEL)).reshape(nv, 8, D_MODEL)
        for j in range(CONV_WIDTH):
            a, s = divmod(j + off, 8)
            rows = u_scr[s, r0 + 8 * a:r0 + 8 * a + _CONV_ROWS, :]
            acc = acc + cw_ref[j][None] * rows.reshape(nv, 8, D_MODEL)
        c_scr[r0:r0 + _CONV_ROWS, :] = acc.reshape(_CONV_ROWS, D_MODEL)
    y_ref[...] = _ln_gate_out(x, c_scr[...], z, lg_ref, lb_ref, wout_ref)


def _odd_p(x, norm_g, w_in, cw, cb, lg, lb, w_out, tm):
    t = x.shape[0]
    row = pl.BlockSpec((tm, D_MODEL), lambda i: (i, 0))
    halo = pl.BlockSpec((_HALO, D_MODEL), lambda i: (jnp.maximum(i * (tm // _HALO) - 1, 0), 0))
    vec = _full((1, D_MODEL))
    return pl.pallas_call(
        functools.partial(_odd_p_kernel, tm=tm),
        grid=(t // tm,),
        in_specs=[row, halo, vec, _full((D_MODEL, 3 * D_MODEL)), _full((CONV_WIDTH, 8, D_MODEL)), vec, vec, vec,
                  _full((D_MODEL, D_MODEL))],
        out_specs=(row, _full((_HALO, D_MODEL))),
        out_shape=(jax.ShapeDtypeStruct((t, D_MODEL), F32), jax.ShapeDtypeStruct((_HALO, D_MODEL), F32)),
        scratch_shapes=[pltpu.VMEM((8, _HALO + tm + 8, D_MODEL), F32), pltpu.VMEM((tm, D_MODEL), F32)],
        compiler_params=_cparams(("arbitrary",)),
        name="odd_prompt",
    )(x, x, norm_g, w_in, cw, cb, lg, lb, w_out)


def _odd_s_kernel(x_ref, st_ref, g_ref, win_ref, cw_ref, cb_ref, lg_ref, lb_ref, wout_ref,
                  y_ref, u_ref, *, nb, tq):
    x = x_ref[...]
    xn = _rms(x, g_ref[...])
    u = _glu(xn, win_ref)
    z = _dot(xn, win_ref[:, 2 * D_MODEL:3 * D_MODEL])
    u_ref[...] = u
    n_state = CONV_WIDTH - 1
    cb = cb_ref[...]
    cs = []
    for t in range(tq):
        acc = jnp.broadcast_to(cb, (nb, D_MODEL))
        for j in range(CONV_WIDTH):
            e = t + j
            slab = st_ref[e * nb:(e + 1) * nb, :] if e < n_state else u[(e - n_state) * nb:(e - n_state + 1) * nb, :]
            acc = acc + cw_ref[j:j + 1, :] * slab
        cs.append(acc)
    y_ref[...] = _ln_gate_out(x, jnp.concatenate(cs, axis=0), z, lg_ref, lb_ref, wout_ref)


def _odd_s(x_tm, state_tm, norm_g, w_in, cw, cb, lg, lb, w_out, nb, tq):
    n = x_tm.shape[0]
    return pl.pallas_call(
        functools.partial(_odd_s_kernel, nb=nb, tq=tq),
        out_shape=(jax.ShapeDtypeStruct((n, D_MODEL), F32), jax.ShapeDtypeStruct((n, D_MODEL), F32)),
        compiler_params=pltpu.CompilerParams(vmem_limit_bytes=VMEM_LIMIT),
        name="odd_sample",
    )(x_tm, state_tm, norm_g, w_in, cw, cb, lg, lb, w_out)


def _rope_tables(pos):
    half = HEAD_DIM // 2
    freq = ROPE_THETA ** (-jnp.arange(half, dtype=F32) / half)
    ang = pos.astype(F32)[:, None] * freq[None, :]
    cos = jnp.cos(ang)
    sin = jnp.sin(ang)
    return jnp.tile(cos, (1, 4)), jnp.tile(jnp.concatenate([-sin, sin], axis=1), (1, 2))


def _even_weights(w_in):
    pq, kc, ks, kw, pg, zn, rq, rk, rv, zr = jnp.split(
        w_in, np.cumsum([512, 256, 256, 256, 24, 512, 256, 256, 512])[:].tolist(), axis=-1)
    zero = jnp.zeros((D_MODEL, HEAD_DIM), w_in.dtype)
    qcols = []
    for h in range(NSA_HEADS):
        wh = pq[:, h * HEAD_DIM:(h + 1) * HEAD_DIM]
        qcols += [wh, zero] if h < NSA_GROUP else [zero, wh]
    gpad = jnp.pad(pg, ((0, 0), (0, LANES - 3 * NSA_HEADS)))
    return jnp.concatenate(qcols + [kc, ks, kw, zn, rq, rk, rv, zr, gpad], axis=-1).astype(BF16)


def _pick_kv_half(o, nb, tq):
    o6 = o.reshape(nb, NSA_GROUP, NSA_KV_HEADS, tq, NSA_KV_HEADS, HEAD_DIM)
    o5 = jnp.stack([o6[:, :, 0, :, 0], o6[:, :, 1, :, 1]], axis=2)
    return o5.transpose(0, 3, 2, 1, 4).reshape(nb * tq, NSA_HEADS * HEAD_DIM)


def kernel(x_prompt, x_sample, cache_cmp_kv, cache_sel_kv, cache_win_kv, state_ret, state_conv, page_table,
           norm_even, w_in_even, q_norm_g, k_norm_g, w_cmp_pos, w_out_even,
           norm_odd, w_in_odd, conv_w, conv_b, ln_g, ln_b, w_out_odd):
    _, seq, _ = x_prompt.shape
    nb, tq, _ = x_sample.shape
    n_pages = page_table.shape[1]
    past = n_pages * PAGE_SIZE
    n_buf = cache_win_kv.shape[2]
    tm = 256
    tq_p = 128

    w_e = _even_weights(w_in_even[0])
    w_oe = w_out_even[0].astype(BF16)
    w_io = w_in_odd[0].astype(BF16)
    w_oo = w_out_odd[0].astype(BF16)
    gq = jnp.tile(q_norm_g[0], 2)[None, :]
    gk = jnp.tile(k_norm_g[0], (1, 2))
    w2 = jnp.repeat(w_cmp_pos[0].transpose(1, 0, 2).reshape(CMP_BLOCK, 4), HEAD_DIM, axis=1)
    ne = norm_even[0][None, :]
    no = norm_odd[0][None, :]

    xp = x_prompt[0]
    cos_p, sin_p = _rope_tables(jnp.arange(seq))
    (q_p, kvc_p, kvs_p, kvw_p, k2s_p, k2w_p, vts_p, vtw_p, gate_p, sz_p,
     rq_p, rk_p, rv_p, szr_p, kn_p) = _even_in(xp, ne, w_e, cos_p, sin_p, gq, gk, tm)
    ident = jnp.arange(seq // PAGE_SIZE, dtype=jnp.int32)[None, :]
    ce, co = _compress(kvc_p.reshape(seq // PAGE_SIZE, PAGE_SIZE, 256), ident, w2, 16)
    cmp_p = jnp.concatenate([ce[0], co[0]], axis=0)
    kc_p = cmp_p[:, :LANES].astype(BF16)
    vtc_p = cmp_p[:, LANES:].T.astype(BF16)
    gates_t = gate_p[:, :3 * NSA_HEADS].reshape(seq, NSA_HEADS, 3).transpose(2, 1, 0)
    oc_p, sel_p = _cmp_topk_p(q_p, kc_p, vtc_p, gates_t[0], tq_p)
    ow_p = _win_p(q_p, k2w_p, vtw_p, gates_t[2], tq_p)
    mix_p = _sel_p(q_p, k2s_p, vts_p, kn_p, sel_p, gates_t[1], oc_p, ow_p, sz_p, tq_p, 256, 4)
    s0_p = jnp.zeros((1, 2, 128, 128), F32)
    ret_c = 4 * RET_CHUNK
    mret_p, s_p = _retention(rq_p, rk_p, rv_p, szr_p, s0_p, ret_c, 1, seq // ret_c, ret_c)
    y1_p = _even_out(xp, (mix_p,), mret_p, w_oe, tm)

    xs = x_sample.reshape(nb * tq, D_MODEL)
    cos_s, sin_s = _rope_tables(jnp.tile(past + jnp.arange(tq), nb))
    (q_s, kvc_s, kvs_s, kvw_s, _, _, _, _, gate_s, sz_s,
     rq_s, rk_s, rv_s, szr_s, _) = _even_in(xs, ne, w_e, cos_s, sin_s, gq, gk, nb * tq)
    rows_t = lambda c: c.transpose(0, 2, 3, 4, 1).reshape(c.shape[0], 256, c.shape[1])
    cmp_s = _compress_t(rows_t(cache_cmp_kv[0]), page_table, w_cmp_pos[0], LANES * CMP_BLOCK // PAGE_SIZE)
    qs_s = (q_s.reshape(nb, tq, NSA_KV_HEADS, NSA_GROUP, LANES).transpose(0, 3, 2, 1, 4)
            .reshape(nb, NSA_HEADS * tq, LANES))
    g_s = (gate_s[:, :3 * NSA_HEADS].reshape(nb, tq, NSA_KV_HEADS, NSA_GROUP, 3).transpose(0, 3, 2, 1, 4)
           .reshape(nb, NSA_HEADS * tq, 3))
    oc_s, sel_s = _cmp_topk_s(qs_s, cmp_s, g_s, tq, past, 4 if nb % 4 == 0 else 1)
    new_t = lambda r: jnp.pad(r.reshape(nb, tq, 256), ((0, 0), (0, PAGE_SIZE - tq), (0, 0))).transpose(0, 2, 1)
    os_s, ow_s = _selwin_s(qs_s, sel_s, g_s, rows_t(cache_sel_kv[0]), page_table,
                           new_t(kvs_s), rows_t(cache_win_kv[0]), new_t(kvw_s), tq, past, 16)
    pad_c = lambda r: jnp.pad(r.reshape(nb, tq, -1), ((0, 0), (0, RET_CHUNK - tq), (0, 0))).reshape(nb * RET_CHUNK, -1)
    mret_s, s_s = _retention(pad_c(rq_s), pad_c(rk_s), pad_c(rv_s), pad_c(szr_s),
                             state_ret[0].reshape(nb, 2, 128, 128), float(tq), nb, 1, RET_CHUNK)
    mret_s = mret_s.reshape(nb, RET_CHUNK, 512)[:, :tq].reshape(nb * tq, 512)
    y1_s = _even_out(xs, (_pick_kv_half(oc_s, nb, tq), _pick_kv_half(os_s, nb, tq), _pick_kv_half(ow_s, nb, tq),
                          sz_s), mret_s, w_oe, nb * tq)

    cb = conv_b[0][None, :]
    lg = ln_g[0][None, :]
    lb = ln_b[0][None, :]
    cw8 = jnp.broadcast_to(conv_w[0][:, None, :], (CONV_WIDTH, 8, D_MODEL))
    y2_p, ust_p = _odd_p(y1_p, no, w_io, cw8, cb, lg, lb, w_oo, tm)
    x_tm = y1_s.reshape(nb, tq, D_MODEL).transpose(1, 0, 2).reshape(tq * nb, D_MODEL)
    st_tm = state_conv[0].transpose(1, 0, 2).reshape((CONV_WIDTH - 1) * nb, D_MODEL)
    y2_tm, u_tm = _odd_s(x_tm, st_tm, no, w_io, conv_w[0], cb, lg, lb, w_oo, nb, tq)
    y2_s = y2_tm.reshape(tq, nb, D_MODEL).transpose(1, 0, 2)
    u_s = u_tm.reshape(tq, nb, D_MODEL).transpose(1, 0, 2)

    rows6 = lambda r, b, t: r.reshape(1, b, t, 2, NSA_KV_HEADS, HEAD_DIM)
    n_win = min(WINDOW, seq)
    win_s = jnp.concatenate([cache_win_kv[0].reshape(nb, n_buf, 256), kvw_s.reshape(nb, tq, 256)], axis=1)[:, -n_buf:]
    conv_s = jnp.concatenate([state_conv[0], u_s], axis=1)[:, -(CONV_WIDTH - 1):]
    return (y2_p[None], y2_s,
            rows6(kvc_p, 1, seq), rows6(kvc_s, nb, tq),
            rows6(kvs_p, 1, seq), rows6(kvs_s, nb, tq),
            rows6(kvw_p[seq - n_win:], 1, n_win), rows6(win_s, nb, n_buf),
            s_p.reshape(1, 1, RET_HEADS, RET_DK, RET_DV), s_s.reshape(1, nb, RET_HEADS, RET_DK, RET_DV),
            ust_p[_HALO - (CONV_WIDTH - 1):][None, None], conv_s[None])
```

```python
import functools

import numpy as np
import jax
import jax.numpy as jnp
from jax import lax
from jax.experimental import pallas as pl
from jax.experimental.pallas import tpu as pltpu

F32 = jnp.float32
BF16 = jnp.bfloat16

D_MODEL = 1024
PAGE_SIZE = 128
HEAD_DIM = 64
NSA_HEADS = 8
NSA_KV_HEADS = 2
NSA_GROUP = 4
CMP_BLOCK = 32
SEL_BLOCK = 64
N_SELECT = 16
WINDOW = 512
RET_HEADS = 4
RET_DK = 64
RET_DV = 128
RET_CHUNK = 128
CONV_WIDTH = 31
ROPE_THETA = 10000.0
NEG_INF = -1e30
FORCE = 1e9
EPS = 1e-6

LOG2E = 1.4426950408889634
Q_SCALE = HEAD_DIM ** -0.5 * LOG2E

LANES = 128
VMEM_LIMIT = 56 * 1024 * 1024

_QP = 0
_KV = 1024
_ZN = 1792
_RQ = 2304
_RK = 2560
_RV = 2816
_ZR = 3328
_GT = 3840
_EVEN_COLS = 3968


def _cparams(sem):
    return pltpu.CompilerParams(dimension_semantics=sem, vmem_limit_bytes=VMEM_LIMIT)


def _sigmoid(x):
    return 1.0 / (1.0 + jnp.exp(-x))


def _nt_dot(a, b):
    return lax.dot_general(a, b, (((1,), (1,)), ((), ())), preferred_element_type=F32)


def _dot(a, b):
    return jnp.dot(a, b, preferred_element_type=F32)


def _full(shape):
    n = len(shape)
    return pl.BlockSpec(shape, lambda *_: (0,) * n)


def _even_in_kernel(x_ref, g_ref, w_ref, cos_ref, sin_ref, gq_ref, gk_ref,
                    q_ref, kvc_ref, kvs_ref, kvw_ref, k2s_ref, k2w_ref, vts_ref, vtw_ref,
                    gate_ref, sz_ref, rq_ref, rk_ref, rv_ref, szr_ref, kn_ref):
    x = x_ref[...]
    ms = jnp.mean(x * x, axis=-1, keepdims=True)
    xn = (x * lax.rsqrt(ms + EPS) * g_ref[...]).astype(BF16)
    cos = cos_ref[...]
    sin = sin_ref[...]
    lane = lax.broadcasted_iota(jnp.int32, (1, LANES), 1)
    lo_half = (lane & 32) == 0
    gr = lax.broadcasted_iota(jnp.int32, (LANES, LANES), 0) // HEAD_DIM
    gc = lax.broadcasted_iota(jnp.int32, (LANES, LANES), 1) // HEAD_DIM
    gmat = (gr == gc).astype(BF16)

    def proj(a, b):
        return _dot(xn, w_ref[:, a:b])

    def head_rms(h, gain):
        h2 = h * h
        hi = h2.astype(BF16)
        lo = (h2 - hi.astype(F32)).astype(BF16)
        ss = _dot(hi, gmat) + _dot(lo, gmat)
        return h * lax.rsqrt(ss * (1.0 / HEAD_DIM) + EPS) * gain

    def rope(h):
        sw = jnp.where(lo_half, pltpu.roll(h, LANES - 32, 1), pltpu.roll(h, 32, 1))
        return h * cos + sw * sin

    hq = proj(_QP, _KV)
    gq = gq_ref[...]
    for c in range(NSA_HEADS):
        hc = hq[:, c * LANES:(c + 1) * LANES]
        q_ref[:, c * LANES:(c + 1) * LANES] = (rope(head_rms(hc, gq)) * Q_SCALE).astype(BF16)

    hk = proj(_KV, _ZN)
    kv_refs = (kvc_ref, kvs_ref, kvw_ref)
    k2_refs = (None, k2s_ref, k2w_ref)
    vt_refs = (None, vts_ref, vtw_ref)
    for b in range(3):
        k = rope(head_rms(hk[:, 2 * b * LANES:(2 * b + 1) * LANES], gk_ref[b:b + 1, :]))
        v = hk[:, (2 * b + 1) * LANES:(2 * b + 2) * LANES]
        kv_refs[b][:, 0:LANES] = k
        kv_refs[b][:, LANES:2 * LANES] = v
        if k2_refs[b] is not None:
            kb = k.astype(BF16)
            k2_refs[b][...] = kb
            vt = v.T.astype(BF16)
            tail = jnp.where(lax.broadcasted_iota(jnp.int32, (_V_ROWS - HEAD_DIM, 1), 0) == 0, 1.0, 0.0)
            tail = jnp.broadcast_to(tail, (_V_ROWS - HEAD_DIM, vt.shape[1])).astype(BF16)
            for kh in range(NSA_KV_HEADS):
                vt_refs[b][kh, 0:HEAD_DIM, :] = vt[HEAD_DIM * kh:HEAD_DIM * (kh + 1), :]
                vt_refs[b][kh, HEAD_DIM:_V_ROWS, :] = tail
            if b == 1:
                kf = kb.astype(F32)
                k2 = kf * kf
                hi = k2.astype(BF16)
                n2 = _dot(hi, gmat) + _dot((k2 - hi.astype(F32)).astype(BF16), gmat)
                kn_ref[0] = jnp.broadcast_to(jnp.max(n2, axis=0, keepdims=True), (8, LANES)) * 1.001

    hz = proj(_ZN, _RQ)
    sz_ref[...] = hz * _sigmoid(hz)
    hrq = proj(_RQ, _RK)
    hrk = proj(_RK, _RV)
    for c in range(2):
        rq_ref[:, c * LANES:(c + 1) * LANES] = rope(hrq[:, c * LANES:(c + 1) * LANES])
        rk_ref[:, c * LANES:(c + 1) * LANES] = rope(hrk[:, c * LANES:(c + 1) * LANES]) * (RET_DK ** -0.5)
    rv_ref[...] = proj(_RV, _ZR)
    hzr = proj(_ZR, _GT)
    szr_ref[...] = hzr * _sigmoid(hzr)
    gate_ref[...] = _sigmoid(proj(_GT, _EVEN_COLS))


def _even_in(x, norm_g, w_e, cos, sin, gq, gk, tm):
    t = x.shape[0]
    row = lambda n: pl.BlockSpec((tm, n), lambda i: (i, 0))
    out_shape = (
        jax.ShapeDtypeStruct((t, 1024), BF16),
        jax.ShapeDtypeStruct((t, 256), F32),
        jax.ShapeDtypeStruct((t, 256), F32),
        jax.ShapeDtypeStruct((t, 256), F32),
        jax.ShapeDtypeStruct((t, 128), BF16),
        jax.ShapeDtypeStruct((t, 128), BF16),
        jax.ShapeDtypeStruct((NSA_KV_HEADS, _V_ROWS, t), BF16),
        jax.ShapeDtypeStruct((NSA_KV_HEADS, _V_ROWS, t), BF16),
        jax.ShapeDtypeStruct((t, 128), F32),
        jax.ShapeDtypeStruct((t, 512), F32),
        jax.ShapeDtypeStruct((t, 256), F32),
        jax.ShapeDtypeStruct((t, 256), F32),
        jax.ShapeDtypeStruct((t, 512), F32),
        jax.ShapeDtypeStruct((t, 512), F32),
        jax.ShapeDtypeStruct((t // tm, 8, 128), F32),
    )
    vt_spec = pl.BlockSpec((NSA_KV_HEADS, _V_ROWS, tm), lambda i: (0, 0, i))
    out_specs = (row(1024), row(256), row(256), row(256), row(128), row(128), vt_spec, vt_spec,
                 row(128), row(512), row(256), row(256), row(512), row(512),
                 pl.BlockSpec((1, 8, 128), lambda i: (i, 0, 0)))
    return pl.pallas_call(
        _even_in_kernel,
        grid=(t // tm,),
        in_specs=[row(D_MODEL), _full((1, D_MODEL)), _full((D_MODEL, _EVEN_COLS)),
                  row(128), row(128), _full((1, 128)), _full((3, 128))],
        out_specs=out_specs,
        out_shape=out_shape,
        compiler_params=_cparams(("parallel",)),
        name="even_in",
    )(x, norm_g, w_e, cos, sin, gq, gk)


def _compress_kernel(pt_ref, *refs, pp):
    del pt_ref
    page_refs = refs[:pp]
    w_ref, oe_ref, oo_ref = refs[pp:]
    w = w_ref[...]
    for p in range(pp):
        x = page_refs[p][0]
        s = jnp.sum(x.reshape(PAGE_SIZE // CMP_BLOCK, CMP_BLOCK, 256) * w[None], axis=1)
        oe_ref[0, 2 * p:2 * p + 1, :] = s[0:1]
        oo_ref[0, 2 * p:2 * p + 1, :] = s[1:2]
        oe_ref[0, 2 * p + 1:2 * p + 2, :] = s[2:3]
        oo_ref[0, 2 * p + 1:2 * p + 2, :] = s[3:4]


def _compress(pool, page_table, w2, pp):
    nb, n_pages = page_table.shape

    def page_spec(p):
        return pl.BlockSpec((1, PAGE_SIZE, 256), lambda b, g, pt: (pt[b * n_pages + g * pp + p], 0, 0))

    out_spec = pl.BlockSpec((1, 2 * pp, 256), lambda b, g, pt: (b, g, 0))
    grid_spec = pltpu.PrefetchScalarGridSpec(
        num_scalar_prefetch=1,
        grid=(nb, n_pages // pp),
        in_specs=[page_spec(p) for p in range(pp)] + [pl.BlockSpec((CMP_BLOCK, 256), lambda b, g, pt: (0, 0))],
        out_specs=(out_spec, out_spec),
    )
    shp = jax.ShapeDtypeStruct((nb, 2 * n_pages, 256), F32)
    return pl.pallas_call(
        functools.partial(_compress_kernel, pp=pp),
        grid_spec=grid_spec,
        out_shape=(shp, shp),
        compiler_params=_cparams(("parallel", "parallel")),
        name="compress",
    )(page_table.reshape(-1), *([pool] * pp), w2)


def _cmp_lane(w):
    return (w % 2) * (LANES // 2) + w // 2


def _compress_t_kernel(pt_ref, *refs, pp):
    del pt_ref
    page_refs = refs[:pp]
    w_ref, g_ref, o_ref = refs[pp:]
    w = w_ref[...]
    prods = [(page_refs[p][0] * w).astype(BF16) for p in range(pp)]
    o_ref[0] = _dot(jnp.concatenate(prods, axis=1), g_ref[...])


def _compress_t(pool_t, page_table, w_pos, pp):
    nb, n_pages = page_table.shape
    per_page = PAGE_SIZE // CMP_BLOCK
    assert pp * per_page == LANES
    wf = jnp.tile(jnp.broadcast_to(w_pos.transpose(0, 2, 1)[:, :, None, :],
                                   (2, NSA_KV_HEADS, HEAD_DIM, CMP_BLOCK)).reshape(256, CMP_BLOCK), (1, per_page))
    r = jnp.arange(pp * PAGE_SIZE)
    gm = (_cmp_lane(r // CMP_BLOCK)[:, None] == jnp.arange(LANES)[None, :]).astype(BF16)

    def page_spec(p):
        return pl.BlockSpec((1, 256, PAGE_SIZE), lambda b, g, pt: (pt[b * n_pages + g * pp + p], 0, 0))

    grid_spec = pltpu.PrefetchScalarGridSpec(
        num_scalar_prefetch=1,
        grid=(nb, n_pages // pp),
        in_specs=[page_spec(p) for p in range(pp)]
                 + [pl.BlockSpec((256, PAGE_SIZE), lambda b, g, pt: (0, 0)),
                    pl.BlockSpec((pp * PAGE_SIZE, LANES), lambda b, g, pt: (0, 0))],
        out_specs=pl.BlockSpec((1, 256, LANES), lambda b, g, pt: (b, 0, g)),
    )
    return pl.pallas_call(
        functools.partial(_compress_t_kernel, pp=pp),
        grid_spec=grid_spec,
        out_shape=jax.ShapeDtypeStruct((nb, 256, per_page * n_pages), F32),
        compiler_params=_cparams(("parallel", "parallel")),
        name="compress_sample",
    )(page_table.reshape(-1), *([pool_t] * pp), wf, gm)


def _stack_heads(q, k):
    return jnp.concatenate([q[:, (NSA_GROUP * k + g) * LANES:(NSA_GROUP * k + g + 1) * LANES]
                            for g in range(NSA_GROUP)], axis=0)


def _store_heads(o_ref, og, k, tq):
    for j in range(2):
        blk = jnp.concatenate([og[2 * j], og[2 * j + 1]], axis=0)
        o_ref[:, (2 * k + j) * LANES:(2 * k + j + 1) * LANES] = blk.T


_CMP_VARIANTS = 4


def _cmp_topk_p_kernel(q_ref, kc_ref, vtc_ref, g_ref, o_ref, sel_ref, sc_ref, *, tq, n_half, nq):
    i = pl.program_id(0)
    qpos = i * tq + lax.broadcasted_iota(jnp.int32, (1, tq), 1)
    sb = lax.broadcasted_iota(jnp.int32, (n_half, 1), 0)
    sbf = sb.astype(F32)
    forced = (sb == qpos // SEL_BLOCK) | (sb == 0)
    future = sb * SEL_BLOCK > qpos
    q = q_ref[...]
    gates = g_ref[...]

    def attend(nh):
        r = lax.broadcasted_iota(jnp.int32, (2 * nh, 1), 0)
        blk = jnp.where(r < nh, 2 * r, 2 * (r - nh) + 1)
        mask = ((blk + 1) * CMP_BLOCK - 1) <= qpos
        kc = jnp.concatenate([kc_ref[0:nh, :], kc_ref[n_half:n_half + nh, :]], axis=0)
        for k in range(NSA_KV_HEADS):
            s = _nt_dot(kc, _stack_heads(q, k))
            imp = jnp.zeros((2 * nh, tq), F32)
            ps = []
            for g in range(NSA_GROUP):
                sg = jnp.where(mask, s[:, g * tq:(g + 1) * tq], NEG_INF)
                e = jnp.exp2(sg - jnp.max(sg, axis=0, keepdims=True))
                p = jnp.where(mask, e / jnp.sum(e, axis=0, keepdims=True), 0.0)
                imp = imp + p
                ps.append(p.astype(BF16))
            pt = jnp.concatenate(ps, axis=1)
            vk = slice(HEAD_DIM * k, HEAD_DIM * (k + 1))
            ot = (_dot(vtc_ref[vk, 0:nh], pt[0:nh]) + _dot(vtc_ref[vk, n_half:n_half + nh], pt[nh:2 * nh]))
            og = [ot[:, g * tq:(g + 1) * tq] * gates[NSA_GROUP * k + g:NSA_GROUP * k + g + 1, :]
                  for g in range(NSA_GROUP)]
            _store_heads(o_ref, og, k, tq)
            imp2 = imp[:nh] + imp[nh:]
            if nh < n_half:
                imp2 = jnp.concatenate([imp2, jnp.zeros((n_half - nh, tq), F32)], axis=0)
            sc_ref[k] = jnp.where(forced, -jnp.inf, jnp.where(future, -FORCE, imp2))

    per_var = nq // _CMP_VARIANTS
    for c in range(_CMP_VARIANTS):
        pl.when(i // per_var == c)(functools.partial(attend, (c + 1) * n_half // _CMP_VARIANTS))

    n_forced = 1 + (qpos >= SEL_BLOCK).astype(jnp.int32)
    need = min(N_SELECT, n_half) - n_forced

    def pick(rnd, scs):
        out = []
        for sc in scs:
            mx = jnp.max(sc, axis=0, keepdims=True)
            first = jnp.min(jnp.where(sc == mx, sbf, 1e9), axis=0, keepdims=True)
            out.append(jnp.where((sbf == first) & (rnd < need), -jnp.inf, sc))
        return tuple(out)

    n_rounds = min(N_SELECT, n_half) - 1 - (i > 0).astype(jnp.int32)
    scs = lax.fori_loop(0, n_rounds, pick, (sc_ref[0], sc_ref[1]))
    for k in range(NSA_KV_HEADS):
        sel_ref[0, k] = jnp.where(scs[k] == -jnp.inf, 0.0, NEG_INF)


def _cmp_topk_p(q, kc, vtc, gates0, tq):
    t = q.shape[0]
    n_half = kc.shape[0] // 2
    nq = t // tq
    assert nq % _CMP_VARIANTS == 0 and (n_half // _CMP_VARIANTS) % 8 == 0
    assert n_half * SEL_BLOCK >= t
    return pl.pallas_call(
        functools.partial(_cmp_topk_p_kernel, tq=tq, n_half=n_half, nq=nq),
        grid=(nq,),
        in_specs=[pl.BlockSpec((tq, 1024), lambda i: (i, 0)), _full(kc.shape), _full(vtc.shape),
                  pl.BlockSpec((NSA_HEADS, tq), lambda i: (0, i))],
        out_specs=(pl.BlockSpec((tq, 512), lambda i: (i, 0)),
                   pl.BlockSpec((1, NSA_KV_HEADS, n_half, tq), lambda i: (i, 0, 0, 0))),
        out_shape=(jax.ShapeDtypeStruct((t, 512), F32),
                   jax.ShapeDtypeStruct((nq, NSA_KV_HEADS, n_half, tq), F32)),
        scratch_shapes=[pltpu.VMEM((NSA_KV_HEADS, n_half, tq), F32)],
        compiler_params=_cparams(("parallel",)),
        name="cmp_topk_prompt",
    )(q, kc, vtc, gates0)


_V_ROWS = HEAD_DIM + 16
_X_ROWS = 16
_SAFE_BOUND = 60.0


def _sel_p_kernel(q_ref, k2_ref, vt_ref, kn_ref, ex_ref, sel_ref, g_ref, oc_ref, ow_ref, sz_ref, o_ref,
                  acc_ref, m_ref, l_ref, s_ref, p_ref, a_ref, qt_ref, os_ref, *, tq, tk, ns):
    i = pl.program_id(0)
    qpos = i * tq + lax.broadcasted_iota(jnp.int32, (1, tq), 1)
    q = q_ref[...]
    qts = [_stack_heads(q, k).astype(F32).T for k in range(NSA_KV_HEADS)]
    nblk = tk // SEL_BLOCK
    assert 2 * nblk == 8 and tk % tq == 0 and ns % 2 == 0 and (k2_ref.shape[0] // tk) % ns == 0
    n_half = sel_ref.shape[2]
    kiota = lax.broadcasted_iota(jnp.int32, (tk, 1), 0)
    n_tiles = (i * tq + tq + tk - 1) // tk
    n_steps = (n_tiles + ns - 1) // ns
    max_tile = k2_ref.shape[0] // tk - 1
    gates = g_ref[...]

    def finish(numer, denom):
        for k in range(NSA_KV_HEADS):
            ot = numer(k) / denom(k)
            og = [ot[:, g * tq:(g + 1) * tq] * gates[NSA_GROUP * k + g:NSA_GROUP * k + g + 1, :]
                  for g in range(NSA_GROUP)]
            _store_heads(os_ref, og, k, tq)
        o_ref[...] = ((oc_ref[...] + os_ref[...] + ow_ref[...]) * sz_ref[...]).astype(BF16)

    kn = jnp.max(kn_ref[...].reshape(-1, LANES), axis=0, keepdims=True)
    bounds = []
    for k in range(NSA_KV_HEADS):
        kmax2 = jnp.max(kn[:, HEAD_DIM * k:HEAD_DIM * (k + 1)], axis=1, keepdims=True)
        qn2 = jnp.sum(qts[k] * qts[k], axis=0, keepdims=True)
        bounds.append(jnp.sqrt(qn2 * kmax2) * 1.02 + 1.0)
    safe = jnp.maximum(jnp.max(bounds[0]), jnp.max(bounds[1])) < _SAFE_BOUND
    lane = lax.broadcasted_iota(jnp.int32, (1, LANES), 1)
    row8 = lax.broadcasted_iota(jnp.int32, (_X_ROWS - 8, 1), 0)
    tails = [jnp.where(row8 == 0, -b, 0.0) for b in bounds]

    def f_scores(j, s1):
        slot = s1 % ns
        kt = jnp.minimum(ns * j + s1, max_tile)
        ktile = k2_ref[pl.ds(pl.multiple_of(kt * tk, tk), tk), :]
        grp = pl.multiple_of(jnp.minimum((j * (ns // 2) + s1 // 2) * 8, n_half - 8), 8)
        for k in range(NSA_KV_HEADS):
            r0 = HEAD_DIM * (1 - k)
            rows8 = jnp.concatenate([sel_ref[0, k, pl.ds(grp, 8), :]] * NSA_GROUP, axis=1)
            qt_ref[slot, k, r0:r0 + _X_ROWS, :] = jnp.concatenate([rows8, tails[k]], axis=0).astype(BF16)
            own = (lane >= HEAD_DIM * k) & (lane < HEAD_DIM * (k + 1))
            lhs = jnp.where(own, ktile, ex_ref[s1 % 2, k])
            s_ref[slot, k] = _dot(lhs, qt_ref[slot, k])

    def f_values(kt, slot):
        kt = jnp.clip(kt, 0, max_tile)
        for k in range(NSA_KV_HEADS):
            acc_ref[k] = acc_ref[k] + _dot(vt_ref[k, :, pl.ds(pl.multiple_of(kt * tk, tk), tk)], p_ref[slot, k])

    def f_softmax(j, slot, masked):
        visible = (ns * j + slot) * tk + kiota <= qpos
        for k in range(NSA_KV_HEADS):
            for g in range(NSA_GROUP):
                cs = slice(g * tq, (g + 1) * tq)
                p = jnp.exp2(s_ref[slot, k, :, cs])
                if masked:
                    p = jnp.where(visible, p, 0.0)
                p_ref[slot, k, :, cs] = p.astype(BF16)

    def f_step(j, masked):
        for s in range(ns):
            f_values(ns * j + s - 1, (s - 1) % ns)
            f_softmax(j, s, masked)
            f_scores(j, s + 1)

    def fast_path():
        for k in range(NSA_KV_HEADS):
            w = qts[k].astype(BF16)
            for slot in range(ns):
                qt_ref[slot, k] = w
        acc_ref[...] = jnp.zeros(acc_ref.shape, F32)
        p_ref[...] = jnp.zeros(p_ref.shape, BF16)
        f_scores(0, 0)

        def body(j, carry):
            f_step(j, False)
            return carry

        lax.fori_loop(0, n_steps - 1, body, 0)
        f_step(n_steps - 1, True)
        f_values(ns * n_steps - 1, ns - 1)
        finish(lambda k: acc_ref[k, 0:HEAD_DIM], lambda k: acc_ref[k, HEAD_DIM:HEAD_DIM + 1])

    def scores(kt, slot):
        kt = jnp.minimum(kt, max_tile)
        ktile = k2_ref[pl.ds(pl.multiple_of(kt * tk, tk), tk), :]
        for k in range(NSA_KV_HEADS):
            s_ref[slot, k] = _dot(ktile, qt_ref[0, k])

    def values(kt, slot):
        kt = jnp.clip(kt, 0, max_tile)
        for k in range(NSA_KV_HEADS):
            vtile = vt_ref[k, 0:HEAD_DIM, pl.ds(pl.multiple_of(kt * tk, tk), tk)]
            acc_ref[k, 0:HEAD_DIM] = acc_ref[k, 0:HEAD_DIM] * a_ref[slot, k] + _dot(vtile, p_ref[slot, k])

    def softmax(j, slot):
        visible = (ns * j + slot) * tk + kiota <= qpos
        grp = pl.multiple_of((j * (ns // 2) + slot // 2) * 8, 8)
        for k in range(NSA_KV_HEADS):
            rows = sel_ref[0, k, pl.ds(grp, 8), :][(slot % 2) * nblk:(slot % 2 + 1) * nblk]
            bias = jnp.concatenate([jnp.broadcast_to(rows[r:r + 1, :], (SEL_BLOCK, tq)) for r in range(nblk)], axis=0)
            bias = jnp.where(visible, bias, NEG_INF)
            for g in range(NSA_GROUP):
                cs = slice(g * tq, (g + 1) * tq)
                sg = s_ref[slot, k, :, cs] + bias
                m_old = m_ref[k, :, cs]
                m_new = jnp.maximum(m_old, jnp.max(sg, axis=0, keepdims=True))
                alpha = jnp.exp2(m_old - m_new)
                p = jnp.exp2(sg - m_new)
                l_ref[k, :, cs] = alpha * l_ref[k, :, cs] + jnp.sum(p, axis=0, keepdims=True)
                m_ref[k, :, cs] = m_new
                a_ref[slot, k, :, cs] = alpha
                p_ref[slot, k, :, cs] = p.astype(BF16)

    def body(j, carry):
        for s in range(ns):
            scores(ns * j + s + 1, (s + 1) % ns)
            values(ns * j + s - 1, (s - 1) % ns)
            softmax(j, s)
        return carry

    def exact_path():
        for k in range(NSA_KV_HEADS):
            qt_ref[0, k] = qts[k].astype(BF16)
        m_ref[...] = jnp.full(m_ref.shape, NEG_INF, F32)
        l_ref[...] = jnp.zeros(l_ref.shape, F32)
        acc_ref[...] = jnp.zeros(acc_ref.shape, F32)
        p_ref[...] = jnp.zeros(p_ref.shape, BF16)
        a_ref[...] = jnp.ones(a_ref.shape, F32)
        scores(0, 0)
        lax.fori_loop(0, n_steps, body, 0)
        values(ns * n_steps - 1, ns - 1)
        finish(lambda k: acc_ref[k, 0:HEAD_DIM], lambda k: l_ref[k])

    lax.cond(safe, fast_path, exact_path)


def _sel_extra_columns(tk):
    row = np.arange(tk)[:, None]
    lane = np.arange(LANES)[None, :]
    ex = np.zeros((2, NSA_KV_HEADS, tk, LANES), np.float32)
    for par in range(2):
        for k in range(NSA_KV_HEADS):
            r0 = HEAD_DIM * (1 - k)
            ex[par, k] = (lane == r0 + par * (tk // SEL_BLOCK) + row // SEL_BLOCK) | (lane == r0 + 8)
    return jnp.asarray(ex, BF16)


def _sel_p(q, k2, vt, kn, sel, gates1, oc, ow, sz, tq, tk, ns):
    t = q.shape[0]
    n_half = sel.shape[2]
    ex = _sel_extra_columns(tk)
    return pl.pallas_call(
        functools.partial(_sel_p_kernel, tq=tq, tk=tk, ns=ns),
        grid=(t // tq,),
        in_specs=[pl.BlockSpec((tq, 1024), lambda i: (i, 0)), _full(k2.shape), _full(vt.shape), _full(kn.shape),
                  _full(ex.shape),
                  pl.BlockSpec((1, NSA_KV_HEADS, n_half, tq), lambda i: (i, 0, 0, 0)),
                  pl.BlockSpec((NSA_HEADS, tq), lambda i: (0, i))] + [pl.BlockSpec((tq, 512), lambda i: (i, 0))] * 3,
        out_specs=pl.BlockSpec((tq, 512), lambda i: (i, 0)),
        out_shape=jax.ShapeDtypeStruct((t, 512), BF16),
        scratch_shapes=[pltpu.VMEM((NSA_KV_HEADS, _V_ROWS, NSA_GROUP * tq), F32),
                        pltpu.VMEM((NSA_KV_HEADS, 1, NSA_GROUP * tq), F32),
                        pltpu.VMEM((NSA_KV_HEADS, 1, NSA_GROUP * tq), F32),
                        pltpu.VMEM((ns, NSA_KV_HEADS, tk, NSA_GROUP * tq), F32),
                        pltpu.VMEM((ns, NSA_KV_HEADS, tk, NSA_GROUP * tq), BF16),
                        pltpu.VMEM((ns, NSA_KV_HEADS, 1, NSA_GROUP * tq), F32),
                        pltpu.VMEM((ns, NSA_KV_HEADS, LANES, NSA_GROUP * tq), BF16),
                        pltpu.VMEM((tq, 512), F32)],
        compiler_params=_cparams(("parallel",)),
        name="sel_prompt",
    )(q, k2, vt, kn, ex, sel, gates1, oc, ow, sz)


def _win_p_kernel(q_ref, k2_ref, vt_ref, g_ref, o_ref, *, tq, nsub):
    nk = WINDOW + tq
    lane = lax.broadcasted_iota(jnp.int32, (1, LANES), 1)
    rowx = lax.broadcasted_iota(jnp.int32, (LANES, 1), 0)
    blocks = []
    for sub in range(nsub):
        i = pl.program_id(0) * nsub + sub
        qpos = i * tq + lax.broadcasted_iota(jnp.int32, (1, tq), 1)
        start = pl.multiple_of(jnp.maximum(i * tq - WINDOW, 0), tq)
        kpos = start + lax.broadcasted_iota(jnp.int32, (nk, 1), 0)
        ok = (kpos <= qpos) & (kpos > qpos - WINDOW)
        bias = jnp.where(ok, 0.0, NEG_INF)
        q = q_ref[sub * tq:(sub + 1) * tq, :]
        ktile = k2_ref[pl.ds(start, nk), :]
        kf = ktile.astype(F32)
        kmax2 = jnp.max(jnp.sum(kf * kf, axis=1, keepdims=True), axis=0, keepdims=True)
        qts = [_stack_heads(q, k).astype(F32).T for k in range(NSA_KV_HEADS)]
        bounds = [jnp.sqrt(jnp.sum(t * t, axis=0, keepdims=True) * kmax2) * 1.02 + 1.0 for t in qts]
        blocks.append(dict(start=start, bias4=jnp.concatenate([bias] * NSA_GROUP, axis=1), q=q, ktile=ktile,
                           qts=qts, bounds=bounds, gates=g_ref[:, sub * tq:(sub + 1) * tq],
                           out=o_ref.at[sub * tq:(sub + 1) * tq, :]))
    safe = functools.reduce(jnp.maximum, [jnp.max(b) for blk in blocks for b in blk["bounds"]]) < _SAFE_BOUND

    def finish(blk, k, ot):
        og = [ot[:, g * tq:(g + 1) * tq] * blk["gates"][NSA_GROUP * k + g:NSA_GROUP * k + g + 1, :]
              for g in range(NSA_GROUP)]
        _store_heads(blk["out"], og, k, tq)

    def fast_path():
        for blk in blocks:
            for k in range(NSA_KV_HEADS):
                r0 = HEAD_DIM * (1 - k)
                own = (lane >= HEAD_DIM * k) & (lane < HEAD_DIM * (k + 1))
                lhs = jnp.where(own, blk["ktile"], jnp.where(lane == r0, 1.0, 0.0).astype(BF16))
                w = jnp.where(rowx == r0, -blk["bounds"][k], blk["qts"][k]).astype(BF16)
                p = jnp.exp2(_dot(lhs, w) + blk["bias4"]).astype(BF16)
                acc = _dot(vt_ref[k, :, pl.ds(blk["start"], nk)], p)
                finish(blk, k, acc[0:HEAD_DIM] / acc[HEAD_DIM:HEAD_DIM + 1])

    def exact_path():
        for blk in blocks:
            for k in range(NSA_KV_HEADS):
                s = _nt_dot(blk["ktile"], _stack_heads(blk["q"], k)) + blk["bias4"]
                p = jnp.exp2(s - jnp.max(s, axis=0, keepdims=True))
                l = jnp.sum(p, axis=0, keepdims=True)
                finish(blk, k, _dot(vt_ref[k, 0:HEAD_DIM, pl.ds(blk["start"], nk)], p.astype(BF16)) / l)

    lax.cond(safe, fast_path, exact_path)


def _win_p(q, k2, vt, gates2, tq, nsub):
    t = q.shape[0]
    tq, tq1 = tq * nsub, tq
    return pl.pallas_call(
        functools.partial(_win_p_kernel, tq=tq1, nsub=nsub),
        grid=(t // tq,),
        in_specs=[pl.BlockSpec((tq, 1024), lambda i: (i, 0)), _full(k2.shape), _full(vt.shape),
                  pl.BlockSpec((NSA_HEADS, tq), lambda i: (0, i))],
        out_specs=pl.BlockSpec((tq, 512), lambda i: (i, 0)),
        out_shape=jax.ShapeDtypeStruct((t, 512), F32),
        compiler_params=_cparams(("parallel",)),
        name="win_prompt",
    )(q, k2, vt, gates2)


def _flag_block(lane):
    r = lane % LANES
    return (lane // LANES) * (LANES // 2) + r, r < LANES // 2


def _cmp_topk_s_kernel(q_ref, ct_ref, g_ref, o_ref, sel_ref, *, tq, past, n_cmp, n_sb, n_cols):
    bs, nrow, _ = q_ref.shape
    nr8 = NSA_KV_HEADS * tq
    qpos = past + lax.broadcasted_iota(jnp.int32, (nrow, 1), 0) % tq
    col = lax.broadcasted_iota(jnp.int32, (1, n_cmp), 1)
    r = col % LANES
    blk = (col // LANES) * LANES + 2 * (r % (LANES // 2)) + r // (LANES // 2)
    mask = ((blk + 1) * CMP_BLOCK - 1) <= qpos
    imps = []
    for j in range(bs):
        qs = q_ref[j]
        ct = ct_ref[j]
        sm = jnp.where(mask, _dot(qs, ct[0:LANES].astype(BF16)), NEG_INF)
        e = jnp.exp2(sm - jnp.max(sm, axis=-1, keepdims=True))
        p = jnp.where(mask, e / jnp.sum(e, axis=-1, keepdims=True), 0.0)
        o_ref[j] = _nt_dot(p.astype(BF16), ct[LANES:2 * LANES].astype(BF16)) * g_ref[j][:, 0:1]
        imp = p[0:nr8]
        for g in range(1, NSA_GROUP):
            imp = imp + p[g * nr8:(g + 1) * nr8]
        pairs = []
        for c in range(n_cmp // LANES):
            ch = imp[:, c * LANES:(c + 1) * LANES]
            pairs.append(ch + pltpu.roll(ch, LANES // 2, 1))
        imps.append(jnp.concatenate(pairs + [jnp.zeros((nr8, n_cols - n_cmp), F32)], axis=1))
    imp = jnp.concatenate(imps, axis=0)
    qpos8 = past + lax.broadcasted_iota(jnp.int32, (bs * nr8, 1), 0) % tq
    sb, used = _flag_block(lax.broadcasted_iota(jnp.int32, (1, n_cols), 1))
    sbf = sb.astype(F32)
    score = jnp.where((sb == qpos8 // SEL_BLOCK) | (sb == 0), FORCE, imp)
    score = jnp.where(sb * SEL_BLOCK > qpos8, -FORCE, score)
    valid = used & (sb < n_sb)
    score = jnp.where(valid, score, -jnp.inf)

    def pick(_, sc):
        mx = jnp.max(sc, axis=-1, keepdims=True)
        first = jnp.min(jnp.where(sc == mx, sbf, 1e9), axis=-1, keepdims=True)
        return jnp.where(sbf == first, -jnp.inf, sc)

    sc = lax.fori_loop(0, min(N_SELECT, n_sb), pick, score)
    chosen = jnp.where(valid & (sc == -jnp.inf), 1.0, 0.0)
    for j in range(bs):
        sel_ref[j] = chosen[j * nr8:(j + 1) * nr8]


def _cmp_topk_s(qs, cmp_t, gates, tq, past, bs):
    nb, nrow, _ = qs.shape
    n_cmp = cmp_t.shape[2]
    n_sb = -(-(past + tq) // SEL_BLOCK)
    n_cols = n_cmp + LANES
    assert n_sb <= (n_cols // LANES) * (LANES // 2) and nb % bs == 0
    b3 = lambda a: pl.BlockSpec((bs,) + a.shape[1:], lambda b: (b, 0, 0))
    return pl.pallas_call(
        functools.partial(_cmp_topk_s_kernel, tq=tq, past=past, n_cmp=n_cmp, n_sb=n_sb, n_cols=n_cols),
        grid=(nb // bs,),
        in_specs=[b3(qs), b3(cmp_t), b3(gates)],
        out_specs=(pl.BlockSpec((bs, nrow, 128), lambda b: (b, 0, 0)),
                   pl.BlockSpec((bs, NSA_KV_HEADS * tq, n_cols), lambda b: (b, 0, 0))),
        out_shape=(jax.ShapeDtypeStruct((nb, nrow, 128), F32),
                   jax.ShapeDtypeStruct((nb, NSA_KV_HEADS * tq, n_cols), F32)),
        compiler_params=_cparams(("parallel",)),
        name="cmp_topk_sample",
    )(qs, cmp_t, gates)


def _selwin_s_kernel(pt_ref, q_ref, sel_ref, g_ref, *refs, pp, tq, past):
    del pt_ref
    page_refs = refs[:pp]
    newsel_ref, winbuf_ref, newwin_ref, em_ref, osel_ref, owin_ref, m_ref, l_ref, acc_ref = refs[pp:]
    g = pl.program_id(1)
    ng = pl.num_programs(1)
    qs = q_ref[0]
    nrow = qs.shape[0]
    flags = sel_ref[0]
    n_cols = flags.shape[1]
    qpos = past + lax.broadcasted_iota(jnp.int32, (nrow, 1), 0) % tq

    @pl.when(g == 0)
    def _():
        m_ref[...] = jnp.full(m_ref.shape, NEG_INF, F32)
        l_ref[...] = jnp.zeros(l_ref.shape, F32)
        acc_ref[...] = jnp.zeros(acc_ref.shape, F32)

    def update(s, ok, vts):
        sm = s if ok is None else jnp.where(ok, s, NEG_INF)
        m_old = m_ref[...]
        m_new = jnp.maximum(m_old, jnp.max(sm, axis=-1, keepdims=True))
        alpha = jnp.exp2(m_old - m_new)
        p = jnp.exp2(sm - m_new)
        if ok is not None:
            p = jnp.where(ok, p, 0.0)
        l_ref[...] = alpha * l_ref[...] + jnp.sum(p, axis=-1, keepdims=True)
        acc_ref[...] = alpha * acc_ref[...] + _nt_dot(p.astype(BF16), vts)
        m_ref[...] = m_new

    def expand(first_block, n):
        ncol = lax.broadcasted_iota(jnp.int32, (1, n), 1)
        sbl, used = _flag_block(lax.broadcasted_iota(jnp.int32, (n_cols, 1), 0))
        emat = ((sbl == first_block + ncol // SEL_BLOCK) & used).astype(BF16)
        mf = _dot(flags.astype(BF16), emat)
        return jnp.concatenate([mf] * NSA_GROUP, axis=0) > 0.5

    n = pp * PAGE_SIZE
    pages = [page_refs[p][0] for p in range(pp)]
    kt_all = jnp.concatenate([pg[0:LANES].astype(BF16) for pg in pages], axis=1)
    vt_all = jnp.concatenate([pg[LANES:2 * LANES].astype(BF16) for pg in pages], axis=1)
    nsb = n // SEL_BLOCK
    chunk = sel_ref[0, :, pl.ds(pl.multiple_of((g // 2) * LANES, LANES), LANES)]
    cb = jnp.where(chunk > 0.5, 0.0, NEG_INF)
    cb = jnp.where(g % 2 == 0, cb, pltpu.roll(cb, LANES - nsb, 1))
    cb = jnp.where(lax.broadcasted_iota(jnp.int32, (1, LANES), 1) < nsb, cb, 0.0)
    lhs = jnp.concatenate([qs, jnp.concatenate([cb] * NSA_GROUP, axis=0).astype(BF16)], axis=1)
    update(_dot(lhs, jnp.concatenate([kt_all, em_ref[...]], axis=0)), None, vt_all)

    @pl.when(g == ng - 1)
    def _():
        new = newsel_ref[0]
        kpos_n = past + lax.broadcasted_iota(jnp.int32, (1, PAGE_SIZE), 1)
        update(_dot(qs, new[0:LANES].astype(BF16)),
               expand(past // SEL_BLOCK, PAGE_SIZE) & (kpos_n <= qpos), new[LANES:2 * LANES].astype(BF16))
        gates = g_ref[0]
        osel_ref[0] = acc_ref[...] / l_ref[...] * gates[:, 1:2]

        wb = winbuf_ref[0]
        nw = newwin_ref[0]
        n_buf = wb.shape[1]
        kw = jnp.concatenate([wb[0:LANES], nw[0:LANES]], axis=1).astype(BF16)
        vw = jnp.concatenate([wb[LANES:2 * LANES], nw[LANES:2 * LANES]], axis=1).astype(BF16)
        wpos = past - n_buf + lax.broadcasted_iota(jnp.int32, (1, n_buf + PAGE_SIZE), 1)
        okw = (wpos <= qpos) & (wpos > qpos - WINDOW) & (wpos >= 0)
        sw = jnp.where(okw, _dot(qs, kw), NEG_INF)
        e = jnp.where(okw, jnp.exp2(sw - jnp.max(sw, axis=-1, keepdims=True)), 0.0)
        owin_ref[0] = _nt_dot(e.astype(BF16), vw) / jnp.sum(e, axis=-1, keepdims=True) * gates[:, 2:3]


def _selwin_s(qs, sel, gates, pool, page_table, newsel, winbuf, newwin, tq, past, pp):
    nb, nrow, _ = qs.shape
    n_pages = page_table.shape[1]
    bmap = lambda b, g, pt: (b, 0, 0)
    b3 = lambda a: pl.BlockSpec((1,) + a.shape[1:], bmap)

    def page_spec(p):
        return pl.BlockSpec((1, 256, PAGE_SIZE), lambda b, g, pt: (pt[b * n_pages + g * pp + p], 0, 0))

    n = pp * PAGE_SIZE
    assert 4 * (n // SEL_BLOCK) == LANES
    em = jnp.asarray(np.arange(LANES)[:, None] == np.arange(n)[None, :] // SEL_BLOCK, BF16)
    grid_spec = pltpu.PrefetchScalarGridSpec(
        num_scalar_prefetch=1,
        grid=(nb, n_pages // pp),
        in_specs=[b3(qs), b3(sel), b3(gates)] + [page_spec(p) for p in range(pp)]
                 + [b3(newsel), b3(winbuf), b3(newwin), pl.BlockSpec(em.shape, lambda b, g, pt: (0, 0))],
        out_specs=(pl.BlockSpec((1, nrow, 128), bmap), pl.BlockSpec((1, nrow, 128), bmap)),
        scratch_shapes=[pltpu.VMEM((nrow, 1), F32), pltpu.VMEM((nrow, 1), F32), pltpu.VMEM((nrow, 128), F32)],
    )
    shp = jax.ShapeDtypeStruct((nb, nrow, 128), F32)
    return pl.pallas_call(
        functools.partial(_selwin_s_kernel, pp=pp, tq=tq, past=past),
        grid_spec=grid_spec,
        out_shape=(shp, shp),
        compiler_params=_cparams(("parallel", "arbitrary")),
        name="selwin_sample",
    )(page_table.reshape(-1), qs, sel, gates, *([pool] * pp), newsel, winbuf, newwin, em)


def _ret_kernel(rq_ref, rk_ref, rv_ref, sz_ref, s0_ref, dec_ref, qd_ref, kd_ref, gc_ref,
                o_ref, sout_ref, s_ref):
    c = pl.program_id(1)

    @pl.when(c == 0)
    def _():
        s_ref[...] = s0_ref[0]

    lane = lax.broadcasted_iota(jnp.int32, (1, LANES), 1)
    for pr in range(RET_HEADS // 2):
        cs = slice(pr * LANES, (pr + 1) * LANES)
        qc = rq_ref[:, cs]
        kc = rk_ref[:, cs]
        kb = kc.astype(BF16)
        state = s_ref[pr]
        sb = state.astype(BF16)
        new_state = gc_ref[pr] * state
        for hh in range(2):
            h = 2 * pr + hh
            hm = (lane >= RET_DK * hh) & (lane < RET_DK * (hh + 1))
            qm = jnp.where(hm, qc, 0.0)
            a = _nt_dot(qm.astype(BF16), kb) * dec_ref[h]
            v = rv_ref[:, h * RET_DV:(h + 1) * RET_DV]
            vb = v.astype(BF16)
            o = _dot(a.astype(BF16), vb) + _dot((qm * qd_ref[:, cs]).astype(BF16), sb)
            km = jnp.where(hm, kc * kd_ref[:, cs], 0.0)
            new_state = new_state + _dot(km.T.astype(BF16), vb)
            mu = jnp.mean(o, axis=-1, keepdims=True)
            var = jnp.mean(jnp.square(o - mu), axis=-1, keepdims=True)
            o_ref[:, h * RET_DV:(h + 1) * RET_DV] = ((o - mu) * lax.rsqrt(var + EPS)
                                                    * sz_ref[:, h * RET_DV:(h + 1) * RET_DV])
        s_ref[pr] = new_state
        sout_ref[0, pr] = new_state


def _retention(rq, rk, rv, szr, s0, c_real, nb, nc, cc):
    log_g = jnp.log1p(-jnp.exp2(-5.0 - jnp.arange(RET_HEADS, dtype=F32)))
    i = jnp.arange(cc, dtype=F32)
    rel = i[:, None] - i[None, :]
    dec = jnp.where(rel >= 0, jnp.exp(log_g[:, None, None] * jnp.maximum(rel, 0.0)), 0.0)
    qd = jnp.repeat(jnp.exp(log_g[None, :] * (i[:, None] + 1.0)), RET_DK, axis=1)
    kd = jnp.repeat(jnp.exp(log_g[None, :] * (c_real - 1.0 - i[:, None])), RET_DK, axis=1)
    gc = jnp.broadcast_to(jnp.repeat(jnp.exp(log_g * c_real), RET_DK).reshape(2, 128, 1), (2, 128, 128))
    row = lambda n: pl.BlockSpec((cc, n), lambda b, c: (b * nc + c, 0))
    st = pl.BlockSpec((1, 2, 128, 128), lambda b, c: (b, 0, 0, 0))
    t = rq.shape[0]
    return pl.pallas_call(
        _ret_kernel,
        grid=(nb, nc),
        in_specs=[row(256), row(256), row(512), row(512), st, _full((RET_HEADS, cc, cc)),
                  _full((cc, 256)), _full((cc, 256)), _full((2, 128, 128))],
        out_specs=(row(512), st),
        out_shape=(jax.ShapeDtypeStruct((t, 512), F32), jax.ShapeDtypeStruct((nb, 2, 128, 128), F32)),
        scratch_shapes=[pltpu.VMEM((2, 128, 128), F32)],
        compiler_params=_cparams(("parallel", "arbitrary")),
        name="retention",
    )(rq, rk, rv, szr, s0, dec, qd, kd, gc)


def _even_out_kernel(x_ref, *refs):
    mr_ref, w_ref, y_ref = refs[-3:]
    if len(refs) == 4:
        mixed = refs[0][...]
    else:
        oc_ref, os_ref, ow_ref, sz_ref = refs[:4]
        mixed = ((oc_ref[...] + os_ref[...] + ow_ref[...]) * sz_ref[...]).astype(BF16)
    y_ref[...] = (x_ref[...] + _dot(mixed, w_ref[0:512, :]) + _dot(mr_ref[...].astype(BF16), w_ref[512:1024, :]))


def _even_out(x, nsa_parts, mret, w_out, tm):
    t = x.shape[0]
    row = lambda n: pl.BlockSpec((tm, n), lambda i: (i, 0))
    return pl.pallas_call(
        _even_out_kernel,
        grid=(t // tm,),
        in_specs=[row(D_MODEL)] + [row(512)] * (len(nsa_parts) + 1) + [_full((1024, D_MODEL))],
        out_specs=row(D_MODEL),
        out_shape=jax.ShapeDtypeStruct((t, D_MODEL), F32),
        compiler_params=_cparams(("parallel",)),
        name="even_out",
    )(x, *nsa_parts, mret, w_out)


_HALO = 32
_CONV_ROWS = 16


def _rms(x, g):
    return (x * lax.rsqrt(jnp.mean(x * x, axis=-1, keepdims=True) + EPS) * g).astype(BF16)


def _glu(xn, w_ref):
    return _dot(xn, w_ref[:, 0:D_MODEL]) * _sigmoid(_dot(xn, w_ref[:, D_MODEL:2 * D_MODEL]))


def _ln_gate_out(x, c, z, lg_ref, lb_ref, wout_ref):
    mu = jnp.mean(c, axis=-1, keepdims=True)
    var = jnp.mean(jnp.square(c - mu), axis=-1, keepdims=True)
    yn = (c - mu) * lax.rsqrt(var + EPS) * lg_ref[...] + lb_ref[...]
    y = (yn * _sigmoid(yn)) * (z * _sigmoid(z))
    return x + _dot(y.astype(BF16), wout_ref[...])


def _odd_p_kernel(x_ref, xh_ref, g_ref, win_ref, cw_ref, cb_ref, lg_ref, lb_ref, wout_ref,
                  y_ref, ust_ref, u_scr, c_scr, *, tm):
    i = pl.program_id(0)
    x = x_ref[...]
    g = g_ref[...]
    xn = _rms(x, g)
    u = _glu(xn, win_ref)
    z = _dot(xn, win_ref[:, 2 * D_MODEL:3 * D_MODEL])
    uh = _glu(_rms(xh_ref[...], g), win_ref)
    u_scr[0, 0:_HALO, :] = jnp.where(i > 0, uh, 0.0)
    u_scr[0, _HALO:_HALO + tm, :] = u
    u_scr[0, _HALO + tm:_HALO + tm + 8, :] = jnp.zeros((8, D_MODEL), F32)
    ust_ref[...] = u[tm - _HALO:tm, :]
    for s in range(1, 8):
        u_scr[s, 0:_HALO + tm, :] = u_scr[0, s:s + _HALO + tm, :]
    off = _HALO - (CONV_WIDTH - 1)
    cb = cb_ref[...]
    nv = _CONV_ROWS // 8
    for rc in range(tm // _CONV_ROWS):
        r0 = rc * _CONV_ROWS
        acc = jnp.broadcast_to(cb, (_CONV_ROWS, D_MODEL)).reshape(nv, 8, D_MODEL)
        for j in range(CONV_WIDTH):
            a, s = divmod(j + off, 8)
            rows = u_scr[s, r0 + 8 * a:r0 + 8 * a + _CONV_ROWS, :]
            acc = acc + cw_ref[j][None] * rows.reshape(nv, 8, D_MODEL)
        c_scr[r0:r0 + _CONV_ROWS, :] = acc.reshape(_CONV_ROWS, D_MODEL)
    y_ref[...] = _ln_gate_out(x, c_scr[...], z, lg_ref, lb_ref, wout_ref)


def _odd_p(x, norm_g, w_in, cw, cb, lg, lb, w_out, tm):
    t = x.shape[0]
    row = pl.BlockSpec((tm, D_MODEL), lambda i: (i, 0))
    halo = pl.BlockSpec((_HALO, D_MODEL), lambda i: (jnp.maximum(i * (tm // _HALO) - 1, 0), 0))
    vec = _full((1, D_MODEL))
    return pl.pallas_call(
        functools.partial(_odd_p_kernel, tm=tm),
        grid=(t // tm,),
        in_specs=[row, halo, vec, _full((D_MODEL, 3 * D_MODEL)), _full((CONV_WIDTH, 8, D_MODEL)), vec, vec, vec,
                  _full((D_MODEL, D_MODEL))],
        out_specs=(row, _full((_HALO, D_MODEL))),
        out_shape=(jax.ShapeDtypeStruct((t, D_MODEL), F32), jax.ShapeDtypeStruct((_HALO, D_MODEL), F32)),
        scratch_shapes=[pltpu.VMEM((8, _HALO + tm + 8, D_MODEL), F32), pltpu.VMEM((tm, D_MODEL), F32)],
        compiler_params=_cparams(("arbitrary",)),
        name="odd_prompt",
    )(x, x, norm_g, w_in, cw, cb, lg, lb, w_out)


def _odd_s_kernel(x_ref, st_ref, g_ref, win_ref, cw_ref, cb_ref, lg_ref, lb_ref, wout_ref,
                  y_ref, u_ref, *, nb, tq):
    x = x_ref[...]
    xn = _rms(x, g_ref[...])
    u = _glu(xn, win_ref)
    z = _dot(xn, win_ref[:, 2 * D_MODEL:3 * D_MODEL])
    u_ref[...] = u
    n_state = CONV_WIDTH - 1
    cb = cb_ref[...]
    cs = []
    for t in range(tq):
        acc = jnp.broadcast_to(cb, (nb, D_MODEL))
        for j in range(CONV_WIDTH):
            e = t + j
            slab = st_ref[e * nb:(e + 1) * nb, :] if e < n_state else u[(e - n_state) * nb:(e - n_state + 1) * nb, :]
            acc = acc + cw_ref[j:j + 1, :] * slab
        cs.append(acc)
    y_ref[...] = _ln_gate_out(x, jnp.concatenate(cs, axis=0), z, lg_ref, lb_ref, wout_ref)


def _odd_s(x_tm, state_tm, norm_g, w_in, cw, cb, lg, lb, w_out, nb, tq):
    n = x_tm.shape[0]
    return pl.pallas_call(
        functools.partial(_odd_s_kernel, nb=nb, tq=tq),
        out_shape=(jax.ShapeDtypeStruct((n, D_MODEL), F32), jax.ShapeDtypeStruct((n, D_MODEL), F32)),
        compiler_params=pltpu.CompilerParams(vmem_limit_bytes=VMEM_LIMIT),
        name="odd_sample",
    )(x_tm, state_tm, norm_g, w_in, cw, cb, lg, lb, w_out)


def _rope_tables(pos):
    half = HEAD_DIM // 2
    freq = ROPE_THETA ** (-jnp.arange(half, dtype=F32) / half)
    ang = pos.astype(F32)[:, None] * freq[None, :]
    cos = jnp.cos(ang)
    sin = jnp.sin(ang)
    return jnp.tile(cos, (1, 4)), jnp.tile(jnp.concatenate([-sin, sin], axis=1), (1, 2))


def _even_weights(w_in):
    pq, kc, ks, kw, pg, zn, rq, rk, rv, zr = jnp.split(
        w_in, np.cumsum([512, 256, 256, 256, 24, 512, 256, 256, 512])[:].tolist(), axis=-1)
    zero = jnp.zeros((D_MODEL, HEAD_DIM), w_in.dtype)
    qcols = []
    for h in range(NSA_HEADS):
        wh = pq[:, h * HEAD_DIM:(h + 1) * HEAD_DIM]
        qcols += [wh, zero] if h < NSA_GROUP else [zero, wh]
    gpad = jnp.pad(pg, ((0, 0), (0, LANES - 3 * NSA_HEADS)))
    return jnp.concatenate(qcols + [kc, ks, kw, zn, rq, rk, rv, zr, gpad], axis=-1).astype(BF16)


def _pick_kv_half(o, nb, tq):
    o6 = o.reshape(nb, NSA_GROUP, NSA_KV_HEADS, tq, NSA_KV_HEADS, HEAD_DIM)
    o5 = jnp.stack([o6[:, :, 0, :, 0], o6[:, :, 1, :, 1]], axis=2)
    return o5.transpose(0, 3, 2, 1, 4).reshape(nb * tq, NSA_HEADS * HEAD_DIM)


def kernel(x_prompt, x_sample, cache_cmp_kv, cache_sel_kv, cache_win_kv, state_ret, state_conv, page_table,
           norm_even, w_in_even, q_norm_g, k_norm_g, w_cmp_pos, w_out_even,
           norm_odd, w_in_odd, conv_w, conv_b, ln_g, ln_b, w_out_odd):
    _, seq, _ = x_prompt.shape
    nb, tq, _ = x_sample.shape
    n_pages = page_table.shape[1]
    past = n_pages * PAGE_SIZE
    n_buf = cache_win_kv.shape[2]
    tm = 256
    tq_p = 128

    w_e = _even_weights(w_in_even[0])
    w_oe = w_out_even[0].astype(BF16)
    w_io = w_in_odd[0].astype(BF16)
    w_oo = w_out_odd[0].astype(BF16)
    gq = jnp.tile(q_norm_g[0], 2)[None, :]
    gk = jnp.tile(k_norm_g[0], (1, 2))
    w2 = jnp.repeat(w_cmp_pos[0].transpose(1, 0, 2).reshape(CMP_BLOCK, 4), HEAD_DIM, axis=1)
    ne = norm_even[0][None, :]
    no = norm_odd[0][None, :]

    xp = x_prompt[0]
    cos_p, sin_p = _rope_tables(jnp.arange(seq))
    (q_p, kvc_p, kvs_p, kvw_p, k2s_p, k2w_p, vts_p, vtw_p, gate_p, sz_p,
     rq_p, rk_p, rv_p, szr_p, kn_p) = _even_in(xp, ne, w_e, cos_p, sin_p, gq, gk, tm)
    ident = jnp.arange(seq // PAGE_SIZE, dtype=jnp.int32)[None, :]
    ce, co = _compress(kvc_p.reshape(seq // PAGE_SIZE, PAGE_SIZE, 256), ident, w2, 16)
    cmp_p = jnp.concatenate([ce[0], co[0]], axis=0)
    kc_p = cmp_p[:, :LANES].astype(BF16)
    vtc_p = cmp_p[:, LANES:].T.astype(BF16)
    gates_t = gate_p[:, :3 * NSA_HEADS].reshape(seq, NSA_HEADS, 3).transpose(2, 1, 0)
    oc_p, sel_p = _cmp_topk_p(q_p, kc_p, vtc_p, gates_t[0], tq_p)
    ow_p = _win_p(q_p, k2w_p, vtw_p, gates_t[2], tq_p, 2)
    mix_p = _sel_p(q_p, k2s_p, vts_p, kn_p, sel_p, gates_t[1], oc_p, ow_p, sz_p, tq_p, 256, 8)
    s0_p = jnp.zeros((1, 2, 128, 128), F32)
    ret_c = 4 * RET_CHUNK
    mret_p, s_p = _retention(rq_p, rk_p, rv_p, szr_p, s0_p, ret_c, 1, seq // ret_c, ret_c)
    y1_p = _even_out(xp, (mix_p,), mret_p, w_oe, tm)

    xs = x_sample.reshape(nb * tq, D_MODEL)
    cos_s, sin_s = _rope_tables(jnp.tile(past + jnp.arange(tq), nb))
    (q_s, kvc_s, kvs_s, kvw_s, _, _, _, _, gate_s, sz_s,
     rq_s, rk_s, rv_s, szr_s, _) = _even_in(xs, ne, w_e, cos_s, sin_s, gq, gk, nb * tq)
    rows_t = lambda c: c.transpose(0, 2, 3, 4, 1).reshape(c.shape[0], 256, c.shape[1])
    cmp_s = _compress_t(rows_t(cache_cmp_kv[0]), page_table, w_cmp_pos[0], LANES * CMP_BLOCK // PAGE_SIZE)
    qs_s = (q_s.reshape(nb, tq, NSA_KV_HEADS, NSA_GROUP, LANES).transpose(0, 3, 2, 1, 4)
            .reshape(nb, NSA_HEADS * tq, LANES))
    g_s = (gate_s[:, :3 * NSA_HEADS].reshape(nb, tq, NSA_KV_HEADS, NSA_GROUP, 3).transpose(0, 3, 2, 1, 4)
           .reshape(nb, NSA_HEADS * tq, 3))
    oc_s, sel_s = _cmp_topk_s(qs_s, cmp_s, g_s, tq, past, 4 if nb % 4 == 0 else 1)
    new_t = lambda r: jnp.pad(r.reshape(nb, tq, 256), ((0, 0), (0, PAGE_SIZE - tq), (0, 0))).transpose(0, 2, 1)
    os_s, ow_s = _selwin_s(qs_s, sel_s, g_s, rows_t(cache_sel_kv[0]), page_table,
                           new_t(kvs_s), rows_t(cache_win_kv[0]), new_t(kvw_s), tq, past, 16)
    pad_c = lambda r: jnp.pad(r.reshape(nb, tq, -1), ((0, 0), (0, RET_CHUNK - tq), (0, 0))).reshape(nb * RET_CHUNK, -1)
    mret_s, s_s = _retention(pad_c(rq_s), pad_c(rk_s), pad_c(rv_s), pad_c(szr_s),
                             state_ret[0].reshape(nb, 2, 128, 128), float(tq), nb, 1, RET_CHUNK)
    mret_s = mret_s.reshape(nb, RET_CHUNK, 512)[:, :tq].reshape(nb * tq, 512)
    y1_s = _even_out(xs, (_pick_kv_half(oc_s, nb, tq), _pick_kv_half(os_s, nb, tq), _pick_kv_half(ow_s, nb, tq),
                          sz_s), mret_s, w_oe, nb * tq)

    cb = conv_b[0][None, :]
    lg = ln_g[0][None, :]
    lb = ln_b[0][None, :]
    cw8 = jnp.broadcast_to(conv_w[0][:, None, :], (CONV_WIDTH, 8, D_MODEL))
    y2_p, ust_p = _odd_p(y1_p, no, w_io, cw8, cb, lg, lb, w_oo, tm)
    x_tm = y1_s.reshape(nb, tq, D_MODEL).transpose(1, 0, 2).reshape(tq * nb, D_MODEL)
    st_tm = state_conv[0].transpose(1, 0, 2).reshape((CONV_WIDTH - 1) * nb, D_MODEL)
    y2_tm, u_tm = _odd_s(x_tm, st_tm, no, w_io, conv_w[0], cb, lg, lb, w_oo, nb, tq)
    y2_s = y2_tm.reshape(tq, nb, D_MODEL).transpose(1, 0, 2)
    u_s = u_tm.reshape(tq, nb, D_MODEL).transpose(1, 0, 2)

    rows6 = lambda r, b, t: r.reshape(1, b, t, 2, NSA_KV_HEADS, HEAD_DIM)
    n_win = min(WINDOW, seq)
    win_s = jnp.concatenate([cache_win_kv[0].reshape(nb, n_buf, 256), kvw_s.reshape(nb, tq, 256)], axis=1)[:, -n_buf:]
    conv_s = jnp.concatenate([state_conv[0], u_s], axis=1)[:, -(CONV_WIDTH - 1):]
    return (y2_p[None], y2_s,
            rows6(kvc_p, 1, seq), rows6(kvc_s, nb, tq),
            rows6(kvs_p, 1, seq), rows6(kvs_s, nb, tq),
            rows6(kvw_p[seq - n_win:], 1, n_win), rows6(win_s, nb, n_buf),
            s_p.reshape(1, 1, RET_HEADS, RET_DK, RET_DV), s_s.reshape(1, nb, RET_HEADS, RET_DK, RET_DV),
            ust_p[_HALO - (CONV_WIDTH - 1):][None, None], conv_s[None])
```

```python
import functools

import numpy as np
import jax
import jax.numpy as jnp
from jax import lax
from jax.experimental import pallas as pl
from jax.experimental.pallas import tpu as pltpu

F32 = jnp.float32
BF16 = jnp.bfloat16

D_MODEL = 1024
PAGE_SIZE = 128
HEAD_DIM = 64
NSA_HEADS = 8
NSA_KV_HEADS = 2
NSA_GROUP = 4
CMP_BLOCK = 32
SEL_BLOCK = 64
N_SELECT = 16
WINDOW = 512
RET_HEADS = 4
RET_DK = 64
RET_DV = 128
RET_CHUNK = 128
CONV_WIDTH = 31
ROPE_THETA = 10000.0
NEG_INF = -1e30
FORCE = 1e9
EPS = 1e-6

LOG2E = 1.4426950408889634
Q_SCALE = HEAD_DIM ** -0.5 * LOG2E

LANES = 128
VMEM_LIMIT = 56 * 1024 * 1024

_QP = 0
_KV = 1024
_ZN = 1792
_RQ = 2304
_RK = 2560
_RV = 2816
_ZR = 3328
_GT = 3840
_EVEN_COLS = 3968


def _cparams(sem):
    return pltpu.CompilerParams(dimension_semantics=sem, vmem_limit_bytes=VMEM_LIMIT)


def _sigmoid(x):
    return 1.0 / (1.0 + jnp.exp(-x))


def _nt_dot(a, b):
    return lax.dot_general(a, b, (((1,), (1,)), ((), ())), preferred_element_type=F32)


def _dot(a, b):
    return jnp.dot(a, b, preferred_element_type=F32)


def _full(shape):
    n = len(shape)
    return pl.BlockSpec(shape, lambda *_: (0,) * n)


def _even_in_kernel(x_ref, g_ref, w_ref, cos_ref, sin_ref, gq_ref, gk_ref,
                    q_ref, kvc_ref, kvs_ref, kvw_ref, k2s_ref, k2w_ref, vts_ref, vtw_ref,
                    gate_ref, sz_ref, rq_ref, rk_ref, rv_ref, szr_ref, kn_ref):
    x = x_ref[...]
    ms = jnp.mean(x * x, axis=-1, keepdims=True)
    xn = (x * lax.rsqrt(ms + EPS) * g_ref[...]).astype(BF16)
    cos = cos_ref[...]
    sin = sin_ref[...]
    lane = lax.broadcasted_iota(jnp.int32, (1, LANES), 1)
    lo_half = (lane & 32) == 0
    gr = lax.broadcasted_iota(jnp.int32, (LANES, LANES), 0) // HEAD_DIM
    gc = lax.broadcasted_iota(jnp.int32, (LANES, LANES), 1) // HEAD_DIM
    gmat = (gr == gc).astype(BF16)

    def proj(a, b):
        return _dot(xn, w_ref[:, a:b])

    def head_rms(h, gain):
        h2 = h * h
        hi = h2.astype(BF16)
        lo = (h2 - hi.astype(F32)).astype(BF16)
        ss = _dot(hi, gmat) + _dot(lo, gmat)
        return h * lax.rsqrt(ss * (1.0 / HEAD_DIM) + EPS) * gain

    def rope(h):
        sw = jnp.where(lo_half, pltpu.roll(h, LANES - 32, 1), pltpu.roll(h, 32, 1))
        return h * cos + sw * sin

    hq = proj(_QP, _KV)
    gq = gq_ref[...]
    for c in range(NSA_HEADS):
        hc = hq[:, c * LANES:(c + 1) * LANES]
        q_ref[:, c * LANES:(c + 1) * LANES] = (rope(head_rms(hc, gq)) * Q_SCALE).astype(BF16)

    hk = proj(_KV, _ZN)
    kv_refs = (kvc_ref, kvs_ref, kvw_ref)
    k2_refs = (None, k2s_ref, k2w_ref)
    vt_refs = (None, vts_ref, vtw_ref)
    for b in range(3):
        k = rope(head_rms(hk[:, 2 * b * LANES:(2 * b + 1) * LANES], gk_ref[b:b + 1, :]))
        v = hk[:, (2 * b + 1) * LANES:(2 * b + 2) * LANES]
        kv_refs[b][:, 0:LANES] = k
        kv_refs[b][:, LANES:2 * LANES] = v
        if k2_refs[b] is not None:
            kb = k.astype(BF16)
            k2_refs[b][...] = kb
            vt = v.T.astype(BF16)
            tail = jnp.where(lax.broadcasted_iota(jnp.int32, (_V_ROWS - HEAD_DIM, 1), 0) == 0, 1.0, 0.0)
            tail = jnp.broadcast_to(tail, (_V_ROWS - HEAD_DIM, vt.shape[1])).astype(BF16)
            for kh in range(NSA_KV_HEADS):
                vt_refs[b][kh, 0:HEAD_DIM, :] = vt[HEAD_DIM * kh:HEAD_DIM * (kh + 1), :]
                vt_refs[b][kh, HEAD_DIM:_V_ROWS, :] = tail
            if b == 1:
                kf = kb.astype(F32)
                k2 = kf * kf
                hi = k2.astype(BF16)
                n2 = _dot(hi, gmat) + _dot((k2 - hi.astype(F32)).astype(BF16), gmat)
                kn_ref[0] = jnp.broadcast_to(jnp.max(n2, axis=0, keepdims=True), (8, LANES)) * 1.001

    hz = proj(_ZN, _RQ)
    sz_ref[...] = hz * _sigmoid(hz)
    hrq = proj(_RQ, _RK)
    hrk = proj(_RK, _RV)
    for c in range(2):
        rq_ref[:, c * LANES:(c + 1) * LANES] = rope(hrq[:, c * LANES:(c + 1) * LANES])
        rk_ref[:, c * LANES:(c + 1) * LANES] = rope(hrk[:, c * LANES:(c + 1) * LANES]) * (RET_DK ** -0.5)
    rv_ref[...] = proj(_RV, _ZR)
    hzr = proj(_ZR, _GT)
    szr_ref[...] = hzr * _sigmoid(hzr)
    gate_ref[...] = _sigmoid(proj(_GT, _EVEN_COLS))


def _even_in(x, norm_g, w_e, cos, sin, gq, gk, tm):
    t = x.shape[0]
    row = lambda n: pl.BlockSpec((tm, n), lambda i: (i, 0))
    out_shape = (
        jax.ShapeDtypeStruct((t, 1024), BF16),
        jax.ShapeDtypeStruct((t, 256), F32),
        jax.ShapeDtypeStruct((t, 256), F32),
        jax.ShapeDtypeStruct((t, 256), F32),
        jax.ShapeDtypeStruct((t, 128), BF16),
        jax.ShapeDtypeStruct((t, 128), BF16),
        jax.ShapeDtypeStruct((NSA_KV_HEADS, _V_ROWS, t), BF16),
        jax.ShapeDtypeStruct((NSA_KV_HEADS, _V_ROWS, t), BF16),
        jax.ShapeDtypeStruct((t, 128), F32),
        jax.ShapeDtypeStruct((t, 512), F32),
        jax.ShapeDtypeStruct((t, 256), F32),
        jax.ShapeDtypeStruct((t, 256), F32),
        jax.ShapeDtypeStruct((t, 512), F32),
        jax.ShapeDtypeStruct((t, 512), F32),
        jax.ShapeDtypeStruct((t // tm, 8, 128), F32),
    )
    vt_spec = pl.BlockSpec((NSA_KV_HEADS, _V_ROWS, tm), lambda i: (0, 0, i))
    out_specs = (row(1024), row(256), row(256), row(256), row(128), row(128), vt_spec, vt_spec,
                 row(128), row(512), row(256), row(256), row(512), row(512),
                 pl.BlockSpec((1, 8, 128), lambda i: (i, 0, 0)))
    return pl.pallas_call(
        _even_in_kernel,
        grid=(t // tm,),
        in_specs=[row(D_MODEL), _full((1, D_MODEL)), _full((D_MODEL, _EVEN_COLS)),
                  row(128), row(128), _full((1, 128)), _full((3, 128))],
        out_specs=out_specs,
        out_shape=out_shape,
        compiler_params=_cparams(("parallel",)),
        name="even_in",
    )(x, norm_g, w_e, cos, sin, gq, gk)


def _compress_kernel(pt_ref, *refs, pp):
    del pt_ref
    page_refs = refs[:pp]
    w_ref, oe_ref, oo_ref = refs[pp:]
    w = w_ref[...]
    for p in range(pp):
        x = page_refs[p][0]
        s = jnp.sum(x.reshape(PAGE_SIZE // CMP_BLOCK, CMP_BLOCK, 256) * w[None], axis=1)
        oe_ref[0, 2 * p:2 * p + 1, :] = s[0:1]
        oo_ref[0, 2 * p:2 * p + 1, :] = s[1:2]
        oe_ref[0, 2 * p + 1:2 * p + 2, :] = s[2:3]
        oo_ref[0, 2 * p + 1:2 * p + 2, :] = s[3:4]


def _compress(pool, page_table, w2, pp):
    nb, n_pages = page_table.shape

    def page_spec(p):
        return pl.BlockSpec((1, PAGE_SIZE, 256), lambda b, g, pt: (pt[b * n_pages + g * pp + p], 0, 0))

    out_spec = pl.BlockSpec((1, 2 * pp, 256), lambda b, g, pt: (b, g, 0))
    grid_spec = pltpu.PrefetchScalarGridSpec(
        num_scalar_prefetch=1,
        grid=(nb, n_pages // pp),
        in_specs=[page_spec(p) for p in range(pp)] + [pl.BlockSpec((CMP_BLOCK, 256), lambda b, g, pt: (0, 0))],
        out_specs=(out_spec, out_spec),
    )
    shp = jax.ShapeDtypeStruct((nb, 2 * n_pages, 256), F32)
    return pl.pallas_call(
        functools.partial(_compress_kernel, pp=pp),
        grid_spec=grid_spec,
        out_shape=(shp, shp),
        compiler_params=_cparams(("parallel", "parallel")),
        name="compress",
    )(page_table.reshape(-1), *([pool] * pp), w2)


def _cmp_lane(w):
    return (w % 2) * (LANES // 2) + w // 2


def _compress_t_kernel(pt_ref, *refs, pp):
    del pt_ref
    page_refs = refs[:pp]
    w_ref, g_ref, o_ref = refs[pp:]
    w = w_ref[...]
    prods = [(page_refs[p][0] * w).astype(BF16) for p in range(pp)]
    o_ref[0] = _dot(jnp.concatenate(prods, axis=1), g_ref[...])


def _compress_t(pool_t, page_table, w_pos, pp):
    nb, n_pages = page_table.shape
    per_page = PAGE_SIZE // CMP_BLOCK
    assert pp * per_page == LANES
    wf = jnp.tile(jnp.broadcast_to(w_pos.transpose(0, 2, 1)[:, :, None, :],
                                   (2, NSA_KV_HEADS, HEAD_DIM, CMP_BLOCK)).reshape(256, CMP_BLOCK), (1, per_page))
    r = jnp.arange(pp * PAGE_SIZE)
    gm = (_cmp_lane(r // CMP_BLOCK)[:, None] == jnp.arange(LANES)[None, :]).astype(BF16)

    def page_spec(p):
        return pl.BlockSpec((1, 256, PAGE_SIZE), lambda b, g, pt: (pt[b * n_pages + g * pp + p], 0, 0))

    grid_spec = pltpu.PrefetchScalarGridSpec(
        num_scalar_prefetch=1,
        grid=(nb, n_pages // pp),
        in_specs=[page_spec(p) for p in range(pp)]
                 + [pl.BlockSpec((256, PAGE_SIZE), lambda b, g, pt: (0, 0)),
                    pl.BlockSpec((pp * PAGE_SIZE, LANES), lambda b, g, pt: (0, 0))],
        out_specs=pl.BlockSpec((1, 256, LANES), lambda b, g, pt: (b, 0, g)),
    )
    return pl.pallas_call(
        functools.partial(_compress_t_kernel, pp=pp),
        grid_spec=grid_spec,
        out_shape=jax.ShapeDtypeStruct((nb, 256, per_page * n_pages), F32),
        compiler_params=_cparams(("parallel", "parallel")),
        name="compress_sample",
    )(page_table.reshape(-1), *([pool_t] * pp), wf, gm)


def _stack_heads(q, k):
    return jnp.concatenate([q[:, (NSA_GROUP * k + g) * LANES:(NSA_GROUP * k + g + 1) * LANES]
                            for g in range(NSA_GROUP)], axis=0)


def _store_heads(o_ref, og, k, tq):
    for j in range(2):
        blk = jnp.concatenate([og[2 * j], og[2 * j + 1]], axis=0)
        o_ref[:, (2 * k + j) * LANES:(2 * k + j + 1) * LANES] = blk.T


_CMP_VARIANTS = 4


def _cmp_topk_p_kernel(q_ref, kc_ref, vtc_ref, g_ref, o_ref, sel_ref, *, tq, n_half, nq):
    i = pl.program_id(0)
    qpos = i * tq + lax.broadcasted_iota(jnp.int32, (1, tq), 1)
    sb = lax.broadcasted_iota(jnp.int32, (n_half, 1), 0)
    sbf = sb.astype(F32)
    forced = (sb == qpos // SEL_BLOCK) | (sb == 0)
    future = sb * SEL_BLOCK > qpos
    q = q_ref[...]
    gates = g_ref[...]

    def attend(nh):
        r = lax.broadcasted_iota(jnp.int32, (2 * nh, 1), 0)
        blk = jnp.where(r < nh, 2 * r, 2 * (r - nh) + 1)
        mask = ((blk + 1) * CMP_BLOCK - 1) <= qpos
        kc = jnp.concatenate([kc_ref[0:nh, :], kc_ref[n_half:n_half + nh, :]], axis=0)
        scores = []
        for k in range(NSA_KV_HEADS):
            s = _nt_dot(kc, _stack_heads(q, k))
            imp = jnp.zeros((2 * nh, tq), F32)
            ps = []
            for g in range(NSA_GROUP):
                sg = jnp.where(mask, s[:, g * tq:(g + 1) * tq], NEG_INF)
                e = jnp.exp2(sg - jnp.max(sg, axis=0, keepdims=True))
                p = jnp.where(mask, e / jnp.sum(e, axis=0, keepdims=True), 0.0)
                imp = imp + p
                ps.append(p.astype(BF16))
            pt = jnp.concatenate(ps, axis=1)
            vk = slice(HEAD_DIM * k, HEAD_DIM * (k + 1))
            ot = (_dot(vtc_ref[vk, 0:nh], pt[0:nh]) + _dot(vtc_ref[vk, n_half:n_half + nh], pt[nh:2 * nh]))
            og = [ot[:, g * tq:(g + 1) * tq] * gates[NSA_GROUP * k + g:NSA_GROUP * k + g + 1, :]
                  for g in range(NSA_GROUP)]
            _store_heads(o_ref, og, k, tq)
            scores.append(jnp.where(forced[:nh], -jnp.inf, jnp.where(future[:nh], -FORCE, imp[:nh] + imp[nh:])))

        idx = sbf[:nh]

        def pick(rnd, scs):
            out = []
            for sc in scs:
                mx = jnp.max(sc, axis=0, keepdims=True)
                first = jnp.min(jnp.where(sc == mx, idx, 1e9), axis=0, keepdims=True)
                out.append(jnp.where((idx == first) & (rnd < need), -jnp.inf, sc))
            return tuple(out)

        scs = lax.fori_loop(0, n_rounds, pick, tuple(scores))
        for k in range(NSA_KV_HEADS):
            sel_ref[0, k, 0:nh, :] = jnp.where(scs[k] == -jnp.inf, 0.0, NEG_INF)
            if nh < n_half:
                sel_ref[0, k, nh:n_half, :] = jnp.full((n_half - nh, tq), NEG_INF, F32)

    n_forced = 1 + (qpos >= SEL_BLOCK).astype(jnp.int32)
    need = min(N_SELECT, n_half) - n_forced
    n_rounds = min(N_SELECT, n_half) - 1 - (i > 0).astype(jnp.int32)
    per_var = nq // _CMP_VARIANTS
    for c in range(_CMP_VARIANTS):
        pl.when(i // per_var == c)(functools.partial(attend, (c + 1) * n_half // _CMP_VARIANTS))


def _cmp_topk_p(q, kc, vtc, gates0, tq):
    t = q.shape[0]
    n_half = kc.shape[0] // 2
    nq = t // tq
    assert nq % _CMP_VARIANTS == 0 and (n_half // _CMP_VARIANTS) % 8 == 0
    assert n_half * SEL_BLOCK >= t
    return pl.pallas_call(
        functools.partial(_cmp_topk_p_kernel, tq=tq, n_half=n_half, nq=nq),
        grid=(nq,),
        in_specs=[pl.BlockSpec((tq, 1024), lambda i: (i, 0)), _full(kc.shape), _full(vtc.shape),
                  pl.BlockSpec((NSA_HEADS, tq), lambda i: (0, i))],
        out_specs=(pl.BlockSpec((tq, 512), lambda i: (i, 0)),
                   pl.BlockSpec((1, NSA_KV_HEADS, n_half, tq), lambda i: (i, 0, 0, 0))),
        out_shape=(jax.ShapeDtypeStruct((t, 512), F32),
                   jax.ShapeDtypeStruct((nq, NSA_KV_HEADS, n_half, tq), F32)),
        compiler_params=_cparams(("parallel",)),
        name="cmp_topk_prompt",
    )(q, kc, vtc, gates0)


_V_ROWS = HEAD_DIM + 16
_X_ROWS = 16
_SAFE_BOUND = 60.0


def _sel_p_kernel(q_ref, k2_ref, vt_ref, kn_ref, ex_ref, sel_ref, g_ref, oc_ref, ow_ref, sz_ref, o_ref,
                  acc_ref, m_ref, l_ref, s_ref, p_ref, a_ref, qt_ref, os_ref, *, tq, tk, ns):
    i = pl.program_id(0)
    qpos = i * tq + lax.broadcasted_iota(jnp.int32, (1, tq), 1)
    q = q_ref[...]
    qts = [_stack_heads(q, k).astype(F32).T for k in range(NSA_KV_HEADS)]
    nblk = tk // SEL_BLOCK
    assert 2 * nblk == 8 and tk % tq == 0 and ns % 2 == 0 and (k2_ref.shape[0] // tk) % ns == 0
    n_half = sel_ref.shape[2]
    kiota = lax.broadcasted_iota(jnp.int32, (tk, 1), 0)
    n_tiles = (i * tq + tq + tk - 1) // tk
    n_steps = (n_tiles + ns - 1) // ns
    max_tile = k2_ref.shape[0] // tk - 1
    gates = g_ref[...]

    def finish(numer, denom):
        for k in range(NSA_KV_HEADS):
            ot = numer(k) / denom(k)
            og = [ot[:, g * tq:(g + 1) * tq] * gates[NSA_GROUP * k + g:NSA_GROUP * k + g + 1, :]
                  for g in range(NSA_GROUP)]
            _store_heads(os_ref, og, k, tq)
        o_ref[...] = ((oc_ref[...] + os_ref[...] + ow_ref[...]) * sz_ref[...]).astype(BF16)

    kn = jnp.max(kn_ref[...].reshape(-1, LANES), axis=0, keepdims=True)
    bounds = []
    for k in range(NSA_KV_HEADS):
        kmax2 = jnp.max(kn[:, HEAD_DIM * k:HEAD_DIM * (k + 1)], axis=1, keepdims=True)
        qn2 = jnp.sum(qts[k] * qts[k], axis=0, keepdims=True)
        bounds.append(jnp.sqrt(qn2 * kmax2) * 1.02 + 1.0)
    safe = jnp.maximum(jnp.max(bounds[0]), jnp.max(bounds[1])) < _SAFE_BOUND
    lane = lax.broadcasted_iota(jnp.int32, (1, LANES), 1)
    row8 = lax.broadcasted_iota(jnp.int32, (_X_ROWS - 8, 1), 0)
    tails = [jnp.where(row8 == 0, -b, 0.0) for b in bounds]

    def f_scores(j, s1):
        slot = s1 % ns
        kt = jnp.minimum(ns * j + s1, max_tile)
        ktile = k2_ref[pl.ds(pl.multiple_of(kt * tk, tk), tk), :]
        grp = pl.multiple_of(jnp.minimum((j * (ns // 2) + s1 // 2) * 8, n_half - 8), 8)
        for k in range(NSA_KV_HEADS):
            r0 = HEAD_DIM * (1 - k)
            rows8 = jnp.concatenate([sel_ref[0, k, pl.ds(grp, 8), :]] * NSA_GROUP, axis=1)
            qt_ref[slot, k, r0:r0 + _X_ROWS, :] = jnp.concatenate([rows8, tails[k]], axis=0).astype(BF16)
            own = (lane >= HEAD_DIM * k) & (lane < HEAD_DIM * (k + 1))
            lhs = jnp.where(own, ktile, ex_ref[s1 % 2, k])
            s_ref[slot, k] = _dot(lhs, qt_ref[slot, k])

    def f_values(kt, slot):
        kt = jnp.clip(kt, 0, max_tile)
        for k in range(NSA_KV_HEADS):
            acc_ref[k] = acc_ref[k] + _dot(vt_ref[k, :, pl.ds(pl.multiple_of(kt * tk, tk), tk)], p_ref[slot, k])

    def f_softmax(j, slot, masked):
        visible = (ns * j + slot) * tk + kiota <= qpos
        for k in range(NSA_KV_HEADS):
            for g in range(NSA_GROUP):
                cs = slice(g * tq, (g + 1) * tq)
                p = jnp.exp2(s_ref[slot, k, :, cs])
                if masked:
                    p = jnp.where(visible, p, 0.0)
                p_ref[slot, k, :, cs] = p.astype(BF16)

    def f_step(j, masked):
        for s in range(ns):
            f_values(ns * j + s - 1, (s - 1) % ns)
            f_softmax(j, s, masked)
            f_scores(j, s + 1)

    def fast_path():
        for k in range(NSA_KV_HEADS):
            w = qts[k].astype(BF16)
            for slot in range(ns):
                qt_ref[slot, k] = w
        acc_ref[...] = jnp.zeros(acc_ref.shape, F32)
        p_ref[...] = jnp.zeros(p_ref.shape, BF16)
        f_scores(0, 0)

        def body(j, carry):
            f_step(j, False)
            return carry

        lax.fori_loop(0, n_steps - 1, body, 0)
        f_step(n_steps - 1, True)
        f_values(ns * n_steps - 1, ns - 1)
        finish(lambda k: acc_ref[k, 0:HEAD_DIM], lambda k: acc_ref[k, HEAD_DIM:HEAD_DIM + 1])

    def scores(kt, slot):
        kt = jnp.minimum(kt, max_tile)
        ktile = k2_ref[pl.ds(pl.multiple_of(kt * tk, tk), tk), :]
        for k in range(NSA_KV_HEADS):
            s_ref[slot, k] = _dot(ktile, qt_ref[0, k])

    def values(kt, slot):
        kt = jnp.clip(kt, 0, max_tile)
        for k in range(NSA_KV_HEADS):
            vtile = vt_ref[k, 0:HEAD_DIM, pl.ds(pl.multiple_of(kt * tk, tk), tk)]
            acc_ref[k, 0:HEAD_DIM] = acc_ref[k, 0:HEAD_DIM] * a_ref[slot, k] + _dot(vtile, p_ref[slot, k])

    def softmax(j, slot):
        visible = (ns * j + slot) * tk + kiota <= qpos
        grp = pl.multiple_of((j * (ns // 2) + slot // 2) * 8, 8)
        for k in range(NSA_KV_HEADS):
            rows = sel_ref[0, k, pl.ds(grp, 8), :][(slot % 2) * nblk:(slot % 2 + 1) * nblk]
            bias = jnp.concatenate([jnp.broadcast_to(rows[r:r + 1, :], (SEL_BLOCK, tq)) for r in range(nblk)], axis=0)
            bias = jnp.where(visible, bias, NEG_INF)
            for g in range(NSA_GROUP):
                cs = slice(g * tq, (g + 1) * tq)
                sg = s_ref[slot, k, :, cs] + bias
                m_old = m_ref[k, :, cs]
                m_new = jnp.maximum(m_old, jnp.max(sg, axis=0, keepdims=True))
                alpha = jnp.exp2(m_old - m_new)
                p = jnp.exp2(sg - m_new)
                l_ref[k, :, cs] = alpha * l_ref[k, :, cs] + jnp.sum(p, axis=0, keepdims=True)
                m_ref[k, :, cs] = m_new
                a_ref[slot, k, :, cs] = alpha
                p_ref[slot, k, :, cs] = p.astype(BF16)

    def body(j, carry):
        for s in range(ns):
            scores(ns * j + s + 1, (s + 1) % ns)
            values(ns * j + s - 1, (s - 1) % ns)
            softmax(j, s)
        return carry

    def exact_path():
        for k in range(NSA_KV_HEADS):
            qt_ref[0, k] = qts[k].astype(BF16)
        m_ref[...] = jnp.full(m_ref.shape, NEG_INF, F32)
        l_ref[...] = jnp.zeros(l_ref.shape, F32)
        acc_ref[...] = jnp.zeros(acc_ref.shape, F32)
        p_ref[...] = jnp.zeros(p_ref.shape, BF16)
        a_ref[...] = jnp.ones(a_ref.shape, F32)
        scores(0, 0)
        lax.fori_loop(0, n_steps, body, 0)
        values(ns * n_steps - 1, ns - 1)
        finish(lambda k: acc_ref[k, 0:HEAD_DIM], lambda k: l_ref[k])

    lax.cond(safe, fast_path, exact_path)


def _sel_extra_columns(tk):
    row = np.arange(tk)[:, None]
    lane = np.arange(LANES)[None, :]
    ex = np.zeros((2, NSA_KV_HEADS, tk, LANES), np.float32)
    for par in range(2):
        for k in range(NSA_KV_HEADS):
            r0 = HEAD_DIM * (1 - k)
            ex[par, k] = (lane == r0 + par * (tk // SEL_BLOCK) + row // SEL_BLOCK) | (lane == r0 + 8)
    return jnp.asarray(ex, BF16)


def _sel_p(q, k2, vt, kn, sel, gates1, oc, ow, sz, tq, tk, ns):
    t = q.shape[0]
    n_half = sel.shape[2]
    ex = _sel_extra_columns(tk)
    return pl.pallas_call(
        functools.partial(_sel_p_kernel, tq=tq, tk=tk, ns=ns),
        grid=(t // tq,),
        in_specs=[pl.BlockSpec((tq, 1024), lambda i: (i, 0)), _full(k2.shape), _full(vt.shape), _full(kn.shape),
                  _full(ex.shape),
                  pl.BlockSpec((1, NSA_KV_HEADS, n_half, tq), lambda i: (i, 0, 0, 0)),
                  pl.BlockSpec((NSA_HEADS, tq), lambda i: (0, i))] + [pl.BlockSpec((tq, 512), lambda i: (i, 0))] * 3,
        out_specs=pl.BlockSpec((tq, 512), lambda i: (i, 0)),
        out_shape=jax.ShapeDtypeStruct((t, 512), BF16),
        scratch_shapes=[pltpu.VMEM((NSA_KV_HEADS, _V_ROWS, NSA_GROUP * tq), F32),
                        pltpu.VMEM((NSA_KV_HEADS, 1, NSA_GROUP * tq), F32),
                        pltpu.VMEM((NSA_KV_HEADS, 1, NSA_GROUP * tq), F32),
                        pltpu.VMEM((ns, NSA_KV_HEADS, tk, NSA_GROUP * tq), F32),
                        pltpu.VMEM((ns, NSA_KV_HEADS, tk, NSA_GROUP * tq), BF16),
                        pltpu.VMEM((ns, NSA_KV_HEADS, 1, NSA_GROUP * tq), F32),
                        pltpu.VMEM((ns, NSA_KV_HEADS, LANES, NSA_GROUP * tq), BF16),
                        pltpu.VMEM((tq, 512), F32)],
        compiler_params=_cparams(("parallel",)),
        name="sel_prompt",
    )(q, k2, vt, kn, ex, sel, gates1, oc, ow, sz)


def _win_p_kernel(q_ref, k2_ref, vt_ref, g_ref, o_ref, *, tq, nsub):
    nk = WINDOW + tq
    lane = lax.broadcasted_iota(jnp.int32, (1, LANES), 1)
    rowx = lax.broadcasted_iota(jnp.int32, (LANES, 1), 0)
    blocks = []
    for sub in range(nsub):
        i = pl.program_id(0) * nsub + sub
        qpos = i * tq + lax.broadcasted_iota(jnp.int32, (1, tq), 1)
        start = pl.multiple_of(jnp.maximum(i * tq - WINDOW, 0), tq)
        kpos = start + lax.broadcasted_iota(jnp.int32, (nk, 1), 0)
        ok = (kpos <= qpos) & (kpos > qpos - WINDOW)
        bias = jnp.where(ok, 0.0, NEG_INF)
        q = q_ref[sub * tq:(sub + 1) * tq, :]
        ktile = k2_ref[pl.ds(start, nk), :]
        kf = ktile.astype(F32)
        kmax2 = jnp.max(jnp.sum(kf * kf, axis=1, keepdims=True), axis=0, keepdims=True)
        qts = [_stack_heads(q, k).astype(F32).T for k in range(NSA_KV_HEADS)]
        bounds = [jnp.sqrt(jnp.sum(t * t, axis=0, keepdims=True) * kmax2) * 1.02 + 1.0 for t in qts]
        blocks.append(dict(start=start, bias4=jnp.concatenate([bias] * NSA_GROUP, axis=1), q=q, ktile=ktile,
                           qts=qts, bounds=bounds, gates=g_ref[:, sub * tq:(sub + 1) * tq],
                           out=o_ref.at[sub * tq:(sub + 1) * tq, :]))
    safe = functools.reduce(jnp.maximum, [jnp.max(b) for blk in blocks for b in blk["bounds"]]) < _SAFE_BOUND

    def finish(blk, k, ot):
        og = [ot[:, g * tq:(g + 1) * tq] * blk["gates"][NSA_GROUP * k + g:NSA_GROUP * k + g + 1, :]
              for g in range(NSA_GROUP)]
        _store_heads(blk["out"], og, k, tq)

    def fast_path():
        for blk in blocks:
            for k in range(NSA_KV_HEADS):
                r0 = HEAD_DIM * (1 - k)
                own = (lane >= HEAD_DIM * k) & (lane < HEAD_DIM * (k + 1))
                lhs = jnp.where(own, blk["ktile"], jnp.where(lane == r0, 1.0, 0.0).astype(BF16))
                w = jnp.where(rowx == r0, -blk["bounds"][k], blk["qts"][k]).astype(BF16)
                p = jnp.exp2(_dot(lhs, w) + blk["bias4"]).astype(BF16)
                acc = _dot(vt_ref[k, :, pl.ds(blk["start"], nk)], p)
                finish(blk, k, acc[0:HEAD_DIM] / acc[HEAD_DIM:HEAD_DIM + 1])

    def exact_path():
        for blk in blocks:
            for k in range(NSA_KV_HEADS):
                s = _nt_dot(blk["ktile"], _stack_heads(blk["q"], k)) + blk["bias4"]
                p = jnp.exp2(s - jnp.max(s, axis=0, keepdims=True))
                l = jnp.sum(p, axis=0, keepdims=True)
                finish(blk, k, _dot(vt_ref[k, 0:HEAD_DIM, pl.ds(blk["start"], nk)], p.astype(BF16)) / l)

    lax.cond(safe, fast_path, exact_path)


def _win_p(q, k2, vt, gates2, tq, nsub):
    t = q.shape[0]
    tq, tq1 = tq * nsub, tq
    return pl.pallas_call(
        functools.partial(_win_p_kernel, tq=tq1, nsub=nsub),
        grid=(t // tq,),
        in_specs=[pl.BlockSpec((tq, 1024), lambda i: (i, 0)), _full(k2.shape), _full(vt.shape),
                  pl.BlockSpec((NSA_HEADS, tq), lambda i: (0, i))],
        out_specs=pl.BlockSpec((tq, 512), lambda i: (i, 0)),
        out_shape=jax.ShapeDtypeStruct((t, 512), F32),
        compiler_params=_cparams(("parallel",)),
        name="win_prompt",
    )(q, k2, vt, gates2)


def _flag_block(lane):
    r = lane % LANES
    return (lane // LANES) * (LANES // 2) + r, r < LANES // 2


def _cmp_topk_s_kernel(q_ref, ct_ref, g_ref, o_ref, sel_ref, *, tq, past, n_cmp, n_sb, n_cols):
    bs, nrow, _ = q_ref.shape
    nr8 = NSA_KV_HEADS * tq
    qpos = past + lax.broadcasted_iota(jnp.int32, (nrow, 1), 0) % tq
    col = lax.broadcasted_iota(jnp.int32, (1, n_cmp), 1)
    r = col % LANES
    blk = (col // LANES) * LANES + 2 * (r % (LANES // 2)) + r // (LANES // 2)
    mask = ((blk + 1) * CMP_BLOCK - 1) <= qpos
    imps = []
    for j in range(bs):
        qs = q_ref[j]
        ct = ct_ref[j]
        sm = jnp.where(mask, _dot(qs, ct[0:LANES].astype(BF16)), NEG_INF)
        e = jnp.exp2(sm - jnp.max(sm, axis=-1, keepdims=True))
        p = jnp.where(mask, e / jnp.sum(e, axis=-1, keepdims=True), 0.0)
        o_ref[j] = _nt_dot(p.astype(BF16), ct[LANES:2 * LANES].astype(BF16)) * g_ref[j][:, 0:1]
        imp = p[0:nr8]
        for g in range(1, NSA_GROUP):
            imp = imp + p[g * nr8:(g + 1) * nr8]
        pairs = []
        for c in range(n_cmp // LANES):
            ch = imp[:, c * LANES:(c + 1) * LANES]
            pairs.append(ch + pltpu.roll(ch, LANES // 2, 1))
        imps.append(jnp.concatenate(pairs + [jnp.zeros((nr8, n_cols - n_cmp), F32)], axis=1))
    imp = jnp.concatenate(imps, axis=0)
    qpos8 = past + lax.broadcasted_iota(jnp.int32, (bs * nr8, 1), 0) % tq
    sb, used = _flag_block(lax.broadcasted_iota(jnp.int32, (1, n_cols), 1))
    sbf = sb.astype(F32)
    score = jnp.where((sb == qpos8 // SEL_BLOCK) | (sb == 0), FORCE, imp)
    score = jnp.where(sb * SEL_BLOCK > qpos8, -FORCE, score)
    valid = used & (sb < n_sb)
    score = jnp.where(valid, score, -jnp.inf)

    def pick(_, sc):
        mx = jnp.max(sc, axis=-1, keepdims=True)
        first = jnp.min(jnp.where(sc == mx, sbf, 1e9), axis=-1, keepdims=True)
        return jnp.where(sbf == first, -jnp.inf, sc)

    sc = lax.fori_loop(0, min(N_SELECT, n_sb), pick, score)
    chosen = jnp.where(valid & (sc == -jnp.inf), 1.0, 0.0)
    for j in range(bs):
        sel_ref[j] = chosen[j * nr8:(j + 1) * nr8]


def _cmp_topk_s(qs, cmp_t, gates, tq, past, bs):
    nb, nrow, _ = qs.shape
    n_cmp = cmp_t.shape[2]
    n_sb = -(-(past + tq) // SEL_BLOCK)
    n_cols = n_cmp + LANES
    assert n_sb <= (n_cols // LANES) * (LANES // 2) and nb % bs == 0
    b3 = lambda a: pl.BlockSpec((bs,) + a.shape[1:], lambda b: (b, 0, 0))
    return pl.pallas_call(
        functools.partial(_cmp_topk_s_kernel, tq=tq, past=past, n_cmp=n_cmp, n_sb=n_sb, n_cols=n_cols),
        grid=(nb // bs,),
        in_specs=[b3(qs), b3(cmp_t), b3(gates)],
        out_specs=(pl.BlockSpec((bs, nrow, 128), lambda b: (b, 0, 0)),
                   pl.BlockSpec((bs, NSA_KV_HEADS * tq, n_cols), lambda b: (b, 0, 0))),
        out_shape=(jax.ShapeDtypeStruct((nb, nrow, 128), F32),
                   jax.ShapeDtypeStruct((nb, NSA_KV_HEADS * tq, n_cols), F32)),
        compiler_params=_cparams(("parallel",)),
        name="cmp_topk_sample",
    )(qs, cmp_t, gates)


def _selwin_s_kernel(pt_ref, q_ref, sel_ref, g_ref, *refs, pp, tq, past):
    del pt_ref
    page_refs = refs[:pp]
    newsel_ref, winbuf_ref, newwin_ref, em_ref, osel_ref, owin_ref, m_ref, l_ref, acc_ref = refs[pp:]
    g = pl.program_id(1)
    ng = pl.num_programs(1)
    qs = q_ref[0]
    nrow = qs.shape[0]
    flags = sel_ref[0]
    n_cols = flags.shape[1]
    qpos = past + lax.broadcasted_iota(jnp.int32, (nrow, 1), 0) % tq

    @pl.when(g == 0)
    def _():
        m_ref[...] = jnp.full(m_ref.shape, NEG_INF, F32)
        l_ref[...] = jnp.zeros(l_ref.shape, F32)
        acc_ref[...] = jnp.zeros(acc_ref.shape, F32)

    def update(s, ok, vts):
        sm = s if ok is None else jnp.where(ok, s, NEG_INF)
        m_old = m_ref[...]
        m_new = jnp.maximum(m_old, jnp.max(sm, axis=-1, keepdims=True))
        alpha = jnp.exp2(m_old - m_new)
        p = jnp.exp2(sm - m_new)
        if ok is not None:
            p = jnp.where(ok, p, 0.0)
        l_ref[...] = alpha * l_ref[...] + jnp.sum(p, axis=-1, keepdims=True)
        acc_ref[...] = alpha * acc_ref[...] + _nt_dot(p.astype(BF16), vts)
        m_ref[...] = m_new

    def expand(first_block, n):
        ncol = lax.broadcasted_iota(jnp.int32, (1, n), 1)
        sbl, used = _flag_block(lax.broadcasted_iota(jnp.int32, (n_cols, 1), 0))
        emat = ((sbl == first_block + ncol // SEL_BLOCK) & used).astype(BF16)
        mf = _dot(flags.astype(BF16), emat)
        return jnp.concatenate([mf] * NSA_GROUP, axis=0) > 0.5

    n = pp * PAGE_SIZE
    pages = [page_refs[p][0] for p in range(pp)]
    kt_all = jnp.concatenate([pg[0:LANES].astype(BF16) for pg in pages], axis=1)
    vt_all = jnp.concatenate([pg[LANES:2 * LANES].astype(BF16) for pg in pages], axis=1)
    nsb = n // SEL_BLOCK
    chunk = sel_ref[0, :, pl.ds(pl.multiple_of((g // 2) * LANES, LANES), LANES)]
    cb = jnp.where(chunk > 0.5, 0.0, NEG_INF)
    cb = jnp.where(g % 2 == 0, cb, pltpu.roll(cb, LANES - nsb, 1))
    cb = jnp.where(lax.broadcasted_iota(jnp.int32, (1, LANES), 1) < nsb, cb, 0.0)
    lhs = jnp.concatenate([qs, jnp.concatenate([cb] * NSA_GROUP, axis=0).astype(BF16)], axis=1)
    update(_dot(lhs, jnp.concatenate([kt_all, em_ref[...]], axis=0)), None, vt_all)

    @pl.when(g == ng - 1)
    def _():
        new = newsel_ref[0]
        kpos_n = past + lax.broadcasted_iota(jnp.int32, (1, PAGE_SIZE), 1)
        update(_dot(qs, new[0:LANES].astype(BF16)),
               expand(past // SEL_BLOCK, PAGE_SIZE) & (kpos_n <= qpos), new[LANES:2 * LANES].astype(BF16))
        gates = g_ref[0]
        osel_ref[0] = acc_ref[...] / l_ref[...] * gates[:, 1:2]

        wb = winbuf_ref[0]
        nw = newwin_ref[0]
        n_buf = wb.shape[1]
        kw = jnp.concatenate([wb[0:LANES], nw[0:LANES]], axis=1).astype(BF16)
        vw = jnp.concatenate([wb[LANES:2 * LANES], nw[LANES:2 * LANES]], axis=1).astype(BF16)
        wpos = past - n_buf + lax.broadcasted_iota(jnp.int32, (1, n_buf + PAGE_SIZE), 1)
        okw = (wpos <= qpos) & (wpos > qpos - WINDOW) & (wpos >= 0)
        sw = jnp.where(okw, _dot(qs, kw), NEG_INF)
        e = jnp.where(okw, jnp.exp2(sw - jnp.max(sw, axis=-1, keepdims=True)), 0.0)
        owin_ref[0] = _nt_dot(e.astype(BF16), vw) / jnp.sum(e, axis=-1, keepdims=True) * gates[:, 2:3]


def _selwin_s(qs, sel, gates, pool, page_table, newsel, winbuf, newwin, tq, past, pp):
    nb, nrow, _ = qs.shape
    n_pages = page_table.shape[1]
    bmap = lambda b, g, pt: (b, 0, 0)
    b3 = lambda a: pl.BlockSpec((1,) + a.shape[1:], bmap)

    def page_spec(p):
        return pl.BlockSpec((1, 256, PAGE_SIZE), lambda b, g, pt: (pt[b * n_pages + g * pp + p], 0, 0))

    n = pp * PAGE_SIZE
    assert 4 * (n // SEL_BLOCK) == LANES
    em = jnp.asarray(np.arange(LANES)[:, None] == np.arange(n)[None, :] // SEL_BLOCK, BF16)
    grid_spec = pltpu.PrefetchScalarGridSpec(
        num_scalar_prefetch=1,
        grid=(nb, n_pages // pp),
        in_specs=[b3(qs), b3(sel), b3(gates)] + [page_spec(p) for p in range(pp)]
                 + [b3(newsel), b3(winbuf), b3(newwin), pl.BlockSpec(em.shape, lambda b, g, pt: (0, 0))],
        out_specs=(pl.BlockSpec((1, nrow, 128), bmap), pl.BlockSpec((1, nrow, 128), bmap)),
        scratch_shapes=[pltpu.VMEM((nrow, 1), F32), pltpu.VMEM((nrow, 1), F32), pltpu.VMEM((nrow, 128), F32)],
    )
    shp = jax.ShapeDtypeStruct((nb, nrow, 128), F32)
    return pl.pallas_call(
        functools.partial(_selwin_s_kernel, pp=pp, tq=tq, past=past),
        grid_spec=grid_spec,
        out_shape=(shp, shp),
        compiler_params=_cparams(("parallel", "arbitrary")),
        name="selwin_sample",
    )(page_table.reshape(-1), qs, sel, gates, *([pool] * pp), newsel, winbuf, newwin, em)


def _ret_kernel(rq_ref, rk_ref, rv_ref, sz_ref, s0_ref, dec_ref, qd_ref, kd_ref, gc_ref,
                o_ref, sout_ref, s_ref):
    c = pl.program_id(1)

    @pl.when(c == 0)
    def _():
        s_ref[...] = s0_ref[0]

    lane = lax.broadcasted_iota(jnp.int32, (1, LANES), 1)
    for pr in range(RET_HEADS // 2):
        cs = slice(pr * LANES, (pr + 1) * LANES)
        qc = rq_ref[:, cs]
        kc = rk_ref[:, cs]
        kb = kc.astype(BF16)
        state = s_ref[pr]
        sb = state.astype(BF16)
        new_state = gc_ref[pr] * state
        for hh in range(2):
            h = 2 * pr + hh
            hm = (lane >= RET_DK * hh) & (lane < RET_DK * (hh + 1))
            qm = jnp.where(hm, qc, 0.0)
            a = _nt_dot(qm.astype(BF16), kb) * dec_ref[h]
            v = rv_ref[:, h * RET_DV:(h + 1) * RET_DV]
            vb = v.astype(BF16)
            o = _dot(a.astype(BF16), vb) + _dot((qm * qd_ref[:, cs]).astype(BF16), sb)
            km = jnp.where(hm, kc * kd_ref[:, cs], 0.0)
            new_state = new_state + _dot(km.T.astype(BF16), vb)
            mu = jnp.mean(o, axis=-1, keepdims=True)
            var = jnp.mean(jnp.square(o - mu), axis=-1, keepdims=True)
            o_ref[:, h * RET_DV:(h + 1) * RET_DV] = ((o - mu) * lax.rsqrt(var + EPS)
                                                    * sz_ref[:, h * RET_DV:(h + 1) * RET_DV])
        s_ref[pr] = new_state
        sout_ref[0, pr] = new_state


def _retention(rq, rk, rv, szr, s0, c_real, nb, nc, cc):
    log_g = jnp.log1p(-jnp.exp2(-5.0 - jnp.arange(RET_HEADS, dtype=F32)))
    i = jnp.arange(cc, dtype=F32)
    rel = i[:, None] - i[None, :]
    dec = jnp.where(rel >= 0, jnp.exp(log_g[:, None, None] * jnp.maximum(rel, 0.0)), 0.0)
    qd = jnp.repeat(jnp.exp(log_g[None, :] * (i[:, None] + 1.0)), RET_DK, axis=1)
    kd = jnp.repeat(jnp.exp(log_g[None, :] * (c_real - 1.0 - i[:, None])), RET_DK, axis=1)
    gc = jnp.broadcast_to(jnp.repeat(jnp.exp(log_g * c_real), RET_DK).reshape(2, 128, 1), (2, 128, 128))
    row = lambda n: pl.BlockSpec((cc, n), lambda b, c: (b * nc + c, 0))
    st = pl.BlockSpec((1, 2, 128, 128), lambda b, c: (b, 0, 0, 0))
    t = rq.shape[0]
    return pl.pallas_call(
        _ret_kernel,
        grid=(nb, nc),
        in_specs=[row(256), row(256), row(512), row(512), st, _full((RET_HEADS, cc, cc)),
                  _full((cc, 256)), _full((cc, 256)), _full((2, 128, 128))],
        out_specs=(row(512), st),
        out_shape=(jax.ShapeDtypeStruct((t, 512), F32), jax.ShapeDtypeStruct((nb, 2, 128, 128), F32)),
        scratch_shapes=[pltpu.VMEM((2, 128, 128), F32)],
        compiler_params=_cparams(("parallel", "arbitrary")),
        name="retention",
    )(rq, rk, rv, szr, s0, dec, qd, kd, gc)


def _even_out_kernel(x_ref, *refs):
    mr_ref, w_ref, y_ref = refs[-3:]
    if len(refs) == 4:
        mixed = refs[0][...]
    else:
        oc_ref, os_ref, ow_ref, sz_ref = refs[:4]
        mixed = ((oc_ref[...] + os_ref[...] + ow_ref[...]) * sz_ref[...]).astype(BF16)
    y_ref[...] = (x_ref[...] + _dot(mixed, w_ref[0:512, :]) + _dot(mr_ref[...].astype(BF16), w_ref[512:1024, :]))


def _even_out(x, nsa_parts, mret, w_out, tm):
    t = x.shape[0]
    row = lambda n: pl.BlockSpec((tm, n), lambda i: (i, 0))
    return pl.pallas_call(
        _even_out_kernel,
        grid=(t // tm,),
        in_specs=[row(D_MODEL)] + [row(512)] * (len(nsa_parts) + 1) + [_full((1024, D_MODEL))],
        out_specs=row(D_MODEL),
        out_shape=jax.ShapeDtypeStruct((t, D_MODEL), F32),
        compiler_params=_cparams(("parallel",)),
        name="even_out",
    )(x, *nsa_parts, mret, w_out)


_HALO = 32
_CONV_ROWS = 16


def _rms(x, g):
    return (x * lax.rsqrt(jnp.mean(x * x, axis=-1, keepdims=True) + EPS) * g).astype(BF16)


def _glu(xn, w_ref):
    return _dot(xn, w_ref[:, 0:D_MODEL]) * _sigmoid(_dot(xn, w_ref[:, D_MODEL:2 * D_MODEL]))


def _ln_gate_out(x, c, z, lg_ref, lb_ref, wout_ref):
    mu = jnp.mean(c, axis=-1, keepdims=True)
    var = jnp.mean(jnp.square(c - mu), axis=-1, keepdims=True)
    yn = (c - mu) * lax.rsqrt(var + EPS) * lg_ref[...] + lb_ref[...]
    y = (yn * _sigmoid(yn)) * (z * _sigmoid(z))
    return x + _dot(y.astype(BF16), wout_ref[...])


def _odd_p_kernel(x_ref, xh_ref, g_ref, win_ref, cw_ref, cb_ref, lg_ref, lb_ref, wout_ref,
                  y_ref, ust_ref, u_scr, c_scr, *, tm):
    i = pl.program_id(0)
    x = x_ref[...]
    g = g_ref[...]
    xn = _rms(x, g)
    u = _glu(xn, win_ref)
    z = _dot(xn, win_ref[:, 2 * D_MODEL:3 * D_MODEL])
    uh = _glu(_rms(xh_ref[...], g), win_ref)
    u_scr[0, 0:_HALO, :] = jnp.where(i > 0, uh, 0.0)
    u_scr[0, _HALO:_HALO + tm, :] = u
    u_scr[0, _HALO + tm:_HALO + tm + 8, :] = jnp.zeros((8, D_MODEL), F32)
    ust_ref[...] = u[tm - _HALO:tm, :]
    for s in range(1, 8):
        u_scr[s, 0:_HALO + tm, :] = u_scr[0, s:s + _HALO + tm, :]
    off = _HALO - (CONV_WIDTH - 1)
    cb = cb_ref[...]
    nv = _CONV_ROWS // 8
    for rc in range(tm // _CONV_ROWS):
        r0 = rc * _CONV_ROWS
        acc = jnp.broadcast_to(cb, (_CONV_ROWS, D_MODEL)).reshape(nv, 8, D_MODEL)
        for j in range(CONV_WIDTH):
            a, s = divmod(j + off, 8)
            rows = u_scr[s, r0 + 8 * a:r0 + 8 * a + _CONV_ROWS, :]
            acc = acc + cw_ref[j][None] * rows.reshape(nv, 8, D_MODEL)
        c_scr[r0:r0 + _CONV_ROWS, :] = acc.reshape(_CONV_ROWS, D_MODEL)
    y_ref[...] = _ln_gate_out(x, c_scr[...], z, lg_ref, lb_ref, wout_ref)


def _odd_p(x, norm_g, w_in, cw, cb, lg, lb, w_out, tm):
    t = x.shape[0]
    row = pl.BlockSpec((tm, D_MODEL), lambda i: (i, 0))
    halo = pl.BlockSpec((_HALO, D_MODEL), lambda i: (jnp.maximum(i * (tm // _HALO) - 1, 0), 0))
    vec = _full((1, D_MODEL))
    return pl.pallas_call(
        functools.partial(_odd_p_kernel, tm=tm),
        grid=(t // tm,),
        in_specs=[row, halo, vec, _full((D_MODEL, 3 * D_MODEL)), _full((CONV_WIDTH, 8, D_MODEL)), vec, vec, vec,
                  _full((D_MODEL, D_MODEL))],
        out_specs=(row, _full((_HALO, D_MODEL))),
        out_shape=(jax.ShapeDtypeStruct((t, D_MODEL), F32), jax.ShapeDtypeStruct((_HALO, D_MODEL), F32)),
        scratch_shapes=[pltpu.VMEM((8, _HALO + tm + 8, D_MODEL), F32), pltpu.VMEM((tm, D_MODEL), F32)],
        compiler_params=_cparams(("arbitrary",)),
        name="odd_prompt",
    )(x, x, norm_g, w_in, cw, cb, lg, lb, w_out)


def _odd_s_kernel(x_ref, st_ref, g_ref, win_ref, cw_ref, cb_ref, lg_ref, lb_ref, wout_ref,
                  y_ref, u_ref, *, nb, tq):
    x = x_ref[...]
    xn = _rms(x, g_ref[...])
    u = _glu(xn, win_ref)
    z = _dot(xn, win_ref[:, 2 * D_MODEL:3 * D_MODEL])
    u_ref[...] = u
    n_state = CONV_WIDTH - 1
    cb = cb_ref[...]
    cs = []
    for t in range(tq):
        acc = jnp.broadcast_to(cb, (nb, D_MODEL))
        for j in range(CONV_WIDTH):
            e = t + j
            slab = st_ref[e * nb:(e + 1) * nb, :] if e < n_state else u[(e - n_state) * nb:(e - n_state + 1) * nb, :]
            acc = acc + cw_ref[j:j + 1, :] * slab
        cs.append(acc)
    y_ref[...] = _ln_gate_out(x, jnp.concatenate(cs, axis=0), z, lg_ref, lb_ref, wout_ref)


def _odd_s(x_tm, state_tm, norm_g, w_in, cw, cb, lg, lb, w_out, nb, tq):
    n = x_tm.shape[0]
    return pl.pallas_call(
        functools.partial(_odd_s_kernel, nb=nb, tq=tq),
        out_shape=(jax.ShapeDtypeStruct((n, D_MODEL), F32), jax.ShapeDtypeStruct((n, D_MODEL), F32)),
        compiler_params=pltpu.CompilerParams(vmem_limit_bytes=VMEM_LIMIT),
        name="odd_sample",
    )(x_tm, state_tm, norm_g, w_in, cw, cb, lg, lb, w_out)


def _rope_tables(pos):
    half = HEAD_DIM // 2
    freq = ROPE_THETA ** (-jnp.arange(half, dtype=F32) / half)
    ang = pos.astype(F32)[:, None] * freq[None, :]
    cos = jnp.cos(ang)
    sin = jnp.sin(ang)
    return jnp.tile(cos, (1, 4)), jnp.tile(jnp.concatenate([-sin, sin], axis=1), (1, 2))


def _even_weights(w_in):
    pq, kc, ks, kw, pg, zn, rq, rk, rv, zr = jnp.split(
        w_in, np.cumsum([512, 256, 256, 256, 24, 512, 256, 256, 512])[:].tolist(), axis=-1)
    zero = jnp.zeros((D_MODEL, HEAD_DIM), w_in.dtype)
    qcols = []
    for h in range(NSA_HEADS):
        wh = pq[:, h * HEAD_DIM:(h + 1) * HEAD_DIM]
        qcols += [wh, zero] if h < NSA_GROUP else [zero, wh]
    gpad = jnp.pad(pg, ((0, 0), (0, LANES - 3 * NSA_HEADS)))
    return jnp.concatenate(qcols + [kc, ks, kw, zn, rq, rk, rv, zr, gpad], axis=-1).astype(BF16)


def _pick_kv_half(o, nb, tq):
    o6 = o.reshape(nb, NSA_GROUP, NSA_KV_HEADS, tq, NSA_KV_HEADS, HEAD_DIM)
    o5 = jnp.stack([o6[:, :, 0, :, 0], o6[:, :, 1, :, 1]], axis=2)
    return o5.transpose(0, 3, 2, 1, 4).reshape(nb * tq, NSA_HEADS * HEAD_DIM)


def kernel(x_prompt, x_sample, cache_cmp_kv, cache_sel_kv, cache_win_kv, state_ret, state_conv, page_table,
           norm_even, w_in_even, q_norm_g, k_norm_g, w_cmp_pos, w_out_even,
           norm_odd, w_in_odd, conv_w, conv_b, ln_g, ln_b, w_out_odd):
    _, seq, _ = x_prompt.shape
    nb, tq, _ = x_sample.shape
    n_pages = page_table.shape[1]
    past = n_pages * PAGE_SIZE
    n_buf = cache_win_kv.shape[2]
    tm = 256
    tq_p = 128

    w_e = _even_weights(w_in_even[0])
    w_oe = w_out_even[0].astype(BF16)
    w_io = w_in_odd[0].astype(BF16)
    w_oo = w_out_odd[0].astype(BF16)
    gq = jnp.tile(q_norm_g[0], 2)[None, :]
    gk = jnp.tile(k_norm_g[0], (1, 2))
    w2 = jnp.repeat(w_cmp_pos[0].transpose(1, 0, 2).reshape(CMP_BLOCK, 4), HEAD_DIM, axis=1)
    ne = norm_even[0][None, :]
    no = norm_odd[0][None, :]

    xp = x_prompt[0]
    cos_p, sin_p = _rope_tables(jnp.arange(seq))
    (q_p, kvc_p, kvs_p, kvw_p, k2s_p, k2w_p, vts_p, vtw_p, gate_p, sz_p,
     rq_p, rk_p, rv_p, szr_p, kn_p) = _even_in(xp, ne, w_e, cos_p, sin_p, gq, gk, tm)
    ident = jnp.arange(seq // PAGE_SIZE, dtype=jnp.int32)[None, :]
    ce, co = _compress(kvc_p.reshape(seq // PAGE_SIZE, PAGE_SIZE, 256), ident, w2, 16)
    cmp_p = jnp.concatenate([ce[0], co[0]], axis=0)
    kc_p = cmp_p[:, :LANES].astype(BF16)
    vtc_p = cmp_p[:, LANES:].T.astype(BF16)
    gates_t = gate_p[:, :3 * NSA_HEADS].reshape(seq, NSA_HEADS, 3).transpose(2, 1, 0)
    oc_p, sel_p = _cmp_topk_p(q_p, kc_p, vtc_p, gates_t[0], tq_p)
    ow_p = _win_p(q_p, k2w_p, vtw_p, gates_t[2], tq_p, 4)
    mix_p = _sel_p(q_p, k2s_p, vts_p, kn_p, sel_p, gates_t[1], oc_p, ow_p, sz_p, tq_p, 256, 8)
    s0_p = jnp.zeros((1, 2, 128, 128), F32)
    ret_c = 4 * RET_CHUNK
    mret_p, s_p = _retention(rq_p, rk_p, rv_p, szr_p, s0_p, ret_c, 1, seq // ret_c, ret_c)
    y1_p = _even_out(xp, (mix_p,), mret_p, w_oe, tm)

    xs = x_sample.reshape(nb * tq, D_MODEL)
    cos_s, sin_s = _rope_tables(jnp.tile(past + jnp.arange(tq), nb))
    (q_s, kvc_s, kvs_s, kvw_s, _, _, _, _, gate_s, sz_s,
     rq_s, rk_s, rv_s, szr_s, _) = _even_in(xs, ne, w_e, cos_s, sin_s, gq, gk, nb * tq)
    rows_t = lambda c: c.transpose(0, 2, 3, 4, 1).reshape(c.shape[0], 256, c.shape[1])
    cmp_s = _compress_t(rows_t(cache_cmp_kv[0]), page_table, w_cmp_pos[0], LANES * CMP_BLOCK // PAGE_SIZE)
    qs_s = (q_s.reshape(nb, tq, NSA_KV_HEADS, NSA_GROUP, LANES).transpose(0, 3, 2, 1, 4)
            .reshape(nb, NSA_HEADS * tq, LANES))
    g_s = (gate_s[:, :3 * NSA_HEADS].reshape(nb, tq, NSA_KV_HEADS, NSA_GROUP, 3).transpose(0, 3, 2, 1, 4)
           .reshape(nb, NSA_HEADS * tq, 3))
    oc_s, sel_s = _cmp_topk_s(qs_s, cmp_s, g_s, tq, past, 4 if nb % 4 == 0 else 1)
    new_t = lambda r: jnp.pad(r.reshape(nb, tq, 256), ((0, 0), (0, PAGE_SIZE - tq), (0, 0))).transpose(0, 2, 1)
    os_s, ow_s = _selwin_s(qs_s, sel_s, g_s, rows_t(cache_sel_kv[0]), page_table,
                           new_t(kvs_s), rows_t(cache_win_kv[0]), new_t(kvw_s), tq, past, 16)
    pad_c = lambda r: jnp.pad(r.reshape(nb, tq, -1), ((0, 0), (0, RET_CHUNK - tq), (0, 0))).reshape(nb * RET_CHUNK, -1)
    mret_s, s_s = _retention(pad_c(rq_s), pad_c(rk_s), pad_c(rv_s), pad_c(szr_s),
                             state_ret[0].reshape(nb, 2, 128, 128), float(tq), nb, 1, RET_CHUNK)
    mret_s = mret_s.reshape(nb, RET_CHUNK, 512)[:, :tq].reshape(nb * tq, 512)
    y1_s = _even_out(xs, (_pick_kv_half(oc_s, nb, tq), _pick_kv_half(os_s, nb, tq), _pick_kv_half(ow_s, nb, tq),
                          sz_s), mret_s, w_oe, nb * tq)

    cb = conv_b[0][None, :]
    lg = ln_g[0][None, :]
    lb = ln_b[0][None, :]
    cw8 = jnp.broadcast_to(conv_w[0][:, None, :], (CONV_WIDTH, 8, D_MODEL))
    y2_p, ust_p = _odd_p(y1_p, no, w_io, cw8, cb, lg, lb, w_oo, tm)
    x_tm = y1_s.reshape(nb, tq, D_MODEL).transpose(1, 0, 2).reshape(tq * nb, D_MODEL)
    st_tm = state_conv[0].transpose(1, 0, 2).reshape((CONV_WIDTH - 1) * nb, D_MODEL)
    y2_tm, u_tm = _odd_s(x_tm, st_tm, no, w_io, conv_w[0], cb, lg, lb, w_oo, nb, tq)
    y2_s = y2_tm.reshape(tq, nb, D_MODEL).transpose(1, 0, 2)
    u_s = u_tm.reshape(tq, nb, D_MODEL).transpose(1, 0, 2)

    rows6 = lambda r, b, t: r.reshape(1, b, t, 2, NSA_KV_HEADS, HEAD_DIM)
    n_win = min(WINDOW, seq)
    win_s = jnp.concatenate([cache_win_kv[0].reshape(nb, n_buf, 256), kvw_s.reshape(nb, tq, 256)], axis=1)[:, -n_buf:]
    conv_s = jnp.concatenate([state_conv[0], u_s], axis=1)[:, -(CONV_WIDTH - 1):]
    return (y2_p[None], y2_s,
            rows6(kvc_p, 1, seq), rows6(kvc_s, nb, tq),
            rows6(kvs_p, 1, seq), rows6(kvs_s, nb, tq),
            rows6(kvw_p[seq - n_win:], 1, n_win), rows6(win_s, nb, n_buf),
            s_p.reshape(1, 1, RET_HEADS, RET_DK, RET_DV), s_s.reshape(1, nb, RET_HEADS, RET_DK, RET_DV),
            ust_p[_HALO - (CONV_WIDTH - 1):][None, None], conv_s[None])
```

```python
import functools

import numpy as np
import jax
import jax.numpy as jnp
from jax import lax
from jax.experimental import pallas as pl
from jax.experimental.pallas import tpu as pltpu

F32 = jnp.float32
BF16 = jnp.bfloat16

D_MODEL = 1024
PAGE_SIZE = 128
HEAD_DIM = 64
NSA_HEADS = 8
NSA_KV_HEADS = 2
NSA_GROUP = 4
CMP_BLOCK = 32
SEL_BLOCK = 64
N_SELECT = 16
WINDOW = 512
RET_HEADS = 4
RET_DK = 64
RET_DV = 128
RET_CHUNK = 128
CONV_WIDTH = 31
ROPE_THETA = 10000.0
NEG_INF = -1e30
FORCE = 1e9
EPS = 1e-6

LOG2E = 1.4426950408889634
Q_SCALE = HEAD_DIM ** -0.5 * LOG2E

LANES = 128
VMEM_LIMIT = 56 * 1024 * 1024

_QP = 0
_KV = 1024
_ZN = 1792
_RQ = 2304
_RK = 2560
_RV = 2816
_ZR = 3328
_GT = 3840
_EVEN_COLS = 3968


def _cparams(sem):
    return pltpu.CompilerParams(dimension_semantics=sem, vmem_limit_bytes=VMEM_LIMIT)


def _sigmoid(x):
    return 1.0 / (1.0 + jnp.exp(-x))


def _nt_dot(a, b):
    return lax.dot_general(a, b, (((1,), (1,)), ((), ())), preferred_element_type=F32)


def _dot(a, b):
    return jnp.dot(a, b, preferred_element_type=F32)


def _full(shape):
    n = len(shape)
    return pl.BlockSpec(shape, lambda *_: (0,) * n)


def _even_in_kernel(x_ref, g_ref, w_ref, cos_ref, sin_ref, gq_ref, gk_ref,
                    q_ref, kvc_ref, kvs_ref, kvw_ref, k2s_ref, k2w_ref, vts_ref, vtw_ref,
                    gate_ref, sz_ref, rq_ref, rk_ref, rv_ref, szr_ref, kn_ref):
    x = x_ref[...]
    ms = jnp.mean(x * x, axis=-1, keepdims=True)
    xn = (x * lax.rsqrt(ms + EPS) * g_ref[...]).astype(BF16)
    cos = cos_ref[...]
    sin = sin_ref[...]
    lane = lax.broadcasted_iota(jnp.int32, (1, LANES), 1)
    lo_half = (lane & 32) == 0
    gr = lax.broadcasted_iota(jnp.int32, (LANES, LANES), 0) // HEAD_DIM
    gc = lax.broadcasted_iota(jnp.int32, (LANES, LANES), 1) // HEAD_DIM
    gmat = (gr == gc).astype(BF16)

    def proj(a, b):
        return _dot(xn, w_ref[:, a:b])

    def head_rms(h, gain):
        h2 = h * h
        hi = h2.astype(BF16)
        lo = (h2 - hi.astype(F32)).astype(BF16)
        ss = _dot(hi, gmat) + _dot(lo, gmat)
        return h * lax.rsqrt(ss * (1.0 / HEAD_DIM) + EPS) * gain

    def rope(h):
        sw = jnp.where(lo_half, pltpu.roll(h, LANES - 32, 1), pltpu.roll(h, 32, 1))
        return h * cos + sw * sin

    hq = proj(_QP, _KV)
    gq = gq_ref[...]
    for c in range(NSA_HEADS):
        hc = hq[:, c * LANES:(c + 1) * LANES]
        q_ref[:, c * LANES:(c + 1) * LANES] = (rope(head_rms(hc, gq)) * Q_SCALE).astype(BF16)

    hk = proj(_KV, _ZN)
    kv_refs = (kvc_ref, kvs_ref, kvw_ref)
    k2_refs = (None, k2s_ref, k2w_ref)
    vt_refs = (None, vts_ref, vtw_ref)
    for b in range(3):
        k = rope(head_rms(hk[:, 2 * b * LANES:(2 * b + 1) * LANES], gk_ref[b:b + 1, :]))
        v = hk[:, (2 * b + 1) * LANES:(2 * b + 2) * LANES]
        kv_refs[b][:, 0:LANES] = k
        kv_refs[b][:, LANES:2 * LANES] = v
        if k2_refs[b] is not None:
            kb = k.astype(BF16)
            k2_refs[b][...] = kb
            vt = v.T.astype(BF16)
            tail = jnp.where(lax.broadcasted_iota(jnp.int32, (_V_ROWS - HEAD_DIM, 1), 0) == 0, 1.0, 0.0)
            tail = jnp.broadcast_to(tail, (_V_ROWS - HEAD_DIM, vt.shape[1])).astype(BF16)
            for kh in range(NSA_KV_HEADS):
                vt_refs[b][kh, 0:HEAD_DIM, :] = vt[HEAD_DIM * kh:HEAD_DIM * (kh + 1), :]
                vt_refs[b][kh, HEAD_DIM:_V_ROWS, :] = tail
            if b == 1:
                kf = kb.astype(F32)
                k2 = kf * kf
                hi = k2.astype(BF16)
                n2 = _dot(hi, gmat) + _dot((k2 - hi.astype(F32)).astype(BF16), gmat)
                kn_ref[0] = jnp.broadcast_to(jnp.max(n2, axis=0, keepdims=True), (8, LANES)) * 1.001

    hz = proj(_ZN, _RQ)
    sz_ref[...] = hz * _sigmoid(hz)
    hrq = proj(_RQ, _RK)
    hrk = proj(_RK, _RV)
    for c in range(2):
        rq_ref[:, c * LANES:(c + 1) * LANES] = rope(hrq[:, c * LANES:(c + 1) * LANES])
        rk_ref[:, c * LANES:(c + 1) * LANES] = rope(hrk[:, c * LANES:(c + 1) * LANES]) * (RET_DK ** -0.5)
    rv_ref[...] = proj(_RV, _ZR)
    hzr = proj(_ZR, _GT)
    szr_ref[...] = hzr * _sigmoid(hzr)
    gate_ref[...] = _sigmoid(proj(_GT, _EVEN_COLS))


def _even_in(x, norm_g, w_e, cos, sin, gq, gk, tm):
    t = x.shape[0]
    row = lambda n: pl.BlockSpec((tm, n), lambda i: (i, 0))
    out_shape = (
        jax.ShapeDtypeStruct((t, 1024), BF16),
        jax.ShapeDtypeStruct((t, 256), F32),
        jax.ShapeDtypeStruct((t, 256), F32),
        jax.ShapeDtypeStruct((t, 256), F32),
        jax.ShapeDtypeStruct((t, 128), BF16),
        jax.ShapeDtypeStruct((t, 128), BF16),
        jax.ShapeDtypeStruct((NSA_KV_HEADS, _V_ROWS, t), BF16),
        jax.ShapeDtypeStruct((NSA_KV_HEADS, _V_ROWS, t), BF16),
        jax.ShapeDtypeStruct((t, 128), F32),
        jax.ShapeDtypeStruct((t, 512), F32),
        jax.ShapeDtypeStruct((t, 256), F32),
        jax.ShapeDtypeStruct((t, 256), F32),
        jax.ShapeDtypeStruct((t, 512), F32),
        jax.ShapeDtypeStruct((t, 512), F32),
        jax.ShapeDtypeStruct((t // tm, 8, 128), F32),
    )
    vt_spec = pl.BlockSpec((NSA_KV_HEADS, _V_ROWS, tm), lambda i: (0, 0, i))
    out_specs = (row(1024), row(256), row(256), row(256), row(128), row(128), vt_spec, vt_spec,
                 row(128), row(512), row(256), row(256), row(512), row(512),
                 pl.BlockSpec((1, 8, 128), lambda i: (i, 0, 0)))
    return pl.pallas_call(
        _even_in_kernel,
        grid=(t // tm,),
        in_specs=[row(D_MODEL), _full((1, D_MODEL)), _full((D_MODEL, _EVEN_COLS)),
                  row(128), row(128), _full((1, 128)), _full((3, 128))],
        out_specs=out_specs,
        out_shape=out_shape,
        compiler_params=_cparams(("parallel",)),
        name="even_in",
    )(x, norm_g, w_e, cos, sin, gq, gk)


def _compress_kernel(pt_ref, *refs, pp):
    del pt_ref
    page_refs = refs[:pp]
    w_ref, oe_ref, oo_ref = refs[pp:]
    w = w_ref[...]
    for p in range(pp):
        x = page_refs[p][0]
        s = jnp.sum(x.reshape(PAGE_SIZE // CMP_BLOCK, CMP_BLOCK, 256) * w[None], axis=1)
        oe_ref[0, 2 * p:2 * p + 1, :] = s[0:1]
        oo_ref[0, 2 * p:2 * p + 1, :] = s[1:2]
        oe_ref[0, 2 * p + 1:2 * p + 2, :] = s[2:3]
        oo_ref[0, 2 * p + 1:2 * p + 2, :] = s[3:4]


def _compress(pool, page_table, w2, pp):
    nb, n_pages = page_table.shape

    def page_spec(p):
        return pl.BlockSpec((1, PAGE_SIZE, 256), lambda b, g, pt: (pt[b * n_pages + g * pp + p], 0, 0))

    out_spec = pl.BlockSpec((1, 2 * pp, 256), lambda b, g, pt: (b, g, 0))
    grid_spec = pltpu.PrefetchScalarGridSpec(
        num_scalar_prefetch=1,
        grid=(nb, n_pages // pp),
        in_specs=[page_spec(p) for p in range(pp)] + [pl.BlockSpec((CMP_BLOCK, 256), lambda b, g, pt: (0, 0))],
        out_specs=(out_spec, out_spec),
    )
    shp = jax.ShapeDtypeStruct((nb, 2 * n_pages, 256), F32)
    return pl.pallas_call(
        functools.partial(_compress_kernel, pp=pp),
        grid_spec=grid_spec,
        out_shape=(shp, shp),
        compiler_params=_cparams(("parallel", "parallel")),
        name="compress",
    )(page_table.reshape(-1), *([pool] * pp), w2)


def _cmp_lane(w):
    return (w % 2) * (LANES // 2) + w // 2


def _compress_t_kernel(pt_ref, *refs, pp):
    del pt_ref
    page_refs = refs[:pp]
    w_ref, g_ref, o_ref = refs[pp:]
    w = w_ref[...]
    prods = [(page_refs[p][0] * w).astype(BF16) for p in range(pp)]
    o_ref[0] = _dot(jnp.concatenate(prods, axis=1), g_ref[...])


def _compress_t(pool_t, page_table, w_pos, pp):
    nb, n_pages = page_table.shape
    per_page = PAGE_SIZE // CMP_BLOCK
    assert pp * per_page == LANES
    wf = jnp.tile(jnp.broadcast_to(w_pos.transpose(0, 2, 1)[:, :, None, :],
                                   (2, NSA_KV_HEADS, HEAD_DIM, CMP_BLOCK)).reshape(256, CMP_BLOCK), (1, per_page))
    r = jnp.arange(pp * PAGE_SIZE)
    gm = (_cmp_lane(r // CMP_BLOCK)[:, None] == jnp.arange(LANES)[None, :]).astype(BF16)

    def page_spec(p):
        return pl.BlockSpec((1, 256, PAGE_SIZE), lambda b, g, pt: (pt[b * n_pages + g * pp + p], 0, 0))

    grid_spec = pltpu.PrefetchScalarGridSpec(
        num_scalar_prefetch=1,
        grid=(nb, n_pages // pp),
        in_specs=[page_spec(p) for p in range(pp)]
                 + [pl.BlockSpec((256, PAGE_SIZE), lambda b, g, pt: (0, 0)),
                    pl.BlockSpec((pp * PAGE_SIZE, LANES), lambda b, g, pt: (0, 0))],
        out_specs=pl.BlockSpec((1, 256, LANES), lambda b, g, pt: (b, 0, g)),
    )
    return pl.pallas_call(
        functools.partial(_compress_t_kernel, pp=pp),
        grid_spec=grid_spec,
        out_shape=jax.ShapeDtypeStruct((nb, 256, per_page * n_pages), F32),
        compiler_params=_cparams(("parallel", "parallel")),
        name="compress_sample",
    )(page_table.reshape(-1), *([pool_t] * pp), wf, gm)


def _stack_heads(q, k):
    return jnp.concatenate([q[:, (NSA_GROUP * k + g) * LANES:(NSA_GROUP * k + g + 1) * LANES]
                            for g in range(NSA_GROUP)], axis=0)


def _store_heads(o_ref, og, k, tq):
    for j in range(2):
        blk = jnp.concatenate([og[2 * j], og[2 * j + 1]], axis=0)
        o_ref[:, (2 * k + j) * LANES:(2 * k + j + 1) * LANES] = blk.T


_CMP_VARIANTS = 4


def _cmp_topk_p_kernel(q_ref, kc_ref, vtc_ref, g_ref, o_ref, sel_ref, *, tq, n_half, nq):
    i = pl.program_id(0)
    qpos = i * tq + lax.broadcasted_iota(jnp.int32, (1, tq), 1)
    sb = lax.broadcasted_iota(jnp.int32, (n_half, 1), 0)
    sbf = sb.astype(F32)
    forced = (sb == qpos // SEL_BLOCK) | (sb == 0)
    future = sb * SEL_BLOCK > qpos
    q = q_ref[...]
    gates = g_ref[...]

    def attend(nh):
        r = lax.broadcasted_iota(jnp.int32, (2 * nh, 1), 0)
        blk = jnp.where(r < nh, 2 * r, 2 * (r - nh) + 1)
        mask = ((blk + 1) * CMP_BLOCK - 1) <= qpos
        kc = jnp.concatenate([kc_ref[0:nh, :], kc_ref[n_half:n_half + nh, :]], axis=0)
        scores = []
        for k in range(NSA_KV_HEADS):
            s = _nt_dot(kc, _stack_heads(q, k))
            imp = jnp.zeros((2 * nh, tq), F32)
            ps = []
            for g in range(NSA_GROUP):
                sg = jnp.where(mask, s[:, g * tq:(g + 1) * tq], NEG_INF)
                e = jnp.exp2(sg - jnp.max(sg, axis=0, keepdims=True))
                p = jnp.where(mask, e / jnp.sum(e, axis=0, keepdims=True), 0.0)
                imp = imp + p
                ps.append(p.astype(BF16))
            pt = jnp.concatenate(ps, axis=1)
            vk = slice(HEAD_DIM * k, HEAD_DIM * (k + 1))
            ot = (_dot(vtc_ref[vk, 0:nh], pt[0:nh]) + _dot(vtc_ref[vk, n_half:n_half + nh], pt[nh:2 * nh]))
            og = [ot[:, g * tq:(g + 1) * tq] * gates[NSA_GROUP * k + g:NSA_GROUP * k + g + 1, :]
                  for g in range(NSA_GROUP)]
            _store_heads(o_ref, og, k, tq)
            scores.append(jnp.where(forced[:nh], -jnp.inf, jnp.where(future[:nh], -FORCE, imp[:nh] + imp[nh:])))

        idx = sbf[:nh]

        def pick(rnd, scs):
            out = []
            for sc in scs:
                mx = jnp.max(sc, axis=0, keepdims=True)
                first = jnp.min(jnp.where(sc == mx, idx, 1e9), axis=0, keepdims=True)
                out.append(jnp.where((idx == first) & (rnd < need), -jnp.inf, sc))
            return tuple(out)

        scs = lax.fori_loop(0, n_rounds, pick, tuple(scores))
        for k in range(NSA_KV_HEADS):
            sel_ref[0, k, 0:nh, :] = jnp.where(scs[k] == -jnp.inf, 0.0, NEG_INF)
            if nh < n_half:
                sel_ref[0, k, nh:n_half, :] = jnp.full((n_half - nh, tq), NEG_INF, F32)

    n_forced = 1 + (qpos >= SEL_BLOCK).astype(jnp.int32)
    need = min(N_SELECT, n_half) - n_forced
    n_rounds = min(N_SELECT, n_half) - 1 - (i > 0).astype(jnp.int32)
    per_var = nq // _CMP_VARIANTS
    for c in range(_CMP_VARIANTS):
        pl.when(i // per_var == c)(functools.partial(attend, (c + 1) * n_half // _CMP_VARIANTS))


def _cmp_topk_p(q, kc, vtc, gates0, tq):
    t = q.shape[0]
    n_half = kc.shape[0] // 2
    nq = t // tq
    assert nq % _CMP_VARIANTS == 0 and (n_half // _CMP_VARIANTS) % 8 == 0
    assert n_half * SEL_BLOCK >= t
    return pl.pallas_call(
        functools.partial(_cmp_topk_p_kernel, tq=tq, n_half=n_half, nq=nq),
        grid=(nq,),
        in_specs=[pl.BlockSpec((tq, 1024), lambda i: (i, 0)), _full(kc.shape), _full(vtc.shape),
                  pl.BlockSpec((NSA_HEADS, tq), lambda i: (0, i))],
        out_specs=(pl.BlockSpec((tq, 512), lambda i: (i, 0)),
                   pl.BlockSpec((1, NSA_KV_HEADS, n_half, tq), lambda i: (i, 0, 0, 0))),
        out_shape=(jax.ShapeDtypeStruct((t, 512), F32),
                   jax.ShapeDtypeStruct((nq, NSA_KV_HEADS, n_half, tq), F32)),
        compiler_params=_cparams(("parallel",)),
        name="cmp_topk_prompt",
    )(q, kc, vtc, gates0)


_V_ROWS = HEAD_DIM + 16
_X_ROWS = 16
_SAFE_BOUND = 60.0


def _sel_p_kernel(q_ref, k2_ref, vt_ref, kn_ref, ex_ref, sel_ref, g_ref, oc_ref, ow_ref, sz_ref, o_ref,
                  acc_ref, m_ref, l_ref, s_ref, p_ref, a_ref, qt_ref, os_ref, *, tq, tk, ns):
    i = pl.program_id(0)
    qpos = i * tq + lax.broadcasted_iota(jnp.int32, (1, tq), 1)
    q = q_ref[...]
    qts = [_stack_heads(q, k).astype(F32).T for k in range(NSA_KV_HEADS)]
    nblk = tk // SEL_BLOCK
    assert 2 * nblk == 8 and tk % tq == 0 and ns % 2 == 0 and (k2_ref.shape[0] // tk) % ns == 0
    n_half = sel_ref.shape[2]
    kiota = lax.broadcasted_iota(jnp.int32, (tk, 1), 0)
    n_tiles = (i * tq + tq + tk - 1) // tk
    n_steps = (n_tiles + ns - 1) // ns
    max_tile = k2_ref.shape[0] // tk - 1
    gates = g_ref[...]

    def finish(numer, denom):
        for k in range(NSA_KV_HEADS):
            ot = numer(k) / denom(k)
            og = [ot[:, g * tq:(g + 1) * tq] * gates[NSA_GROUP * k + g:NSA_GROUP * k + g + 1, :]
                  for g in range(NSA_GROUP)]
            _store_heads(os_ref, og, k, tq)
        o_ref[...] = ((oc_ref[...] + os_ref[...] + ow_ref[...]) * sz_ref[...]).astype(BF16)

    kn = jnp.max(kn_ref[...].reshape(-1, LANES), axis=0, keepdims=True)
    bounds = []
    for k in range(NSA_KV_HEADS):
        kmax2 = jnp.max(kn[:, HEAD_DIM * k:HEAD_DIM * (k + 1)], axis=1, keepdims=True)
        qn2 = jnp.sum(qts[k] * qts[k], axis=0, keepdims=True)
        bounds.append(jnp.sqrt(qn2 * kmax2) * 1.02 + 1.0)
    safe = jnp.maximum(jnp.max(bounds[0]), jnp.max(bounds[1])) < _SAFE_BOUND
    lane = lax.broadcasted_iota(jnp.int32, (1, LANES), 1)
    row8 = lax.broadcasted_iota(jnp.int32, (_X_ROWS - 8, 1), 0)
    tails = [jnp.where(row8 == 0, -b, 0.0) for b in bounds]

    def f_scores(j, s1):
        slot = s1 % ns
        kt = jnp.minimum(ns * j + s1, max_tile)
        ktile = k2_ref[pl.ds(pl.multiple_of(kt * tk, tk), tk), :]
        grp = pl.multiple_of(jnp.minimum((j * (ns // 2) + s1 // 2) * 8, n_half - 8), 8)
        for k in range(NSA_KV_HEADS):
            r0 = HEAD_DIM * (1 - k)
            rows8 = jnp.concatenate([sel_ref[0, k, pl.ds(grp, 8), :]] * NSA_GROUP, axis=1)
            qt_ref[slot, k, r0:r0 + _X_ROWS, :] = jnp.concatenate([rows8, tails[k]], axis=0).astype(BF16)
            own = (lane >= HEAD_DIM * k) & (lane < HEAD_DIM * (k + 1))
            lhs = jnp.where(own, ktile, ex_ref[s1 % 2, k])
            s_ref[slot, k] = _dot(lhs, qt_ref[slot, k])

    def f_values(kt, slot):
        kt = jnp.clip(kt, 0, max_tile)
        for k in range(NSA_KV_HEADS):
            acc_ref[k] = acc_ref[k] + _dot(vt_ref[k, :, pl.ds(pl.multiple_of(kt * tk, tk), tk)], p_ref[slot, k])

    def f_softmax(j, slot, masked):
        visible = (ns * j + slot) * tk + kiota <= qpos
        for k in range(NSA_KV_HEADS):
            for g in range(NSA_GROUP):
                cs = slice(g * tq, (g + 1) * tq)
                p = jnp.exp2(s_ref[slot, k, :, cs])
                if masked:
                    p = jnp.where(visible, p, 0.0)
                p_ref[slot, k, :, cs] = p.astype(BF16)

    def f_step(j, masked):
        for s in range(ns):
            f_values(ns * j + s - 1, (s - 1) % ns)
            f_softmax(j, s, masked)
            f_scores(j, s + 1)

    def fast_path():
        for k in range(NSA_KV_HEADS):
            w = qts[k].astype(BF16)
            for slot in range(ns):
                qt_ref[slot, k] = w
        acc_ref[...] = jnp.zeros(acc_ref.shape, F32)
        p_ref[...] = jnp.zeros(p_ref.shape, BF16)
        f_scores(0, 0)

        def body(j, carry):
            f_step(j, False)
            return carry

        lax.fori_loop(0, n_steps - 1, body, 0)
        f_step(n_steps - 1, True)
        f_values(ns * n_steps - 1, ns - 1)
        finish(lambda k: acc_ref[k, 0:HEAD_DIM], lambda k: acc_ref[k, HEAD_DIM:HEAD_DIM + 1])

    def scores(kt, slot):
        kt = jnp.minimum(kt, max_tile)
        ktile = k2_ref[pl.ds(pl.multiple_of(kt * tk, tk), tk), :]
        for k in range(NSA_KV_HEADS):
            s_ref[slot, k] = _dot(ktile, qt_ref[0, k])

    def values(kt, slot):
        kt = jnp.clip(kt, 0, max_tile)
        for k in range(NSA_KV_HEADS):
            vtile = vt_ref[k, 0:HEAD_DIM, pl.ds(pl.multiple_of(kt * tk, tk), tk)]
            acc_ref[k, 0:HEAD_DIM] = acc_ref[k, 0:HEAD_DIM] * a_ref[slot, k] + _dot(vtile, p_ref[slot, k])

    def softmax(j, slot):
        visible = (ns * j + slot) * tk + kiota <= qpos
        grp = pl.multiple_of((j * (ns // 2) + slot // 2) * 8, 8)
        for k in range(NSA_KV_HEADS):
            rows = sel_ref[0, k, pl.ds(grp, 8), :][(slot % 2) * nblk:(slot % 2 + 1) * nblk]
            bias = jnp.concatenate([jnp.broadcast_to(rows[r:r + 1, :], (SEL_BLOCK, tq)) for r in range(nblk)], axis=0)
            bias = jnp.where(visible, bias, NEG_INF)
            for g in range(NSA_GROUP):
                cs = slice(g * tq, (g + 1) * tq)
                sg = s_ref[slot, k, :, cs] + bias
                m_old = m_ref[k, :, cs]
                m_new = jnp.maximum(m_old, jnp.max(sg, axis=0, keepdims=True))
                alpha = jnp.exp2(m_old - m_new)
                p = jnp.exp2(sg - m_new)
                l_ref[k, :, cs] = alpha * l_ref[k, :, cs] + jnp.sum(p, axis=0, keepdims=True)
                m_ref[k, :, cs] = m_new
                a_ref[slot, k, :, cs] = alpha
                p_ref[slot, k, :, cs] = p.astype(BF16)

    def body(j, carry):
        for s in range(ns):
            scores(ns * j + s + 1, (s + 1) % ns)
            values(ns * j + s - 1, (s - 1) % ns)
            softmax(j, s)
        return carry

    def exact_path():
        for k in range(NSA_KV_HEADS):
            qt_ref[0, k] = qts[k].astype(BF16)
        m_ref[...] = jnp.full(m_ref.shape, NEG_INF, F32)
        l_ref[...] = jnp.zeros(l_ref.shape, F32)
        acc_ref[...] = jnp.zeros(acc_ref.shape, F32)
        p_ref[...] = jnp.zeros(p_ref.shape, BF16)
        a_ref[...] = jnp.ones(a_ref.shape, F32)
        scores(0, 0)
        lax.fori_loop(0, n_steps, body, 0)
        values(ns * n_steps - 1, ns - 1)
        finish(lambda k: acc_ref[k, 0:HEAD_DIM], lambda k: l_ref[k])

    lax.cond(safe, fast_path, exact_path)


def _sel_extra_columns(tk):
    row = np.arange(tk)[:, None]
    lane = np.arange(LANES)[None, :]
    ex = np.zeros((2, NSA_KV_HEADS, tk, LANES), np.float32)
    for par in range(2):
        for k in range(NSA_KV_HEADS):
            r0 = HEAD_DIM * (1 - k)
            ex[par, k] = (lane == r0 + par * (tk // SEL_BLOCK) + row // SEL_BLOCK) | (lane == r0 + 8)
    return jnp.asarray(ex, BF16)


def _sel_p(q, k2, vt, kn, sel, gates1, oc, ow, sz, tq, tk, ns):
    t = q.shape[0]
    n_half = sel.shape[2]
    ex = _sel_extra_columns(tk)
    return pl.pallas_call(
        functools.partial(_sel_p_kernel, tq=tq, tk=tk, ns=ns),
        grid=(t // tq,),
        in_specs=[pl.BlockSpec((tq, 1024), lambda i: (i, 0)), _full(k2.shape), _full(vt.shape), _full(kn.shape),
                  _full(ex.shape),
                  pl.BlockSpec((1, NSA_KV_HEADS, n_half, tq), lambda i: (i, 0, 0, 0)),
                  pl.BlockSpec((NSA_HEADS, tq), lambda i: (0, i))] + [pl.BlockSpec((tq, 512), lambda i: (i, 0))] * 3,
        out_specs=pl.BlockSpec((tq, 512), lambda i: (i, 0)),
        out_shape=jax.ShapeDtypeStruct((t, 512), BF16),
        scratch_shapes=[pltpu.VMEM((NSA_KV_HEADS, _V_ROWS, NSA_GROUP * tq), F32),
                        pltpu.VMEM((NSA_KV_HEADS, 1, NSA_GROUP * tq), F32),
                        pltpu.VMEM((NSA_KV_HEADS, 1, NSA_GROUP * tq), F32),
                        pltpu.VMEM((ns, NSA_KV_HEADS, tk, NSA_GROUP * tq), F32),
                        pltpu.VMEM((ns, NSA_KV_HEADS, tk, NSA_GROUP * tq), BF16),
                        pltpu.VMEM((ns, NSA_KV_HEADS, 1, NSA_GROUP * tq), F32),
                        pltpu.VMEM((ns, NSA_KV_HEADS, LANES, NSA_GROUP * tq), BF16),
                        pltpu.VMEM((tq, 512), F32)],
        compiler_params=_cparams(("parallel",)),
        name="sel_prompt",
    )(q, k2, vt, kn, ex, sel, gates1, oc, ow, sz)


def _win_p_kernel(q_ref, k2_ref, vt_ref, g_ref, o_ref, *, tq, nsub):
    nk = WINDOW + tq
    lane = lax.broadcasted_iota(jnp.int32, (1, LANES), 1)
    rowx = lax.broadcasted_iota(jnp.int32, (LANES, 1), 0)
    blocks = []
    for sub in range(nsub):
        i = pl.program_id(0) * nsub + sub
        qpos = i * tq + lax.broadcasted_iota(jnp.int32, (1, tq), 1)
        start = pl.multiple_of(jnp.maximum(i * tq - WINDOW, 0), tq)
        kpos = start + lax.broadcasted_iota(jnp.int32, (nk, 1), 0)
        ok = (kpos <= qpos) & (kpos > qpos - WINDOW)
        bias = jnp.where(ok, 0.0, NEG_INF)
        q = q_ref[sub * tq:(sub + 1) * tq, :]
        ktile = k2_ref[pl.ds(start, nk), :]
        kf = ktile.astype(F32)
        kmax2 = jnp.max(jnp.sum(kf * kf, axis=1, keepdims=True), axis=0, keepdims=True)
        qts = [_stack_heads(q, k).astype(F32).T for k in range(NSA_KV_HEADS)]
        bounds = [jnp.sqrt(jnp.sum(t * t, axis=0, keepdims=True) * kmax2) * 1.02 + 1.0 for t in qts]
        blocks.append(dict(start=start, bias4=jnp.concatenate([bias] * NSA_GROUP, axis=1), q=q, ktile=ktile,
                           qts=qts, bounds=bounds, gates=g_ref[:, sub * tq:(sub + 1) * tq],
                           out=o_ref.at[sub * tq:(sub + 1) * tq, :]))
    safe = functools.reduce(jnp.maximum, [jnp.max(b) for blk in blocks for b in blk["bounds"]]) < _SAFE_BOUND

    def finish(blk, k, ot):
        og = [ot[:, g * tq:(g + 1) * tq] * blk["gates"][NSA_GROUP * k + g:NSA_GROUP * k + g + 1, :]
              for g in range(NSA_GROUP)]
        _store_heads(blk["out"], og, k, tq)

    def fast_path():
        for blk in blocks:
            for k in range(NSA_KV_HEADS):
                r0 = HEAD_DIM * (1 - k)
                own = (lane >= HEAD_DIM * k) & (lane < HEAD_DIM * (k + 1))
                lhs = jnp.where(own, blk["ktile"], jnp.where(lane == r0, 1.0, 0.0).astype(BF16))
                w = jnp.where(rowx == r0, -blk["bounds"][k], blk["qts"][k]).astype(BF16)
                p = jnp.exp2(_dot(lhs, w) + blk["bias4"]).astype(BF16)
                acc = _dot(vt_ref[k, :, pl.ds(blk["start"], nk)], p)
                finish(blk, k, acc[0:HEAD_DIM] / acc[HEAD_DIM:HEAD_DIM + 1])

    def exact_path():
        for blk in blocks:
            for k in range(NSA_KV_HEADS):
                s = _nt_dot(blk["ktile"], _stack_heads(blk["q"], k)) + blk["bias4"]
                p = jnp.exp2(s - jnp.max(s, axis=0, keepdims=True))
                l = jnp.sum(p, axis=0, keepdims=True)
                finish(blk, k, _dot(vt_ref[k, 0:HEAD_DIM, pl.ds(blk["start"], nk)], p.astype(BF16)) / l)

    lax.cond(safe, fast_path, exact_path)


def _win_p(q, k2, vt, gates2, tq, nsub):
    t = q.shape[0]
    tq, tq1 = tq * nsub, tq
    return pl.pallas_call(
        functools.partial(_win_p_kernel, tq=tq1, nsub=nsub),
        grid=(t // tq,),
        in_specs=[pl.BlockSpec((tq, 1024), lambda i: (i, 0)), _full(k2.shape), _full(vt.shape),
                  pl.BlockSpec((NSA_HEADS, tq), lambda i: (0, i))],
        out_specs=pl.BlockSpec((tq, 512), lambda i: (i, 0)),
        out_shape=jax.ShapeDtypeStruct((t, 512), F32),
        compiler_params=_cparams(("parallel",)),
        name="win_prompt",
    )(q, k2, vt, gates2)


def _flag_block(lane):
    r = lane % LANES
    return (lane // LANES) * (LANES // 2) + r, r < LANES // 2


def _cmp_topk_s_kernel(q_ref, ct_ref, g_ref, o_ref, sel_ref, *, tq, past, n_cmp, n_sb, n_cols):
    bs, nrow, _ = q_ref.shape
    nr8 = NSA_KV_HEADS * tq
    qpos = past + lax.broadcasted_iota(jnp.int32, (nrow, 1), 0) % tq
    col = lax.broadcasted_iota(jnp.int32, (1, n_cmp), 1)
    r = col % LANES
    blk = (col // LANES) * LANES + 2 * (r % (LANES // 2)) + r // (LANES // 2)
    mask = ((blk + 1) * CMP_BLOCK - 1) <= qpos
    imps = []
    for j in range(bs):
        qs = q_ref[j]
        ct = ct_ref[j]
        sm = jnp.where(mask, _dot(qs, ct[0:LANES].astype(BF16)), NEG_INF)
        e = jnp.exp2(sm - jnp.max(sm, axis=-1, keepdims=True))
        p = jnp.where(mask, e / jnp.sum(e, axis=-1, keepdims=True), 0.0)
        o_ref[j] = _nt_dot(p.astype(BF16), ct[LANES:2 * LANES].astype(BF16)) * g_ref[j][:, 0:1]
        imp = p[0:nr8]
        for g in range(1, NSA_GROUP):
            imp = imp + p[g * nr8:(g + 1) * nr8]
        pairs = []
        for c in range(n_cmp // LANES):
            ch = imp[:, c * LANES:(c + 1) * LANES]
            pairs.append(ch + pltpu.roll(ch, LANES // 2, 1))
        imps.append(jnp.concatenate(pairs + [jnp.zeros((nr8, n_cols - n_cmp), F32)], axis=1))
    imp = jnp.concatenate(imps, axis=0)
    qpos8 = past + lax.broadcasted_iota(jnp.int32, (bs * nr8, 1), 0) % tq
    sb, used = _flag_block(lax.broadcasted_iota(jnp.int32, (1, n_cols), 1))
    sbf = sb.astype(F32)
    score = jnp.where((sb == qpos8 // SEL_BLOCK) | (sb == 0), FORCE, imp)
    score = jnp.where(sb * SEL_BLOCK > qpos8, -FORCE, score)
    valid = used & (sb < n_sb)
    score = jnp.where(valid, score, -jnp.inf)

    def pick(_, sc):
        mx = jnp.max(sc, axis=-1, keepdims=True)
        first = jnp.min(jnp.where(sc == mx, sbf, 1e9), axis=-1, keepdims=True)
        return jnp.where(sbf == first, -jnp.inf, sc)

    sc = lax.fori_loop(0, min(N_SELECT, n_sb), pick, score)
    chosen = jnp.where(valid & (sc == -jnp.inf), 1.0, 0.0)
    for j in range(bs):
        sel_ref[j] = chosen[j * nr8:(j + 1) * nr8]


def _cmp_topk_s(qs, cmp_t, gates, tq, past, bs):
    nb, nrow, _ = qs.shape
    n_cmp = cmp_t.shape[2]
    n_sb = -(-(past + tq) // SEL_BLOCK)
    n_cols = n_cmp + LANES
    assert n_sb <= (n_cols // LANES) * (LANES // 2) and nb % bs == 0
    b3 = lambda a: pl.BlockSpec((bs,) + a.shape[1:], lambda b: (b, 0, 0))
    return pl.pallas_call(
        functools.partial(_cmp_topk_s_kernel, tq=tq, past=past, n_cmp=n_cmp, n_sb=n_sb, n_cols=n_cols),
        grid=(nb // bs,),
        in_specs=[b3(qs), b3(cmp_t), b3(gates)],
        out_specs=(pl.BlockSpec((bs, nrow, 128), lambda b: (b, 0, 0)),
                   pl.BlockSpec((bs, NSA_KV_HEADS * tq, n_cols), lambda b: (b, 0, 0))),
        out_shape=(jax.ShapeDtypeStruct((nb, nrow, 128), F32),
                   jax.ShapeDtypeStruct((nb, NSA_KV_HEADS * tq, n_cols), F32)),
        compiler_params=_cparams(("parallel",)),
        name="cmp_topk_sample",
    )(qs, cmp_t, gates)


def _selwin_s_kernel(pt_ref, q_ref, sel_ref, g_ref, *refs, pp, tq, past):
    del pt_ref
    page_refs = refs[:pp]
    newsel_ref, winbuf_ref, newwin_ref, em_ref, osel_ref, owin_ref, m_ref, l_ref, acc_ref = refs[pp:]
    g = pl.program_id(1)
    ng = pl.num_programs(1)
    qs = q_ref[0]
    nrow = qs.shape[0]
    flags = sel_ref[0]
    n_cols = flags.shape[1]
    qpos = past + lax.broadcasted_iota(jnp.int32, (nrow, 1), 0) % tq

    @pl.when(g == 0)
    def _():
        m_ref[...] = jnp.full(m_ref.shape, NEG_INF, F32)
        l_ref[...] = jnp.zeros(l_ref.shape, F32)
        acc_ref[...] = jnp.zeros(acc_ref.shape, F32)

    def update(s, ok, vts):
        sm = s if ok is None else jnp.where(ok, s, NEG_INF)
        m_old = m_ref[...]
        m_new = jnp.maximum(m_old, jnp.max(sm, axis=-1, keepdims=True))
        alpha = jnp.exp2(m_old - m_new)
        p = jnp.exp2(sm - m_new)
        if ok is not None:
            p = jnp.where(ok, p, 0.0)
        l_ref[...] = alpha * l_ref[...] + jnp.sum(p, axis=-1, keepdims=True)
        acc_ref[...] = alpha * acc_ref[...] + _nt_dot(p.astype(BF16), vts)
        m_ref[...] = m_new

    def expand(first_block, n):
        ncol = lax.broadcasted_iota(jnp.int32, (1, n), 1)
        sbl, used = _flag_block(lax.broadcasted_iota(jnp.int32, (n_cols, 1), 0))
        emat = ((sbl == first_block + ncol // SEL_BLOCK) & used).astype(BF16)
        mf = _dot(flags.astype(BF16), emat)
        return jnp.concatenate([mf] * NSA_GROUP, axis=0) > 0.5

    n = pp * PAGE_SIZE
    pages = [page_refs[p][0] for p in range(pp)]
    kt_all = jnp.concatenate([pg[0:LANES].astype(BF16) for pg in pages], axis=1)
    vt_all = jnp.concatenate([pg[LANES:2 * LANES].astype(BF16) for pg in pages], axis=1)
    nsb = n // SEL_BLOCK
    spc = (LANES // 2) // nsb
    chunk = sel_ref[0, :, pl.ds(pl.multiple_of((g // spc) * LANES, LANES), LANES)]
    cb = jnp.where(chunk > 0.5, 0.0, NEG_INF)
    if spc == 2:
        cb = jnp.where(g % 2 == 0, cb, pltpu.roll(cb, LANES - nsb, 1))
    cb = jnp.where(lax.broadcasted_iota(jnp.int32, (1, LANES), 1) < nsb, cb, 0.0)
    lhs = jnp.concatenate([qs, jnp.concatenate([cb] * NSA_GROUP, axis=0).astype(BF16)], axis=1)
    update(_dot(lhs, jnp.concatenate([kt_all, em_ref[...]], axis=0)), None, vt_all)

    @pl.when(g == ng - 1)
    def _():
        new = newsel_ref[0]
        kpos_n = past + lax.broadcasted_iota(jnp.int32, (1, PAGE_SIZE), 1)
        update(_dot(qs, new[0:LANES].astype(BF16)),
               expand(past // SEL_BLOCK, PAGE_SIZE) & (kpos_n <= qpos), new[LANES:2 * LANES].astype(BF16))
        gates = g_ref[0]
        osel_ref[0] = acc_ref[...] / l_ref[...] * gates[:, 1:2]

        wb = winbuf_ref[0]
        nw = newwin_ref[0]
        n_buf = wb.shape[1]
        kw = jnp.concatenate([wb[0:LANES], nw[0:LANES]], axis=1).astype(BF16)
        vw = jnp.concatenate([wb[LANES:2 * LANES], nw[LANES:2 * LANES]], axis=1).astype(BF16)
        wpos = past - n_buf + lax.broadcasted_iota(jnp.int32, (1, n_buf + PAGE_SIZE), 1)
        okw = (wpos <= qpos) & (wpos > qpos - WINDOW) & (wpos >= 0)
        sw = jnp.where(okw, _dot(qs, kw), NEG_INF)
        e = jnp.where(okw, jnp.exp2(sw - jnp.max(sw, axis=-1, keepdims=True)), 0.0)
        owin_ref[0] = _nt_dot(e.astype(BF16), vw) / jnp.sum(e, axis=-1, keepdims=True) * gates[:, 2:3]


def _selwin_s(qs, sel, gates, pool, page_table, newsel, winbuf, newwin, tq, past, pp):
    nb, nrow, _ = qs.shape
    n_pages = page_table.shape[1]
    bmap = lambda b, g, pt: (b, 0, 0)
    b3 = lambda a: pl.BlockSpec((1,) + a.shape[1:], bmap)

    def page_spec(p):
        return pl.BlockSpec((1, 256, PAGE_SIZE), lambda b, g, pt: (pt[b * n_pages + g * pp + p], 0, 0))

    n = pp * PAGE_SIZE
    assert (n // SEL_BLOCK) in (LANES // 4, LANES // 2)
    em = jnp.asarray(np.arange(LANES)[:, None] == np.arange(n)[None, :] // SEL_BLOCK, BF16)
    grid_spec = pltpu.PrefetchScalarGridSpec(
        num_scalar_prefetch=1,
        grid=(nb, n_pages // pp),
        in_specs=[b3(qs), b3(sel), b3(gates)] + [page_spec(p) for p in range(pp)]
                 + [b3(newsel), b3(winbuf), b3(newwin), pl.BlockSpec(em.shape, lambda b, g, pt: (0, 0))],
        out_specs=(pl.BlockSpec((1, nrow, 128), bmap), pl.BlockSpec((1, nrow, 128), bmap)),
        scratch_shapes=[pltpu.VMEM((nrow, 1), F32), pltpu.VMEM((nrow, 1), F32), pltpu.VMEM((nrow, 128), F32)],
    )
    shp = jax.ShapeDtypeStruct((nb, nrow, 128), F32)
    return pl.pallas_call(
        functools.partial(_selwin_s_kernel, pp=pp, tq=tq, past=past),
        grid_spec=grid_spec,
        out_shape=(shp, shp),
        compiler_params=_cparams(("parallel", "arbitrary")),
        name="selwin_sample",
    )(page_table.reshape(-1), qs, sel, gates, *([pool] * pp), newsel, winbuf, newwin, em)


def _ret_kernel(rq_ref, rk_ref, rv_ref, sz_ref, s0_ref, dec_ref, qd_ref, kd_ref, gc_ref,
                o_ref, sout_ref, s_ref):
    c = pl.program_id(1)

    @pl.when(c == 0)
    def _():
        s_ref[...] = s0_ref[0]

    lane = lax.broadcasted_iota(jnp.int32, (1, LANES), 1)
    for pr in range(RET_HEADS // 2):
        cs = slice(pr * LANES, (pr + 1) * LANES)
        qc = rq_ref[:, cs]
        kc = rk_ref[:, cs]
        kb = kc.astype(BF16)
        state = s_ref[pr]
        sb = state.astype(BF16)
        new_state = gc_ref[pr] * state
        for hh in range(2):
            h = 2 * pr + hh
            hm = (lane >= RET_DK * hh) & (lane < RET_DK * (hh + 1))
            qm = jnp.where(hm, qc, 0.0)
            a = _nt_dot(qm.astype(BF16), kb) * dec_ref[h]
            v = rv_ref[:, h * RET_DV:(h + 1) * RET_DV]
            vb = v.astype(BF16)
            o = _dot(a.astype(BF16), vb) + _dot((qm * qd_ref[:, cs]).astype(BF16), sb)
            km = jnp.where(hm, kc * kd_ref[:, cs], 0.0)
            new_state = new_state + _dot(km.T.astype(BF16), vb)
            mu = jnp.mean(o, axis=-1, keepdims=True)
            var = jnp.mean(jnp.square(o - mu), axis=-1, keepdims=True)
            o_ref[:, h * RET_DV:(h + 1) * RET_DV] = ((o - mu) * lax.rsqrt(var + EPS)
                                                    * sz_ref[:, h * RET_DV:(h + 1) * RET_DV]).astype(BF16)
        s_ref[pr] = new_state
        sout_ref[0, pr] = new_state


def _retention(rq, rk, rv, szr, s0, c_real, nb, nc, cc):
    log_g = jnp.log1p(-jnp.exp2(-5.0 - jnp.arange(RET_HEADS, dtype=F32)))
    i = jnp.arange(cc, dtype=F32)
    rel = i[:, None] - i[None, :]
    dec = jnp.where(rel >= 0, jnp.exp(log_g[:, None, None] * jnp.maximum(rel, 0.0)), 0.0)
    qd = jnp.repeat(jnp.exp(log_g[None, :] * (i[:, None] + 1.0)), RET_DK, axis=1)
    kd = jnp.repeat(jnp.exp(log_g[None, :] * (c_real - 1.0 - i[:, None])), RET_DK, axis=1)
    gc = jnp.broadcast_to(jnp.repeat(jnp.exp(log_g * c_real), RET_DK).reshape(2, 128, 1), (2, 128, 128))
    row = lambda n: pl.BlockSpec((cc, n), lambda b, c: (b * nc + c, 0))
    st = pl.BlockSpec((1, 2, 128, 128), lambda b, c: (b, 0, 0, 0))
    t = rq.shape[0]
    return pl.pallas_call(
        _ret_kernel,
        grid=(nb, nc),
        in_specs=[row(256), row(256), row(512), row(512), st, _full((RET_HEADS, cc, cc)),
                  _full((cc, 256)), _full((cc, 256)), _full((2, 128, 128))],
        out_specs=(row(512), st),
        out_shape=(jax.ShapeDtypeStruct((t, 512), BF16), jax.ShapeDtypeStruct((nb, 2, 128, 128), F32)),
        scratch_shapes=[pltpu.VMEM((2, 128, 128), F32)],
        compiler_params=_cparams(("parallel", "arbitrary")),
        name="retention",
    )(rq, rk, rv, szr, s0, dec, qd, kd, gc)


def _even_out_kernel(x_ref, *refs):
    mr_ref, w_ref, y_ref = refs[-3:]
    if len(refs) == 4:
        mixed = refs[0][...]
    else:
        oc_ref, os_ref, ow_ref, sz_ref = refs[:4]
        mixed = ((oc_ref[...] + os_ref[...] + ow_ref[...]) * sz_ref[...]).astype(BF16)
    y_ref[...] = (x_ref[...] + _dot(mixed, w_ref[0:512, :]) + _dot(mr_ref[...], w_ref[512:1024, :]))


def _even_out(x, nsa_parts, mret, w_out, tm):
    t = x.shape[0]
    row = lambda n: pl.BlockSpec((tm, n), lambda i: (i, 0))
    return pl.pallas_call(
        _even_out_kernel,
        grid=(t // tm,),
        in_specs=[row(D_MODEL)] + [row(512)] * (len(nsa_parts) + 1) + [_full((1024, D_MODEL))],
        out_specs=row(D_MODEL),
        out_shape=jax.ShapeDtypeStruct((t, D_MODEL), F32),
        compiler_params=_cparams(("parallel",)),
        name="even_out",
    )(x, *nsa_parts, mret, w_out)


_HALO = 32
_CONV_ROWS = 16


def _rms(x, g):
    return (x * lax.rsqrt(jnp.mean(x * x, axis=-1, keepdims=True) + EPS) * g).astype(BF16)


def _glu(xn, w_ref):
    return _dot(xn, w_ref[:, 0:D_MODEL]) * _sigmoid(_dot(xn, w_ref[:, D_MODEL:2 * D_MODEL]))


def _ln_gate_out(x, c, z, lg_ref, lb_ref, wout_ref):
    mu = jnp.mean(c, axis=-1, keepdims=True)
    var = jnp.mean(jnp.square(c - mu), axis=-1, keepdims=True)
    yn = (c - mu) * lax.rsqrt(var + EPS) * lg_ref[...] + lb_ref[...]
    y = (yn * _sigmoid(yn)) * (z * _sigmoid(z))
    return x + _dot(y.astype(BF16), wout_ref[...])


def _odd_p_kernel(x_ref, xh_ref, g_ref, win_ref, cw_ref, cb_ref, lg_ref, lb_ref, wout_ref,
                  y_ref, ust_ref, u_scr, c_scr, *, tm):
    i = pl.program_id(0)
    x = x_ref[...]
    g = g_ref[...]
    xn = _rms(x, g)
    u = _glu(xn, win_ref)
    z = _dot(xn, win_ref[:, 2 * D_MODEL:3 * D_MODEL])
    uh = _glu(_rms(xh_ref[...], g), win_ref)
    u_scr[0, 0:_HALO, :] = jnp.where(i > 0, uh, 0.0)
    u_scr[0, _HALO:_HALO + tm, :] = u
    u_scr[0, _HALO + tm:_HALO + tm + 8, :] = jnp.zeros((8, D_MODEL), F32)
    ust_ref[...] = u[tm - _HALO:tm, :]
    for s in range(1, 8):
        u_scr[s, 0:_HALO + tm, :] = u_scr[0, s:s + _HALO + tm, :]
    off = _HALO - (CONV_WIDTH - 1)
    cb = cb_ref[...]
    nv = _CONV_ROWS // 8
    for rc in range(tm // _CONV_ROWS):
        r0 = rc * _CONV_ROWS
        acc = jnp.broadcast_to(cb, (_CONV_ROWS, D_MODEL)).reshape(nv, 8, D_MODEL)
        for j in range(CONV_WIDTH):
            a, s = divmod(j + off, 8)
            rows = u_scr[s, r0 + 8 * a:r0 + 8 * a + _CONV_ROWS, :]
            acc = acc + cw_ref[j][None] * rows.reshape(nv, 8, D_MODEL)
        c_scr[r0:r0 + _CONV_ROWS, :] = acc.reshape(_CONV_ROWS, D_MODEL)
    y_ref[...] = _ln_gate_out(x, c_scr[...], z, lg_ref, lb_ref, wout_ref)


def _odd_p(x, norm_g, w_in, cw, cb, lg, lb, w_out, tm):
    t = x.shape[0]
    row = pl.BlockSpec((tm, D_MODEL), lambda i: (i, 0))
    halo = pl.BlockSpec((_HALO, D_MODEL), lambda i: (jnp.maximum(i * (tm // _HALO) - 1, 0), 0))
    vec = _full((1, D_MODEL))
    return pl.pallas_call(
        functools.partial(_odd_p_kernel, tm=tm),
        grid=(t // tm,),
        in_specs=[row, halo, vec, _full((D_MODEL, 3 * D_MODEL)), _full((CONV_WIDTH, 8, D_MODEL)), vec, vec, vec,
                  _full((D_MODEL, D_MODEL))],
        out_specs=(row, _full((_HALO, D_MODEL))),
        out_shape=(jax.ShapeDtypeStruct((t, D_MODEL), F32), jax.ShapeDtypeStruct((_HALO, D_MODEL), F32)),
        scratch_shapes=[pltpu.VMEM((8, _HALO + tm + 8, D_MODEL), F32), pltpu.VMEM((tm, D_MODEL), F32)],
        compiler_params=_cparams(("arbitrary",)),
        name="odd_prompt",
    )(x, x, norm_g, w_in, cw, cb, lg, lb, w_out)


def _odd_s_kernel(x_ref, st_ref, g_ref, win_ref, cw_ref, cb_ref, lg_ref, lb_ref, wout_ref,
                  y_ref, u_ref, *, nb, tq):
    x = x_ref[...]
    xn = _rms(x, g_ref[...])
    u = _glu(xn, win_ref)
    z = _dot(xn, win_ref[:, 2 * D_MODEL:3 * D_MODEL])
    u_ref[...] = u
    n_state = CONV_WIDTH - 1
    cb = cb_ref[...]
    cs = []
    for t in range(tq):
        acc = jnp.broadcast_to(cb, (nb, D_MODEL))
        for j in range(CONV_WIDTH):
            e = t + j
            slab = st_ref[e * nb:(e + 1) * nb, :] if e < n_state else u[(e - n_state) * nb:(e - n_state + 1) * nb, :]
            acc = acc + cw_ref[j:j + 1, :] * slab
        cs.append(acc)
    y_ref[...] = _ln_gate_out(x, jnp.concatenate(cs, axis=0), z, lg_ref, lb_ref, wout_ref)


def _odd_s(x_tm, state_tm, norm_g, w_in, cw, cb, lg, lb, w_out, nb, tq):
    n = x_tm.shape[0]
    return pl.pallas_call(
        functools.partial(_odd_s_kernel, nb=nb, tq=tq),
        out_shape=(jax.ShapeDtypeStruct((n, D_MODEL), F32), jax.ShapeDtypeStruct((n, D_MODEL), F32)),
        compiler_params=pltpu.CompilerParams(vmem_limit_bytes=VMEM_LIMIT),
        name="odd_sample",
    )(x_tm, state_tm, norm_g, w_in, cw, cb, lg, lb, w_out)


def _rope_tables(pos):
    half = HEAD_DIM // 2
    freq = ROPE_THETA ** (-jnp.arange(half, dtype=F32) / half)
    ang = pos.astype(F32)[:, None] * freq[None, :]
    cos = jnp.cos(ang)
    sin = jnp.sin(ang)
    return jnp.tile(cos, (1, 4)), jnp.tile(jnp.concatenate([-sin, sin], axis=1), (1, 2))


def _even_weights(w_in):
    pq, kc, ks, kw, pg, zn, rq, rk, rv, zr = jnp.split(
        w_in, np.cumsum([512, 256, 256, 256, 24, 512, 256, 256, 512])[:].tolist(), axis=-1)
    zero = jnp.zeros((D_MODEL, HEAD_DIM), w_in.dtype)
    qcols = []
    for h in range(NSA_HEADS):
        wh = pq[:, h * HEAD_DIM:(h + 1) * HEAD_DIM]
        qcols += [wh, zero] if h < NSA_GROUP else [zero, wh]
    gpad = jnp.pad(pg, ((0, 0), (0, LANES - 3 * NSA_HEADS)))
    return jnp.concatenate(qcols + [kc, ks, kw, zn, rq, rk, rv, zr, gpad], axis=-1).astype(BF16)


def _pick_kv_half(o, nb, tq):
    o6 = o.reshape(nb, NSA_GROUP, NSA_KV_HEADS, tq, NSA_KV_HEADS, HEAD_DIM)
    o5 = jnp.stack([o6[:, :, 0, :, 0], o6[:, :, 1, :, 1]], axis=2)
    return o5.transpose(0, 3, 2, 1, 4).reshape(nb * tq, NSA_HEADS * HEAD_DIM)


def kernel(x_prompt, x_sample, cache_cmp_kv, cache_sel_kv, cache_win_kv, state_ret, state_conv, page_table,
           norm_even, w_in_even, q_norm_g, k_norm_g, w_cmp_pos, w_out_even,
           norm_odd, w_in_odd, conv_w, conv_b, ln_g, ln_b, w_out_odd):
    _, seq, _ = x_prompt.shape
    nb, tq, _ = x_sample.shape
    n_pages = page_table.shape[1]
    past = n_pages * PAGE_SIZE
    n_buf = cache_win_kv.shape[2]
    tm = 256
    tq_p = 128

    w_e = _even_weights(w_in_even[0])
    w_oe = w_out_even[0].astype(BF16)
    w_io = w_in_odd[0].astype(BF16)
    w_oo = w_out_odd[0].astype(BF16)
    gq = jnp.tile(q_norm_g[0], 2)[None, :]
    gk = jnp.tile(k_norm_g[0], (1, 2))
    w2 = jnp.repeat(w_cmp_pos[0].transpose(1, 0, 2).reshape(CMP_BLOCK, 4), HEAD_DIM, axis=1)
    ne = norm_even[0][None, :]
    no = norm_odd[0][None, :]

    xp = x_prompt[0]
    cos_p, sin_p = _rope_tables(jnp.arange(seq))
    (q_p, kvc_p, kvs_p, kvw_p, k2s_p, k2w_p, vts_p, vtw_p, gate_p, sz_p,
     rq_p, rk_p, rv_p, szr_p, kn_p) = _even_in(xp, ne, w_e, cos_p, sin_p, gq, gk, tm)
    ident = jnp.arange(seq // PAGE_SIZE, dtype=jnp.int32)[None, :]
    ce, co = _compress(kvc_p.reshape(seq // PAGE_SIZE, PAGE_SIZE, 256), ident, w2, 16)
    cmp_p = jnp.concatenate([ce[0], co[0]], axis=0)
    kc_p = cmp_p[:, :LANES].astype(BF16)
    vtc_p = cmp_p[:, LANES:].T.astype(BF16)
    gates_t = gate_p[:, :3 * NSA_HEADS].reshape(seq, NSA_HEADS, 3).transpose(2, 1, 0)
    oc_p, sel_p = _cmp_topk_p(q_p, kc_p, vtc_p, gates_t[0], tq_p)
    ow_p = _win_p(q_p, k2w_p, vtw_p, gates_t[2], tq_p, 4)
    mix_p = _sel_p(q_p, k2s_p, vts_p, kn_p, sel_p, gates_t[1], oc_p, ow_p, sz_p, tq_p, 256, 8)
    s0_p = jnp.zeros((1, 2, 128, 128), F32)
    ret_c = 4 * RET_CHUNK
    mret_p, s_p = _retention(rq_p, rk_p, rv_p, szr_p, s0_p, ret_c, 1, seq // ret_c, ret_c)
    y1_p = _even_out(xp, (mix_p,), mret_p, w_oe, tm)

    xs = x_sample.reshape(nb * tq, D_MODEL)
    cos_s, sin_s = _rope_tables(jnp.tile(past + jnp.arange(tq), nb))
    (q_s, kvc_s, kvs_s, kvw_s, _, _, _, _, gate_s, sz_s,
     rq_s, rk_s, rv_s, szr_s, _) = _even_in(xs, ne, w_e, cos_s, sin_s, gq, gk, nb * tq)
    rows_t = lambda c: c.transpose(0, 2, 3, 4, 1).reshape(c.shape[0], 256, c.shape[1])
    cmp_s = _compress_t(rows_t(cache_cmp_kv[0]), page_table, w_cmp_pos[0], LANES * CMP_BLOCK // PAGE_SIZE)
    qs_s = (q_s.reshape(nb, tq, NSA_KV_HEADS, NSA_GROUP, LANES).transpose(0, 3, 2, 1, 4)
            .reshape(nb, NSA_HEADS * tq, LANES))
    g_s = (gate_s[:, :3 * NSA_HEADS].reshape(nb, tq, NSA_KV_HEADS, NSA_GROUP, 3).transpose(0, 3, 2, 1, 4)
           .reshape(nb, NSA_HEADS * tq, 3))
    oc_s, sel_s = _cmp_topk_s(qs_s, cmp_s, g_s, tq, past, 4 if nb % 4 == 0 else 1)
    new_t = lambda r: jnp.pad(r.reshape(nb, tq, 256), ((0, 0), (0, PAGE_SIZE - tq), (0, 0))).transpose(0, 2, 1)
    os_s, ow_s = _selwin_s(qs_s, sel_s, g_s, rows_t(cache_sel_kv[0]), page_table,
                           new_t(kvs_s), rows_t(cache_win_kv[0]), new_t(kvw_s), tq, past, 32)
    pad_c = lambda r: jnp.pad(r.reshape(nb, tq, -1), ((0, 0), (0, RET_CHUNK - tq), (0, 0))).reshape(nb * RET_CHUNK, -1)
    mret_s, s_s = _retention(pad_c(rq_s), pad_c(rk_s), pad_c(rv_s), pad_c(szr_s),
                             state_ret[0].reshape(nb, 2, 128, 128), float(tq), nb, 1, RET_CHUNK)
    mret_s = mret_s.reshape(nb, RET_CHUNK, 512)[:, :tq].reshape(nb * tq, 512)
    y1_s = _even_out(xs, (_pick_kv_half(oc_s, nb, tq), _pick_kv_half(os_s, nb, tq), _pick_kv_half(ow_s, nb, tq),
                          sz_s), mret_s, w_oe, nb * tq)

    cb = conv_b[0][None, :]
    lg = ln_g[0][None, :]
    lb = ln_b[0][None, :]
    cw8 = jnp.broadcast_to(conv_w[0][:, None, :], (CONV_WIDTH, 8, D_MODEL))
    y2_p, ust_p = _odd_p(y1_p, no, w_io, cw8, cb, lg, lb, w_oo, tm)
    x_tm = y1_s.reshape(nb, tq, D_MODEL).transpose(1, 0, 2).reshape(tq * nb, D_MODEL)
    st_tm = state_conv[0].transpose(1, 0, 2).reshape((CONV_WIDTH - 1) * nb, D_MODEL)
    y2_tm, u_tm = _odd_s(x_tm, st_tm, no, w_io, conv_w[0], cb, lg, lb, w_oo, nb, tq)
    y2_s = y2_tm.reshape(tq, nb, D_MODEL).transpose(1, 0, 2)
    u_s = u_tm.reshape(tq, nb, D_MODEL).transpose(1, 0, 2)

    rows6 = lambda r, b, t: r.reshape(1, b, t, 2, NSA_KV_HEADS, HEAD_DIM)
    n_win = min(WINDOW, seq)
    win_s = jnp.concatenate([cache_win_kv[0].reshape(nb, n_buf, 256), kvw_s.reshape(nb, tq, 256)], axis=1)[:, -n_buf:]
    conv_s = jnp.concatenate([state_conv[0], u_s], axis=1)[:, -(CONV_WIDTH - 1):]
    return (y2_p[None], y2_s,
            rows6(kvc_p, 1, seq), rows6(kvc_s, nb, tq),
            rows6(kvs_p, 1, seq), rows6(kvs_s, nb, tq),
            rows6(kvw_p[seq - n_win:], 1, n_win), rows6(win_s, nb, n_buf),
            s_p.reshape(1, 1, RET_HEADS, RET_DK, RET_DV), s_s.reshape(1, nb, RET_HEADS, RET_DK, RET_DV),
            ust_p[_HALO - (CONV_WIDTH - 1):][None, None], conv_s[None])
```

```python
import functools

import numpy as np
import jax
import jax.numpy as jnp
from jax import lax
from jax.experimental import pallas as pl
from jax.experimental.pallas import tpu as pltpu

F32 = jnp.float32
BF16 = jnp.bfloat16

D_MODEL = 1024
PAGE_SIZE = 128
HEAD_DIM = 64
NSA_HEADS = 8
NSA_KV_HEADS = 2
NSA_GROUP = 4
CMP_BLOCK = 32
SEL_BLOCK = 64
N_SELECT = 16
WINDOW = 512
RET_HEADS = 4
RET_DK = 64
RET_DV = 128
RET_CHUNK = 128
CONV_WIDTH = 31
ROPE_THETA = 10000.0
NEG_INF = -1e30
FORCE = 1e9
EPS = 1e-6

LOG2E = 1.4426950408889634
Q_SCALE = HEAD_DIM ** -0.5 * LOG2E

LANES = 128
VMEM_LIMIT = 56 * 1024 * 1024

_QP = 0
_KV = 1024
_ZN = 1792
_RQ = 2304
_RK = 2560
_RV = 2816
_ZR = 3328
_GT = 3840
_EVEN_COLS = 3968


def _cparams(sem):
    return pltpu.CompilerParams(dimension_semantics=sem, vmem_limit_bytes=VMEM_LIMIT)


def _sigmoid(x):
    return 1.0 / (1.0 + jnp.exp(-x))


def _nt_dot(a, b):
    return lax.dot_general(a, b, (((1,), (1,)), ((), ())), preferred_element_type=F32)


def _dot(a, b):
    return jnp.dot(a, b, preferred_element_type=F32)


def _full(shape):
    n = len(shape)
    return pl.BlockSpec(shape, lambda *_: (0,) * n)


def _even_in_kernel(x_ref, g_ref, w_ref, cos_ref, sin_ref, gq_ref, gk_ref,
                    q_ref, kvc_ref, kvs_ref, kvw_ref, k2s_ref, k2w_ref, vts_ref, vtw_ref,
                    gate_ref, sz_ref, rq_ref, rk_ref, rv_ref, szr_ref, kn_ref):
    x = x_ref[...]
    ms = jnp.mean(x * x, axis=-1, keepdims=True)
    xn = (x * lax.rsqrt(ms + EPS) * g_ref[...]).astype(BF16)
    cos = cos_ref[...]
    sin = sin_ref[...]
    lane = lax.broadcasted_iota(jnp.int32, (1, LANES), 1)
    lo_half = (lane & 32) == 0
    gr = lax.broadcasted_iota(jnp.int32, (LANES, LANES), 0) // HEAD_DIM
    gc = lax.broadcasted_iota(jnp.int32, (LANES, LANES), 1) // HEAD_DIM
    gmat = (gr == gc).astype(BF16)

    def proj(a, b):
        return _dot(xn, w_ref[:, a:b])

    def head_rms(h, gain):
        h2 = h * h
        hi = h2.astype(BF16)
        lo = (h2 - hi.astype(F32)).astype(BF16)
        ss = _dot(hi, gmat) + _dot(lo, gmat)
        return h * lax.rsqrt(ss * (1.0 / HEAD_DIM) + EPS) * gain

    def rope(h):
        sw = jnp.where(lo_half, pltpu.roll(h, LANES - 32, 1), pltpu.roll(h, 32, 1))
        return h * cos + sw * sin

    hq = proj(_QP, _KV)
    gq = gq_ref[...]
    for c in range(NSA_HEADS):
        hc = hq[:, c * LANES:(c + 1) * LANES]
        q_ref[:, c * LANES:(c + 1) * LANES] = (rope(head_rms(hc, gq)) * Q_SCALE).astype(BF16)

    hk = proj(_KV, _ZN)
    kv_refs = (kvc_ref, kvs_ref, kvw_ref)
    k2_refs = (None, k2s_ref, k2w_ref)
    vt_refs = (None, vts_ref, vtw_ref)
    for b in range(3):
        k = rope(head_rms(hk[:, 2 * b * LANES:(2 * b + 1) * LANES], gk_ref[b:b + 1, :]))
        v = hk[:, (2 * b + 1) * LANES:(2 * b + 2) * LANES]
        kv_refs[b][:, 0:LANES] = k
        kv_refs[b][:, LANES:2 * LANES] = v
        if k2_refs[b] is not None:
            kb = k.astype(BF16)
            k2_refs[b][...] = kb
            vt = v.T.astype(BF16)
            tail = jnp.where(lax.broadcasted_iota(jnp.int32, (_V_ROWS - HEAD_DIM, 1), 0) == 0, 1.0, 0.0)
            tail = jnp.broadcast_to(tail, (_V_ROWS - HEAD_DIM, vt.shape[1])).astype(BF16)
            for kh in range(NSA_KV_HEADS):
                vt_refs[b][kh, 0:HEAD_DIM, :] = vt[HEAD_DIM * kh:HEAD_DIM * (kh + 1), :]
                vt_refs[b][kh, HEAD_DIM:_V_ROWS, :] = tail
            if b == 1:
                kf = kb.astype(F32)
                k2 = kf * kf
                hi = k2.astype(BF16)
                n2 = _dot(hi, gmat) + _dot((k2 - hi.astype(F32)).astype(BF16), gmat)
                kn_ref[0] = jnp.broadcast_to(jnp.max(n2, axis=0, keepdims=True), (8, LANES)) * 1.001

    hz = proj(_ZN, _RQ)
    sz_ref[...] = hz * _sigmoid(hz)
    hrq = proj(_RQ, _RK)
    hrk = proj(_RK, _RV)
    for c in range(2):
        rq_ref[:, c * LANES:(c + 1) * LANES] = rope(hrq[:, c * LANES:(c + 1) * LANES])
        rk_ref[:, c * LANES:(c + 1) * LANES] = rope(hrk[:, c * LANES:(c + 1) * LANES]) * (RET_DK ** -0.5)
    rv_ref[...] = proj(_RV, _ZR)
    hzr = proj(_ZR, _GT)
    szr_ref[...] = hzr * _sigmoid(hzr)
    gate_ref[...] = _sigmoid(proj(_GT, _EVEN_COLS))


def _even_in(x, norm_g, w_e, cos, sin, gq, gk, tm):
    t = x.shape[0]
    row = lambda n: pl.BlockSpec((tm, n), lambda i: (i, 0))
    out_shape = (
        jax.ShapeDtypeStruct((t, 1024), BF16),
        jax.ShapeDtypeStruct((t, 256), F32),
        jax.ShapeDtypeStruct((t, 256), F32),
        jax.ShapeDtypeStruct((t, 256), F32),
        jax.ShapeDtypeStruct((t, 128), BF16),
        jax.ShapeDtypeStruct((t, 128), BF16),
        jax.ShapeDtypeStruct((NSA_KV_HEADS, _V_ROWS, t), BF16),
        jax.ShapeDtypeStruct((NSA_KV_HEADS, _V_ROWS, t), BF16),
        jax.ShapeDtypeStruct((t, 128), F32),
        jax.ShapeDtypeStruct((t, 512), F32),
        jax.ShapeDtypeStruct((t, 256), F32),
        jax.ShapeDtypeStruct((t, 256), F32),
        jax.ShapeDtypeStruct((t, 512), F32),
        jax.ShapeDtypeStruct((t, 512), F32),
        jax.ShapeDtypeStruct((t // tm, 8, 128), F32),
    )
    vt_spec = pl.BlockSpec((NSA_KV_HEADS, _V_ROWS, tm), lambda i: (0, 0, i))
    out_specs = (row(1024), row(256), row(256), row(256), row(128), row(128), vt_spec, vt_spec,
                 row(128), row(512), row(256), row(256), row(512), row(512),
                 pl.BlockSpec((1, 8, 128), lambda i: (i, 0, 0)))
    return pl.pallas_call(
        _even_in_kernel,
        grid=(t // tm,),
        in_specs=[row(D_MODEL), _full((1, D_MODEL)), _full((D_MODEL, _EVEN_COLS)),
                  row(128), row(128), _full((1, 128)), _full((3, 128))],
        out_specs=out_specs,
        out_shape=out_shape,
        compiler_params=_cparams(("parallel",)),
        name="even_in",
    )(x, norm_g, w_e, cos, sin, gq, gk)


def _compress_kernel(pt_ref, *refs, pp):
    del pt_ref
    page_refs = refs[:pp]
    w_ref, oe_ref, oo_ref = refs[pp:]
    w = w_ref[...]
    for p in range(pp):
        x = page_refs[p][0]
        s = jnp.sum(x.reshape(PAGE_SIZE // CMP_BLOCK, CMP_BLOCK, 256) * w[None], axis=1)
        oe_ref[0, 2 * p:2 * p + 1, :] = s[0:1]
        oo_ref[0, 2 * p:2 * p + 1, :] = s[1:2]
        oe_ref[0, 2 * p + 1:2 * p + 2, :] = s[2:3]
        oo_ref[0, 2 * p + 1:2 * p + 2, :] = s[3:4]


def _compress(pool, page_table, w2, pp):
    nb, n_pages = page_table.shape

    def page_spec(p):
        return pl.BlockSpec((1, PAGE_SIZE, 256), lambda b, g, pt: (pt[b * n_pages + g * pp + p], 0, 0))

    out_spec = pl.BlockSpec((1, 2 * pp, 256), lambda b, g, pt: (b, g, 0))
    grid_spec = pltpu.PrefetchScalarGridSpec(
        num_scalar_prefetch=1,
        grid=(nb, n_pages // pp),
        in_specs=[page_spec(p) for p in range(pp)] + [pl.BlockSpec((CMP_BLOCK, 256), lambda b, g, pt: (0, 0))],
        out_specs=(out_spec, out_spec),
    )
    shp = jax.ShapeDtypeStruct((nb, 2 * n_pages, 256), F32)
    return pl.pallas_call(
        functools.partial(_compress_kernel, pp=pp),
        grid_spec=grid_spec,
        out_shape=(shp, shp),
        compiler_params=_cparams(("parallel", "parallel")),
        name="compress",
    )(page_table.reshape(-1), *([pool] * pp), w2)


def _cmp_lane(w):
    return (w % 2) * (LANES // 2) + w // 2


def _compress_t_kernel(pt_ref, *refs, pp):
    del pt_ref
    page_refs = refs[:pp]
    w_ref, g_ref, o_ref = refs[pp:]
    w = w_ref[...]
    prods = [(page_refs[p][0] * w).astype(BF16) for p in range(pp)]
    o_ref[0] = _dot(jnp.concatenate(prods, axis=1), g_ref[...])


def _compress_t(pool_t, page_table, w_pos, pp):
    nb, n_pages = page_table.shape
    per_page = PAGE_SIZE // CMP_BLOCK
    n_out = pp * per_page
    assert n_out % LANES == 0
    wf = jnp.tile(jnp.broadcast_to(w_pos.transpose(0, 2, 1)[:, :, None, :],
                                   (2, NSA_KV_HEADS, HEAD_DIM, CMP_BLOCK)).reshape(256, CMP_BLOCK), (1, per_page))
    w = jnp.arange(pp * PAGE_SIZE) // CMP_BLOCK
    gm = (((w // LANES) * LANES + _cmp_lane(w % LANES))[:, None] == jnp.arange(n_out)[None, :]).astype(BF16)

    def page_spec(p):
        return pl.BlockSpec((1, 256, PAGE_SIZE), lambda b, g, pt: (pt[b * n_pages + g * pp + p], 0, 0))

    grid_spec = pltpu.PrefetchScalarGridSpec(
        num_scalar_prefetch=1,
        grid=(nb, n_pages // pp),
        in_specs=[page_spec(p) for p in range(pp)]
                 + [pl.BlockSpec((256, PAGE_SIZE), lambda b, g, pt: (0, 0)),
                    pl.BlockSpec((pp * PAGE_SIZE, n_out), lambda b, g, pt: (0, 0))],
        out_specs=pl.BlockSpec((1, 256, n_out), lambda b, g, pt: (b, 0, g)),
    )
    return pl.pallas_call(
        functools.partial(_compress_t_kernel, pp=pp),
        grid_spec=grid_spec,
        out_shape=jax.ShapeDtypeStruct((nb, 256, per_page * n_pages), F32),
        compiler_params=_cparams(("parallel", "parallel")),
        name="compress_sample",
    )(page_table.reshape(-1), *([pool_t] * pp), wf, gm)


def _stack_heads(q, k):
    return jnp.concatenate([q[:, (NSA_GROUP * k + g) * LANES:(NSA_GROUP * k + g + 1) * LANES]
                            for g in range(NSA_GROUP)], axis=0)


def _store_heads(o_ref, og, k, tq):
    for j in range(2):
        blk = jnp.concatenate([og[2 * j], og[2 * j + 1]], axis=0)
        o_ref[:, (2 * k + j) * LANES:(2 * k + j + 1) * LANES] = blk.T


_CMP_VARIANTS = 4


def _cmp_topk_p_kernel(q_ref, kc_ref, vtc_ref, g_ref, o_ref, sel_ref, *, tq, n_half, nq):
    i = pl.program_id(0)
    qpos = i * tq + lax.broadcasted_iota(jnp.int32, (1, tq), 1)
    sb = lax.broadcasted_iota(jnp.int32, (n_half, 1), 0)
    sbf = sb.astype(F32)
    forced = (sb == qpos // SEL_BLOCK) | (sb == 0)
    future = sb * SEL_BLOCK > qpos
    q = q_ref[...]
    gates = g_ref[...]

    def attend(nh):
        r = lax.broadcasted_iota(jnp.int32, (2 * nh, 1), 0)
        blk = jnp.where(r < nh, 2 * r, 2 * (r - nh) + 1)
        mask = ((blk + 1) * CMP_BLOCK - 1) <= qpos
        kc = jnp.concatenate([kc_ref[0:nh, :], kc_ref[n_half:n_half + nh, :]], axis=0)
        scores = []
        for k in range(NSA_KV_HEADS):
            s = _nt_dot(kc, _stack_heads(q, k))
            imp = jnp.zeros((2 * nh, tq), F32)
            ps = []
            for g in range(NSA_GROUP):
                sg = jnp.where(mask, s[:, g * tq:(g + 1) * tq], NEG_INF)
                e = jnp.exp2(sg - jnp.max(sg, axis=0, keepdims=True))
                p = jnp.where(mask, e / jnp.sum(e, axis=0, keepdims=True), 0.0)
                imp = imp + p
                ps.append(p.astype(BF16))
            pt = jnp.concatenate(ps, axis=1)
            vk = slice(HEAD_DIM * k, HEAD_DIM * (k + 1))
            ot = (_dot(vtc_ref[vk, 0:nh], pt[0:nh]) + _dot(vtc_ref[vk, n_half:n_half + nh], pt[nh:2 * nh]))
            og = [ot[:, g * tq:(g + 1) * tq] * gates[NSA_GROUP * k + g:NSA_GROUP * k + g + 1, :]
                  for g in range(NSA_GROUP)]
            _store_heads(o_ref, og, k, tq)
            scores.append(jnp.where(forced[:nh], -jnp.inf, jnp.where(future[:nh], -FORCE, imp[:nh] + imp[nh:])))

        idx = sbf[:nh]

        def pick(rnd, scs):
            out = []
            for sc in scs:
                mx = jnp.max(sc, axis=0, keepdims=True)
                first = jnp.min(jnp.where(sc == mx, idx, 1e9), axis=0, keepdims=True)
                out.append(jnp.where((idx == first) & (rnd < need), -jnp.inf, sc))
            return tuple(out)

        scs = lax.fori_loop(0, n_rounds, pick, tuple(scores))
        for k in range(NSA_KV_HEADS):
            sel_ref[0, k, 0:nh, :] = jnp.where(scs[k] == -jnp.inf, 0.0, NEG_INF)
            if nh < n_half:
                sel_ref[0, k, nh:n_half, :] = jnp.full((n_half - nh, tq), NEG_INF, F32)

    n_forced = 1 + (qpos >= SEL_BLOCK).astype(jnp.int32)
    need = min(N_SELECT, n_half) - n_forced
    n_rounds = min(N_SELECT, n_half) - 1 - (i > 0).astype(jnp.int32)
    per_var = nq // _CMP_VARIANTS
    for c in range(_CMP_VARIANTS):
        pl.when(i // per_var == c)(functools.partial(attend, (c + 1) * n_half // _CMP_VARIANTS))


def _cmp_topk_p(q, kc, vtc, gates0, tq):
    t = q.shape[0]
    n_half = kc.shape[0] // 2
    nq = t // tq
    assert nq % _CMP_VARIANTS == 0 and (n_half // _CMP_VARIANTS) % 8 == 0
    assert n_half * SEL_BLOCK >= t
    return pl.pallas_call(
        functools.partial(_cmp_topk_p_kernel, tq=tq, n_half=n_half, nq=nq),
        grid=(nq,),
        in_specs=[pl.BlockSpec((tq, 1024), lambda i: (i, 0)), _full(kc.shape), _full(vtc.shape),
                  pl.BlockSpec((NSA_HEADS, tq), lambda i: (0, i))],
        out_specs=(pl.BlockSpec((tq, 512), lambda i: (i, 0)),
                   pl.BlockSpec((1, NSA_KV_HEADS, n_half, tq), lambda i: (i, 0, 0, 0))),
        out_shape=(jax.ShapeDtypeStruct((t, 512), F32),
                   jax.ShapeDtypeStruct((nq, NSA_KV_HEADS, n_half, tq), F32)),
        compiler_params=_cparams(("parallel",)),
        name="cmp_topk_prompt",
    )(q, kc, vtc, gates0)


_V_ROWS = HEAD_DIM + 16
_X_ROWS = 16
_SAFE_BOUND = 60.0


def _sel_p_kernel(q_ref, k2_ref, vt_ref, kn_ref, ex_ref, sel_ref, g_ref, oc_ref, ow_ref, sz_ref, o_ref,
                  acc_ref, m_ref, l_ref, s_ref, p_ref, a_ref, qt_ref, os_ref, *, tq, tk, ns):
    i = pl.program_id(0)
    qpos = i * tq + lax.broadcasted_iota(jnp.int32, (1, tq), 1)
    q = q_ref[...]
    qts = [_stack_heads(q, k).astype(F32).T for k in range(NSA_KV_HEADS)]
    nblk = tk // SEL_BLOCK
    assert 2 * nblk == 8 and tk % tq == 0 and ns % 2 == 0 and (k2_ref.shape[0] // tk) % ns == 0
    n_half = sel_ref.shape[2]
    kiota = lax.broadcasted_iota(jnp.int32, (tk, 1), 0)
    n_tiles = (i * tq + tq + tk - 1) // tk
    n_steps = (n_tiles + ns - 1) // ns
    max_tile = k2_ref.shape[0] // tk - 1
    gates = g_ref[...]

    def finish(numer, denom):
        for k in range(NSA_KV_HEADS):
            ot = numer(k) / denom(k)
            og = [ot[:, g * tq:(g + 1) * tq] * gates[NSA_GROUP * k + g:NSA_GROUP * k + g + 1, :]
                  for g in range(NSA_GROUP)]
            _store_heads(os_ref, og, k, tq)
        o_ref[...] = ((oc_ref[...] + os_ref[...] + ow_ref[...]) * sz_ref[...]).astype(BF16)

    kn = jnp.max(kn_ref[...].reshape(-1, LANES), axis=0, keepdims=True)
    bounds = []
    for k in range(NSA_KV_HEADS):
        kmax2 = jnp.max(kn[:, HEAD_DIM * k:HEAD_DIM * (k + 1)], axis=1, keepdims=True)
        qn2 = jnp.sum(qts[k] * qts[k], axis=0, keepdims=True)
        bounds.append(jnp.sqrt(qn2 * kmax2) * 1.02 + 1.0)
    safe = jnp.maximum(jnp.max(bounds[0]), jnp.max(bounds[1])) < _SAFE_BOUND
    lane = lax.broadcasted_iota(jnp.int32, (1, LANES), 1)
    row8 = lax.broadcasted_iota(jnp.int32, (_X_ROWS - 8, 1), 0)
    tails = [jnp.where(row8 == 0, -b, 0.0) for b in bounds]

    def f_scores(j, s1):
        slot = s1 % ns
        kt = jnp.minimum(ns * j + s1, max_tile)
        ktile = k2_ref[pl.ds(pl.multiple_of(kt * tk, tk), tk), :]
        grp = pl.multiple_of(jnp.minimum((j * (ns // 2) + s1 // 2) * 8, n_half - 8), 8)
        for k in range(NSA_KV_HEADS):
            r0 = HEAD_DIM * (1 - k)
            rows8 = jnp.concatenate([sel_ref[0, k, pl.ds(grp, 8), :]] * NSA_GROUP, axis=1)
            qt_ref[slot, k, r0:r0 + _X_ROWS, :] = jnp.concatenate([rows8, tails[k]], axis=0).astype(BF16)
            own = (lane >= HEAD_DIM * k) & (lane < HEAD_DIM * (k + 1))
            lhs = jnp.where(own, ktile, ex_ref[s1 % 2, k])
            s_ref[slot, k] = _dot(lhs, qt_ref[slot, k])

    def f_values(kt, slot):
        kt = jnp.clip(kt, 0, max_tile)
        for k in range(NSA_KV_HEADS):
            acc_ref[k] = acc_ref[k] + _dot(vt_ref[k, :, pl.ds(pl.multiple_of(kt * tk, tk), tk)], p_ref[slot, k])

    def f_softmax(j, slot, masked):
        visible = (ns * j + slot) * tk + kiota <= qpos
        for k in range(NSA_KV_HEADS):
            for g in range(NSA_GROUP):
                cs = slice(g * tq, (g + 1) * tq)
                p = jnp.exp2(s_ref[slot, k, :, cs])
                if masked:
                    p = jnp.where(visible, p, 0.0)
                p_ref[slot, k, :, cs] = p.astype(BF16)

    def f_step(j, masked):
        for s in range(ns):
            f_values(ns * j + s - 1, (s - 1) % ns)
            f_softmax(j, s, masked)
            f_scores(j, s + 1)

    def fast_path():
        for k in range(NSA_KV_HEADS):
            w = qts[k].astype(BF16)
            for slot in range(ns):
                qt_ref[slot, k] = w
        acc_ref[...] = jnp.zeros(acc_ref.shape, F32)
        p_ref[...] = jnp.zeros(p_ref.shape, BF16)
        f_scores(0, 0)

        def body(j, carry):
            f_step(j, False)
            return carry

        lax.fori_loop(0, n_steps - 1, body, 0)
        f_step(n_steps - 1, True)
        f_values(ns * n_steps - 1, ns - 1)
        finish(lambda k: acc_ref[k, 0:HEAD_DIM], lambda k: acc_ref[k, HEAD_DIM:HEAD_DIM + 1])

    def scores(kt, slot):
        kt = jnp.minimum(kt, max_tile)
        ktile = k2_ref[pl.ds(pl.multiple_of(kt * tk, tk), tk), :]
        for k in range(NSA_KV_HEADS):
            s_ref[slot, k] = _dot(ktile, qt_ref[0, k])

    def values(kt, slot):
        kt = jnp.clip(kt, 0, max_tile)
        for k in range(NSA_KV_HEADS):
            vtile = vt_ref[k, 0:HEAD_DIM, pl.ds(pl.multiple_of(kt * tk, tk), tk)]
            acc_ref[k, 0:HEAD_DIM] = acc_ref[k, 0:HEAD_DIM] * a_ref[slot, k] + _dot(vtile, p_ref[slot, k])

    def softmax(j, slot):
        visible = (ns * j + slot) * tk + kiota <= qpos
        grp = pl.multiple_of((j * (ns // 2) + slot // 2) * 8, 8)
        for k in range(NSA_KV_HEADS):
            rows = sel_ref[0, k, pl.ds(grp, 8), :][(slot % 2) * nblk:(slot % 2 + 1) * nblk]
            bias = jnp.concatenate([jnp.broadcast_to(rows[r:r + 1, :], (SEL_BLOCK, tq)) for r in range(nblk)], axis=0)
            bias = jnp.where(visible, bias, NEG_INF)
            for g in range(NSA_GROUP):
                cs = slice(g * tq, (g + 1) * tq)
                sg = s_ref[slot, k, :, cs] + bias
                m_old = m_ref[k, :, cs]
                m_new = jnp.maximum(m_old, jnp.max(sg, axis=0, keepdims=True))
                alpha = jnp.exp2(m_old - m_new)
                p = jnp.exp2(sg - m_new)
                l_ref[k, :, cs] = alpha * l_ref[k, :, cs] + jnp.sum(p, axis=0, keepdims=True)
                m_ref[k, :, cs] = m_new
                a_ref[slot, k, :, cs] = alpha
                p_ref[slot, k, :, cs] = p.astype(BF16)

    def body(j, carry):
        for s in range(ns):
            scores(ns * j + s + 1, (s + 1) % ns)
            values(ns * j + s - 1, (s - 1) % ns)
            softmax(j, s)
        return carry

    def exact_path():
        for k in range(NSA_KV_HEADS):
            qt_ref[0, k] = qts[k].astype(BF16)
        m_ref[...] = jnp.full(m_ref.shape, NEG_INF, F32)
        l_ref[...] = jnp.zeros(l_ref.shape, F32)
        acc_ref[...] = jnp.zeros(acc_ref.shape, F32)
        p_ref[...] = jnp.zeros(p_ref.shape, BF16)
        a_ref[...] = jnp.ones(a_ref.shape, F32)
        scores(0, 0)
        lax.fori_loop(0, n_steps, body, 0)
        values(ns * n_steps - 1, ns - 1)
        finish(lambda k: acc_ref[k, 0:HEAD_DIM], lambda k: l_ref[k])

    lax.cond(safe, fast_path, exact_path)


def _sel_extra_columns(tk):
    row = np.arange(tk)[:, None]
    lane = np.arange(LANES)[None, :]
    ex = np.zeros((2, NSA_KV_HEADS, tk, LANES), np.float32)
    for par in range(2):
        for k in range(NSA_KV_HEADS):
            r0 = HEAD_DIM * (1 - k)
            ex[par, k] = (lane == r0 + par * (tk // SEL_BLOCK) + row // SEL_BLOCK) | (lane == r0 + 8)
    return jnp.asarray(ex, BF16)


def _sel_p(q, k2, vt, kn, sel, gates1, oc, ow, sz, tq, tk, ns):
    t = q.shape[0]
    n_half = sel.shape[2]
    ex = _sel_extra_columns(tk)
    return pl.pallas_call(
        functools.partial(_sel_p_kernel, tq=tq, tk=tk, ns=ns),
        grid=(t // tq,),
        in_specs=[pl.BlockSpec((tq, 1024), lambda i: (i, 0)), _full(k2.shape), _full(vt.shape), _full(kn.shape),
                  _full(ex.shape),
                  pl.BlockSpec((1, NSA_KV_HEADS, n_half, tq), lambda i: (i, 0, 0, 0)),
                  pl.BlockSpec((NSA_HEADS, tq), lambda i: (0, i))] + [pl.BlockSpec((tq, 512), lambda i: (i, 0))] * 3,
        out_specs=pl.BlockSpec((tq, 512), lambda i: (i, 0)),
        out_shape=jax.ShapeDtypeStruct((t, 512), BF16),
        scratch_shapes=[pltpu.VMEM((NSA_KV_HEADS, _V_ROWS, NSA_GROUP * tq), F32),
                        pltpu.VMEM((NSA_KV_HEADS, 1, NSA_GROUP * tq), F32),
                        pltpu.VMEM((NSA_KV_HEADS, 1, NSA_GROUP * tq), F32),
                        pltpu.VMEM((ns, NSA_KV_HEADS, tk, NSA_GROUP * tq), F32),
                        pltpu.VMEM((ns, NSA_KV_HEADS, tk, NSA_GROUP * tq), BF16),
                        pltpu.VMEM((ns, NSA_KV_HEADS, 1, NSA_GROUP * tq), F32),
                        pltpu.VMEM((ns, NSA_KV_HEADS, LANES, NSA_GROUP * tq), BF16),
                        pltpu.VMEM((tq, 512), F32)],
        compiler_params=_cparams(("parallel",)),
        name="sel_prompt",
    )(q, k2, vt, kn, ex, sel, gates1, oc, ow, sz)


def _win_p_kernel(q_ref, k2_ref, vt_ref, g_ref, o_ref, *, tq, nsub):
    nk = WINDOW + tq
    lane = lax.broadcasted_iota(jnp.int32, (1, LANES), 1)
    rowx = lax.broadcasted_iota(jnp.int32, (LANES, 1), 0)
    blocks = []
    for sub in range(nsub):
        i = pl.program_id(0) * nsub + sub
        qpos = i * tq + lax.broadcasted_iota(jnp.int32, (1, tq), 1)
        start = pl.multiple_of(jnp.maximum(i * tq - WINDOW, 0), tq)
        kpos = start + lax.broadcasted_iota(jnp.int32, (nk, 1), 0)
        ok = (kpos <= qpos) & (kpos > qpos - WINDOW)
        bias = jnp.where(ok, 0.0, NEG_INF)
        q = q_ref[sub * tq:(sub + 1) * tq, :]
        ktile = k2_ref[pl.ds(start, nk), :]
        kf = ktile.astype(F32)
        kmax2 = jnp.max(jnp.sum(kf * kf, axis=1, keepdims=True), axis=0, keepdims=True)
        qts = [_stack_heads(q, k).astype(F32).T for k in range(NSA_KV_HEADS)]
        bounds = [jnp.sqrt(jnp.sum(t * t, axis=0, keepdims=True) * kmax2) * 1.02 + 1.0 for t in qts]
        blocks.append(dict(start=start, bias4=jnp.concatenate([bias] * NSA_GROUP, axis=1), q=q, ktile=ktile,
                           qts=qts, bounds=bounds, gates=g_ref[:, sub * tq:(sub + 1) * tq],
                           out=o_ref.at[sub * tq:(sub + 1) * tq, :]))
    safe = functools.reduce(jnp.maximum, [jnp.max(b) for blk in blocks for b in blk["bounds"]]) < _SAFE_BOUND

    def finish(blk, k, ot):
        og = [ot[:, g * tq:(g + 1) * tq] * blk["gates"][NSA_GROUP * k + g:NSA_GROUP * k + g + 1, :]
              for g in range(NSA_GROUP)]
        _store_heads(blk["out"], og, k, tq)

    def fast_path():
        for blk in blocks:
            for k in range(NSA_KV_HEADS):
                r0 = HEAD_DIM * (1 - k)
                own = (lane >= HEAD_DIM * k) & (lane < HEAD_DIM * (k + 1))
                lhs = jnp.where(own, blk["ktile"], jnp.where(lane == r0, 1.0, 0.0).astype(BF16))
                w = jnp.where(rowx == r0, -blk["bounds"][k], blk["qts"][k]).astype(BF16)
                p = jnp.exp2(_dot(lhs, w) + blk["bias4"]).astype(BF16)
                acc = _dot(vt_ref[k, :, pl.ds(blk["start"], nk)], p)
                finish(blk, k, acc[0:HEAD_DIM] / acc[HEAD_DIM:HEAD_DIM + 1])

    def exact_path():
        for blk in blocks:
            for k in range(NSA_KV_HEADS):
                s = _nt_dot(blk["ktile"], _stack_heads(blk["q"], k)) + blk["bias4"]
                p = jnp.exp2(s - jnp.max(s, axis=0, keepdims=True))
                l = jnp.sum(p, axis=0, keepdims=True)
                finish(blk, k, _dot(vt_ref[k, 0:HEAD_DIM, pl.ds(blk["start"], nk)], p.astype(BF16)) / l)

    lax.cond(safe, fast_path, exact_path)


def _win_p(q, k2, vt, gates2, tq, nsub):
    t = q.shape[0]
    tq, tq1 = tq * nsub, tq
    return pl.pallas_call(
        functools.partial(_win_p_kernel, tq=tq1, nsub=nsub),
        grid=(t // tq,),
        in_specs=[pl.BlockSpec((tq, 1024), lambda i: (i, 0)), _full(k2.shape), _full(vt.shape),
                  pl.BlockSpec((NSA_HEADS, tq), lambda i: (0, i))],
        out_specs=pl.BlockSpec((tq, 512), lambda i: (i, 0)),
        out_shape=jax.ShapeDtypeStruct((t, 512), F32),
        compiler_params=_cparams(("parallel",)),
        name="win_prompt",
    )(q, k2, vt, gates2)


def _flag_block(lane):
    r = lane % LANES
    return (lane // LANES) * (LANES // 2) + r, r < LANES // 2


def _cmp_topk_s_kernel(q_ref, ct_ref, g_ref, o_ref, sel_ref, *, tq, past, n_cmp, n_sb, n_cols):
    bs, nrow, _ = q_ref.shape
    nr8 = NSA_KV_HEADS * tq
    qpos = past + lax.broadcasted_iota(jnp.int32, (nrow, 1), 0) % tq
    col = lax.broadcasted_iota(jnp.int32, (1, n_cmp), 1)
    r = col % LANES
    blk = (col // LANES) * LANES + 2 * (r % (LANES // 2)) + r // (LANES // 2)
    mask = ((blk + 1) * CMP_BLOCK - 1) <= qpos
    imps = []
    for j in range(bs):
        qs = q_ref[j]
        ct = ct_ref[j]
        sm = jnp.where(mask, _dot(qs, ct[0:LANES].astype(BF16)), NEG_INF)
        e = jnp.exp2(sm - jnp.max(sm, axis=-1, keepdims=True))
        p = jnp.where(mask, e / jnp.sum(e, axis=-1, keepdims=True), 0.0)
        o_ref[j] = _nt_dot(p.astype(BF16), ct[LANES:2 * LANES].astype(BF16)) * g_ref[j][:, 0:1]
        imp = p[0:nr8]
        for g in range(1, NSA_GROUP):
            imp = imp + p[g * nr8:(g + 1) * nr8]
        pairs = []
        for c in range(n_cmp // LANES):
            ch = imp[:, c * LANES:(c + 1) * LANES]
            pairs.append(ch + pltpu.roll(ch, LANES // 2, 1))
        imps.append(jnp.concatenate(pairs + [jnp.zeros((nr8, n_cols - n_cmp), F32)], axis=1))
    imp = jnp.concatenate(imps, axis=0)
    qpos8 = past + lax.broadcasted_iota(jnp.int32, (bs * nr8, 1), 0) % tq
    sb, used = _flag_block(lax.broadcasted_iota(jnp.int32, (1, n_cols), 1))
    sbf = sb.astype(F32)
    score = jnp.where((sb == qpos8 // SEL_BLOCK) | (sb == 0), FORCE, imp)
    score = jnp.where(sb * SEL_BLOCK > qpos8, -FORCE, score)
    valid = used & (sb < n_sb)
    score = jnp.where(valid, score, -jnp.inf)

    def pick(_, sc):
        mx = jnp.max(sc, axis=-1, keepdims=True)
        first = jnp.min(jnp.where(sc == mx, sbf, 1e9), axis=-1, keepdims=True)
        return jnp.where(sbf == first, -jnp.inf, sc)

    sc = lax.fori_loop(0, min(N_SELECT, n_sb), pick, score)
    chosen = jnp.where(valid & (sc == -jnp.inf), 1.0, 0.0)
    for j in range(bs):
        sel_ref[j] = chosen[j * nr8:(j + 1) * nr8]


def _cmp_topk_s(qs, cmp_t, gates, tq, past, bs):
    nb, nrow, _ = qs.shape
    n_cmp = cmp_t.shape[2]
    n_sb = -(-(past + tq) // SEL_BLOCK)
    n_cols = n_cmp + LANES
    assert n_sb <= (n_cols // LANES) * (LANES // 2) and nb % bs == 0
    b3 = lambda a: pl.BlockSpec((bs,) + a.shape[1:], lambda b: (b, 0, 0))
    return pl.pallas_call(
        functools.partial(_cmp_topk_s_kernel, tq=tq, past=past, n_cmp=n_cmp, n_sb=n_sb, n_cols=n_cols),
        grid=(nb // bs,),
        in_specs=[b3(qs), b3(cmp_t), b3(gates)],
        out_specs=(pl.BlockSpec((bs, nrow, 128), lambda b: (b, 0, 0)),
                   pl.BlockSpec((bs, NSA_KV_HEADS * tq, n_cols), lambda b: (b, 0, 0))),
        out_shape=(jax.ShapeDtypeStruct((nb, nrow, 128), F32),
                   jax.ShapeDtypeStruct((nb, NSA_KV_HEADS * tq, n_cols), F32)),
        compiler_params=_cparams(("parallel",)),
        name="cmp_topk_sample",
    )(qs, cmp_t, gates)


def _selwin_s_kernel(pt_ref, q_ref, sel_ref, g_ref, *refs, pp, tq, past):
    del pt_ref
    page_refs = refs[:pp]
    newsel_ref, winbuf_ref, newwin_ref, em_ref, osel_ref, owin_ref, m_ref, l_ref, acc_ref = refs[pp:]
    g = pl.program_id(1)
    ng = pl.num_programs(1)
    qs = q_ref[0]
    nrow = qs.shape[0]
    flags = sel_ref[0]
    n_cols = flags.shape[1]
    qpos = past + lax.broadcasted_iota(jnp.int32, (nrow, 1), 0) % tq

    @pl.when(g == 0)
    def _():
        m_ref[...] = jnp.full(m_ref.shape, NEG_INF, F32)
        l_ref[...] = jnp.zeros(l_ref.shape, F32)
        acc_ref[...] = jnp.zeros(acc_ref.shape, F32)

    def update(s, ok, vts):
        sm = s if ok is None else jnp.where(ok, s, NEG_INF)
        m_old = m_ref[...]
        m_new = jnp.maximum(m_old, jnp.max(sm, axis=-1, keepdims=True))
        alpha = jnp.exp2(m_old - m_new)
        p = jnp.exp2(sm - m_new)
        if ok is not None:
            p = jnp.where(ok, p, 0.0)
        l_ref[...] = alpha * l_ref[...] + jnp.sum(p, axis=-1, keepdims=True)
        acc_ref[...] = alpha * acc_ref[...] + _nt_dot(p.astype(BF16), vts)
        m_ref[...] = m_new

    def expand(first_block, n):
        ncol = lax.broadcasted_iota(jnp.int32, (1, n), 1)
        sbl, used = _flag_block(lax.broadcasted_iota(jnp.int32, (n_cols, 1), 0))
        emat = ((sbl == first_block + ncol // SEL_BLOCK) & used).astype(BF16)
        mf = _dot(flags.astype(BF16), emat)
        return jnp.concatenate([mf] * NSA_GROUP, axis=0) > 0.5

    n = pp * PAGE_SIZE
    pages = [page_refs[p][0] for p in range(pp)]
    kt_all = jnp.concatenate([pg[0:LANES].astype(BF16) for pg in pages], axis=1)
    vt_all = jnp.concatenate([pg[LANES:2 * LANES].astype(BF16) for pg in pages], axis=1)
    nsb = n // SEL_BLOCK
    spc = (LANES // 2) // nsb
    chunk = sel_ref[0, :, pl.ds(pl.multiple_of((g // spc) * LANES, LANES), LANES)]
    cb = jnp.where(chunk > 0.5, 0.0, NEG_INF)
    if spc == 2:
        cb = jnp.where(g % 2 == 0, cb, pltpu.roll(cb, LANES - nsb, 1))
    cb = jnp.where(lax.broadcasted_iota(jnp.int32, (1, LANES), 1) < nsb, cb, 0.0)
    lhs = jnp.concatenate([qs, jnp.concatenate([cb] * NSA_GROUP, axis=0).astype(BF16)], axis=1)
    update(_dot(lhs, jnp.concatenate([kt_all, em_ref[...]], axis=0)), None, vt_all)

    @pl.when(g == ng - 1)
    def _():
        new = newsel_ref[0]
        kpos_n = past + lax.broadcasted_iota(jnp.int32, (1, PAGE_SIZE), 1)
        update(_dot(qs, new[0:LANES].astype(BF16)),
               expand(past // SEL_BLOCK, PAGE_SIZE) & (kpos_n <= qpos), new[LANES:2 * LANES].astype(BF16))
        gates = g_ref[0]
        osel_ref[0] = acc_ref[...] / l_ref[...] * gates[:, 1:2]

        wb = winbuf_ref[0]
        nw = newwin_ref[0]
        n_buf = wb.shape[1]
        kw = jnp.concatenate([wb[0:LANES], nw[0:LANES]], axis=1).astype(BF16)
        vw = jnp.concatenate([wb[LANES:2 * LANES], nw[LANES:2 * LANES]], axis=1).astype(BF16)
        wpos = past - n_buf + lax.broadcasted_iota(jnp.int32, (1, n_buf + PAGE_SIZE), 1)
        okw = (wpos <= qpos) & (wpos > qpos - WINDOW) & (wpos >= 0)
        sw = jnp.where(okw, _dot(qs, kw), NEG_INF)
        e = jnp.where(okw, jnp.exp2(sw - jnp.max(sw, axis=-1, keepdims=True)), 0.0)
        owin_ref[0] = _nt_dot(e.astype(BF16), vw) / jnp.sum(e, axis=-1, keepdims=True) * gates[:, 2:3]


def _selwin_s(qs, sel, gates, pool, page_table, newsel, winbuf, newwin, tq, past, pp):
    nb, nrow, _ = qs.shape
    n_pages = page_table.shape[1]
    bmap = lambda b, g, pt: (b, 0, 0)
    b3 = lambda a: pl.BlockSpec((1,) + a.shape[1:], bmap)

    def page_spec(p):
        return pl.BlockSpec((1, 256, PAGE_SIZE), lambda b, g, pt: (pt[b * n_pages + g * pp + p], 0, 0))

    n = pp * PAGE_SIZE
    assert (n // SEL_BLOCK) in (LANES // 4, LANES // 2)
    em = jnp.asarray(np.arange(LANES)[:, None] == np.arange(n)[None, :] // SEL_BLOCK, BF16)
    grid_spec = pltpu.PrefetchScalarGridSpec(
        num_scalar_prefetch=1,
        grid=(nb, n_pages // pp),
        in_specs=[b3(qs), b3(sel), b3(gates)] + [page_spec(p) for p in range(pp)]
                 + [b3(newsel), b3(winbuf), b3(newwin), pl.BlockSpec(em.shape, lambda b, g, pt: (0, 0))],
        out_specs=(pl.BlockSpec((1, nrow, 128), bmap), pl.BlockSpec((1, nrow, 128), bmap)),
        scratch_shapes=[pltpu.VMEM((nrow, 1), F32), pltpu.VMEM((nrow, 1), F32), pltpu.VMEM((nrow, 128), F32)],
    )
    shp = jax.ShapeDtypeStruct((nb, nrow, 128), F32)
    return pl.pallas_call(
        functools.partial(_selwin_s_kernel, pp=pp, tq=tq, past=past),
        grid_spec=grid_spec,
        out_shape=(shp, shp),
        compiler_params=_cparams(("parallel", "arbitrary")),
        name="selwin_sample",
    )(page_table.reshape(-1), qs, sel, gates, *([pool] * pp), newsel, winbuf, newwin, em)


def _ret_kernel(rq_ref, rk_ref, rv_ref, sz_ref, s0_ref, dec_ref, qd_ref, kd_ref, gc_ref,
                o_ref, sout_ref, s_ref):
    c = pl.program_id(1)

    @pl.when(c == 0)
    def _():
        s_ref[...] = s0_ref[0]

    lane = lax.broadcasted_iota(jnp.int32, (1, LANES), 1)
    for pr in range(RET_HEADS // 2):
        cs = slice(pr * LANES, (pr + 1) * LANES)
        qc = rq_ref[:, cs]
        kc = rk_ref[:, cs]
        kb = kc.astype(BF16)
        state = s_ref[pr]
        sb = state.astype(BF16)
        new_state = gc_ref[pr] * state
        for hh in range(2):
            h = 2 * pr + hh
            hm = (lane >= RET_DK * hh) & (lane < RET_DK * (hh + 1))
            qm = jnp.where(hm, qc, 0.0)
            a = _nt_dot(qm.astype(BF16), kb) * dec_ref[h]
            v = rv_ref[:, h * RET_DV:(h + 1) * RET_DV]
            vb = v.astype(BF16)
            o = _dot(a.astype(BF16), vb) + _dot((qm * qd_ref[:, cs]).astype(BF16), sb)
            km = jnp.where(hm, kc * kd_ref[:, cs], 0.0)
            new_state = new_state + _dot(km.T.astype(BF16), vb)
            mu = jnp.mean(o, axis=-1, keepdims=True)
            var = jnp.mean(jnp.square(o - mu), axis=-1, keepdims=True)
            o_ref[:, h * RET_DV:(h + 1) * RET_DV] = ((o - mu) * lax.rsqrt(var + EPS)
                                                    * sz_ref[:, h * RET_DV:(h + 1) * RET_DV]).astype(BF16)
        s_ref[pr] = new_state
        sout_ref[0, pr] = new_state


def _retention(rq, rk, rv, szr, s0, c_real, nb, nc, cc):
    log_g = jnp.log1p(-jnp.exp2(-5.0 - jnp.arange(RET_HEADS, dtype=F32)))
    i = jnp.arange(cc, dtype=F32)
    rel = i[:, None] - i[None, :]
    dec = jnp.where(rel >= 0, jnp.exp(log_g[:, None, None] * jnp.maximum(rel, 0.0)), 0.0)
    qd = jnp.repeat(jnp.exp(log_g[None, :] * (i[:, None] + 1.0)), RET_DK, axis=1)
    kd = jnp.repeat(jnp.exp(log_g[None, :] * (c_real - 1.0 - i[:, None])), RET_DK, axis=1)
    gc = jnp.broadcast_to(jnp.repeat(jnp.exp(log_g * c_real), RET_DK).reshape(2, 128, 1), (2, 128, 128))
    row = lambda n: pl.BlockSpec((cc, n), lambda b, c: (b * nc + c, 0))
    st = pl.BlockSpec((1, 2, 128, 128), lambda b, c: (b, 0, 0, 0))
    t = rq.shape[0]
    return pl.pallas_call(
        _ret_kernel,
        grid=(nb, nc),
        in_specs=[row(256), row(256), row(512), row(512), st, _full((RET_HEADS, cc, cc)),
                  _full((cc, 256)), _full((cc, 256)), _full((2, 128, 128))],
        out_specs=(row(512), st),
        out_shape=(jax.ShapeDtypeStruct((t, 512), BF16), jax.ShapeDtypeStruct((nb, 2, 128, 128), F32)),
        scratch_shapes=[pltpu.VMEM((2, 128, 128), F32)],
        compiler_params=_cparams(("parallel", "arbitrary")),
        name="retention",
    )(rq, rk, rv, szr, s0, dec, qd, kd, gc)


def _even_out_kernel(x_ref, *refs):
    mr_ref, w_ref, y_ref = refs[-3:]
    if len(refs) == 4:
        mixed = refs[0][...]
    else:
        oc_ref, os_ref, ow_ref, sz_ref = refs[:4]
        mixed = ((oc_ref[...] + os_ref[...] + ow_ref[...]) * sz_ref[...]).astype(BF16)
    y_ref[...] = (x_ref[...] + _dot(mixed, w_ref[0:512, :]) + _dot(mr_ref[...], w_ref[512:1024, :]))


def _even_out(x, nsa_parts, mret, w_out, tm):
    t = x.shape[0]
    row = lambda n: pl.BlockSpec((tm, n), lambda i: (i, 0))
    return pl.pallas_call(
        _even_out_kernel,
        grid=(t // tm,),
        in_specs=[row(D_MODEL)] + [row(512)] * (len(nsa_parts) + 1) + [_full((1024, D_MODEL))],
        out_specs=row(D_MODEL),
        out_shape=jax.ShapeDtypeStruct((t, D_MODEL), F32),
        compiler_params=_cparams(("parallel",)),
        name="even_out",
    )(x, *nsa_parts, mret, w_out)


_HALO = 32
_CONV_ROWS = 16


def _rms(x, g):
    return (x * lax.rsqrt(jnp.mean(x * x, axis=-1, keepdims=True) + EPS) * g).astype(BF16)


def _glu(xn, w_ref):
    return _dot(xn, w_ref[:, 0:D_MODEL]) * _sigmoid(_dot(xn, w_ref[:, D_MODEL:2 * D_MODEL]))


def _ln_gate_out(x, c, z, lg_ref, lb_ref, wout_ref):
    mu = jnp.mean(c, axis=-1, keepdims=True)
    var = jnp.mean(jnp.square(c - mu), axis=-1, keepdims=True)
    yn = (c - mu) * lax.rsqrt(var + EPS) * lg_ref[...] + lb_ref[...]
    y = (yn * _sigmoid(yn)) * (z * _sigmoid(z))
    return x + _dot(y.astype(BF16), wout_ref[...])


def _odd_p_kernel(x_ref, xh_ref, g_ref, win_ref, cw_ref, cb_ref, lg_ref, lb_ref, wout_ref,
                  y_ref, ust_ref, u_scr, c_scr, *, tm):
    i = pl.program_id(0)
    x = x_ref[...]
    g = g_ref[...]
    xn = _rms(x, g)
    u = _glu(xn, win_ref)
    z = _dot(xn, win_ref[:, 2 * D_MODEL:3 * D_MODEL])
    uh = _glu(_rms(xh_ref[...], g), win_ref)
    u_scr[0, 0:_HALO, :] = jnp.where(i > 0, uh, 0.0)
    u_scr[0, _HALO:_HALO + tm, :] = u
    u_scr[0, _HALO + tm:_HALO + tm + 8, :] = jnp.zeros((8, D_MODEL), F32)
    ust_ref[...] = u[tm - _HALO:tm, :]
    for s in range(1, 8):
        u_scr[s, 0:_HALO + tm, :] = u_scr[0, s:s + _HALO + tm, :]
    off = _HALO - (CONV_WIDTH - 1)
    cb = cb_ref[...]
    nv = _CONV_ROWS // 8
    for rc in range(tm // _CONV_ROWS):
        r0 = rc * _CONV_ROWS
        acc = jnp.broadcast_to(cb, (_CONV_ROWS, D_MODEL)).reshape(nv, 8, D_MODEL)
        for j in range(CONV_WIDTH):
            a, s = divmod(j + off, 8)
            rows = u_scr[s, r0 + 8 * a:r0 + 8 * a + _CONV_ROWS, :]
            acc = acc + cw_ref[j][None] * rows.reshape(nv, 8, D_MODEL)
        c_scr[r0:r0 + _CONV_ROWS, :] = acc.reshape(_CONV_ROWS, D_MODEL)
    y_ref[...] = _ln_gate_out(x, c_scr[...], z, lg_ref, lb_ref, wout_ref)


def _odd_p(x, norm_g, w_in, cw, cb, lg, lb, w_out, tm):
    t = x.shape[0]
    row = pl.BlockSpec((tm, D_MODEL), lambda i: (i, 0))
    halo = pl.BlockSpec((_HALO, D_MODEL), lambda i: (jnp.maximum(i * (tm // _HALO) - 1, 0), 0))
    vec = _full((1, D_MODEL))
    return pl.pallas_call(
        functools.partial(_odd_p_kernel, tm=tm),
        grid=(t // tm,),
        in_specs=[row, halo, vec, _full((D_MODEL, 3 * D_MODEL)), _full((CONV_WIDTH, 8, D_MODEL)), vec, vec, vec,
                  _full((D_MODEL, D_MODEL))],
        out_specs=(row, _full((_HALO, D_MODEL))),
        out_shape=(jax.ShapeDtypeStruct((t, D_MODEL), F32), jax.ShapeDtypeStruct((_HALO, D_MODEL), F32)),
        scratch_shapes=[pltpu.VMEM((8, _HALO + tm + 8, D_MODEL), F32), pltpu.VMEM((tm, D_MODEL), F32)],
        compiler_params=_cparams(("arbitrary",)),
        name="odd_prompt",
    )(x, x, norm_g, w_in, cw, cb, lg, lb, w_out)


def _odd_s_kernel(x_ref, st_ref, g_ref, win_ref, cw_ref, cb_ref, lg_ref, lb_ref, wout_ref,
                  y_ref, u_ref, *, nb, tq):
    x = x_ref[...]
    xn = _rms(x, g_ref[...])
    u = _glu(xn, win_ref)
    z = _dot(xn, win_ref[:, 2 * D_MODEL:3 * D_MODEL])
    u_ref[...] = u
    n_state = CONV_WIDTH - 1
    cb = cb_ref[...]
    cs = []
    for t in range(tq):
        acc = jnp.broadcast_to(cb, (nb, D_MODEL))
        for j in range(CONV_WIDTH):
            e = t + j
            slab = st_ref[e * nb:(e + 1) * nb, :] if e < n_state else u[(e - n_state) * nb:(e - n_state + 1) * nb, :]
            acc = acc + cw_ref[j:j + 1, :] * slab
        cs.append(acc)
    y_ref[...] = _ln_gate_out(x, jnp.concatenate(cs, axis=0), z, lg_ref, lb_ref, wout_ref)


def _odd_s(x_tm, state_tm, norm_g, w_in, cw, cb, lg, lb, w_out, nb, tq):
    n = x_tm.shape[0]
    return pl.pallas_call(
        functools.partial(_odd_s_kernel, nb=nb, tq=tq),
        out_shape=(jax.ShapeDtypeStruct((n, D_MODEL), F32), jax.ShapeDtypeStruct((n, D_MODEL), F32)),
        compiler_params=pltpu.CompilerParams(vmem_limit_bytes=VMEM_LIMIT),
        name="odd_sample",
    )(x_tm, state_tm, norm_g, w_in, cw, cb, lg, lb, w_out)


def _rope_tables(pos):
    half = HEAD_DIM // 2
    freq = ROPE_THETA ** (-jnp.arange(half, dtype=F32) / half)
    ang = pos.astype(F32)[:, None] * freq[None, :]
    cos = jnp.cos(ang)
    sin = jnp.sin(ang)
    return jnp.tile(cos, (1, 4)), jnp.tile(jnp.concatenate([-sin, sin], axis=1), (1, 2))


def _even_weights(w_in):
    pq, kc, ks, kw, pg, zn, rq, rk, rv, zr = jnp.split(
        w_in, np.cumsum([512, 256, 256, 256, 24, 512, 256, 256, 512])[:].tolist(), axis=-1)
    zero = jnp.zeros((D_MODEL, HEAD_DIM), w_in.dtype)
    qcols = []
    for h in range(NSA_HEADS):
        wh = pq[:, h * HEAD_DIM:(h + 1) * HEAD_DIM]
        qcols += [wh, zero] if h < NSA_GROUP else [zero, wh]
    gpad = jnp.pad(pg, ((0, 0), (0, LANES - 3 * NSA_HEADS)))
    return jnp.concatenate(qcols + [kc, ks, kw, zn, rq, rk, rv, zr, gpad], axis=-1).astype(BF16)


def _pick_kv_half(o, nb, tq):
    o6 = o.reshape(nb, NSA_GROUP, NSA_KV_HEADS, tq, NSA_KV_HEADS, HEAD_DIM)
    o5 = jnp.stack([o6[:, :, 0, :, 0], o6[:, :, 1, :, 1]], axis=2)
    return o5.transpose(0, 3, 2, 1, 4).reshape(nb * tq, NSA_HEADS * HEAD_DIM)


def kernel(x_prompt, x_sample, cache_cmp_kv, cache_sel_kv, cache_win_kv, state_ret, state_conv, page_table,
           norm_even, w_in_even, q_norm_g, k_norm_g, w_cmp_pos, w_out_even,
           norm_odd, w_in_odd, conv_w, conv_b, ln_g, ln_b, w_out_odd):
    _, seq, _ = x_prompt.shape
    nb, tq, _ = x_sample.shape
    n_pages = page_table.shape[1]
    past = n_pages * PAGE_SIZE
    n_buf = cache_win_kv.shape[2]
    tm = 256
    tq_p = 128

    w_e = _even_weights(w_in_even[0])
    w_oe = w_out_even[0].astype(BF16)
    w_io = w_in_odd[0].astype(BF16)
    w_oo = w_out_odd[0].astype(BF16)
    gq = jnp.tile(q_norm_g[0], 2)[None, :]
    gk = jnp.tile(k_norm_g[0], (1, 2))
    w2 = jnp.repeat(w_cmp_pos[0].transpose(1, 0, 2).reshape(CMP_BLOCK, 4), HEAD_DIM, axis=1)
    ne = norm_even[0][None, :]
    no = norm_odd[0][None, :]

    xp = x_prompt[0]
    cos_p, sin_p = _rope_tables(jnp.arange(seq))
    (q_p, kvc_p, kvs_p, kvw_p, k2s_p, k2w_p, vts_p, vtw_p, gate_p, sz_p,
     rq_p, rk_p, rv_p, szr_p, kn_p) = _even_in(xp, ne, w_e, cos_p, sin_p, gq, gk, tm)
    ident = jnp.arange(seq // PAGE_SIZE, dtype=jnp.int32)[None, :]
    ce, co = _compress(kvc_p.reshape(seq // PAGE_SIZE, PAGE_SIZE, 256), ident, w2, 16)
    cmp_p = jnp.concatenate([ce[0], co[0]], axis=0)
    kc_p = cmp_p[:, :LANES].astype(BF16)
    vtc_p = cmp_p[:, LANES:].T.astype(BF16)
    gates_t = gate_p[:, :3 * NSA_HEADS].reshape(seq, NSA_HEADS, 3).transpose(2, 1, 0)
    oc_p, sel_p = _cmp_topk_p(q_p, kc_p, vtc_p, gates_t[0], tq_p)
    ow_p = _win_p(q_p, k2w_p, vtw_p, gates_t[2], tq_p, 4)
    mix_p = _sel_p(q_p, k2s_p, vts_p, kn_p, sel_p, gates_t[1], oc_p, ow_p, sz_p, tq_p, 256, 8)
    s0_p = jnp.zeros((1, 2, 128, 128), F32)
    ret_c = 4 * RET_CHUNK
    mret_p, s_p = _retention(rq_p, rk_p, rv_p, szr_p, s0_p, ret_c, 1, seq // ret_c, ret_c)
    y1_p = _even_out(xp, (mix_p,), mret_p, w_oe, tm)

    xs = x_sample.reshape(nb * tq, D_MODEL)
    cos_s, sin_s = _rope_tables(jnp.tile(past + jnp.arange(tq), nb))
    (q_s, kvc_s, kvs_s, kvw_s, _, _, _, _, gate_s, sz_s,
     rq_s, rk_s, rv_s, szr_s, _) = _even_in(xs, ne, w_e, cos_s, sin_s, gq, gk, nb * tq)
    rows_t = lambda c: c.transpose(0, 2, 3, 4, 1).reshape(c.shape[0], 256, c.shape[1])
    pages_per_chunk = LANES * CMP_BLOCK // PAGE_SIZE
    cmp_s = _compress_t(rows_t(cache_cmp_kv[0]), page_table, w_cmp_pos[0],
                        2 * pages_per_chunk if n_pages % (2 * pages_per_chunk) == 0 else pages_per_chunk)
    qs_s = (q_s.reshape(nb, tq, NSA_KV_HEADS, NSA_GROUP, LANES).transpose(0, 3, 2, 1, 4)
            .reshape(nb, NSA_HEADS * tq, LANES))
    g_s = (gate_s[:, :3 * NSA_HEADS].reshape(nb, tq, NSA_KV_HEADS, NSA_GROUP, 3).transpose(0, 3, 2, 1, 4)
           .reshape(nb, NSA_HEADS * tq, 3))
    oc_s, sel_s = _cmp_topk_s(qs_s, cmp_s, g_s, tq, past, 8 if nb % 8 == 0 else (4 if nb % 4 == 0 else 1))
    new_t = lambda r: jnp.pad(r.reshape(nb, tq, 256), ((0, 0), (0, PAGE_SIZE - tq), (0, 0))).transpose(0, 2, 1)
    os_s, ow_s = _selwin_s(qs_s, sel_s, g_s, rows_t(cache_sel_kv[0]), page_table,
                           new_t(kvs_s), rows_t(cache_win_kv[0]), new_t(kvw_s), tq, past, 32)
    pad_c = lambda r: jnp.pad(r.reshape(nb, tq, -1), ((0, 0), (0, RET_CHUNK - tq), (0, 0))).reshape(nb * RET_CHUNK, -1)
    mret_s, s_s = _retention(pad_c(rq_s), pad_c(rk_s), pad_c(rv_s), pad_c(szr_s),
                             state_ret[0].reshape(nb, 2, 128, 128), float(tq), nb, 1, RET_CHUNK)
    mret_s = mret_s.reshape(nb, RET_CHUNK, 512)[:, :tq].reshape(nb * tq, 512)
    y1_s = _even_out(xs, (_pick_kv_half(oc_s, nb, tq), _pick_kv_half(os_s, nb, tq), _pick_kv_half(ow_s, nb, tq),
                          sz_s), mret_s, w_oe, nb * tq)

    cb = conv_b[0][None, :]
    lg = ln_g[0][None, :]
    lb = ln_b[0][None, :]
    cw8 = jnp.broadcast_to(conv_w[0][:, None, :], (CONV_WIDTH, 8, D_MODEL))
    y2_p, ust_p = _odd_p(y1_p, no, w_io, cw8, cb, lg, lb, w_oo, tm)
    x_tm = y1_s.reshape(nb, tq, D_MODEL).transpose(1, 0, 2).reshape(tq * nb, D_MODEL)
    st_tm = state_conv[0].transpose(1, 0, 2).reshape((CONV_WIDTH - 1) * nb, D_MODEL)
    y2_tm, u_tm = _odd_s(x_tm, st_tm, no, w_io, conv_w[0], cb, lg, lb, w_oo, nb, tq)
    y2_s = y2_tm.reshape(tq, nb, D_MODEL).transpose(1, 0, 2)
    u_s = u_tm.reshape(tq, nb, D_MODEL).transpose(1, 0, 2)

    rows6 = lambda r, b, t: r.reshape(1, b, t, 2, NSA_KV_HEADS, HEAD_DIM)
    n_win = min(WINDOW, seq)
    win_s = jnp.concatenate([cache_win_kv[0].reshape(nb, n_buf, 256), kvw_s.reshape(nb, tq, 256)], axis=1)[:, -n_buf:]
    conv_s = jnp.concatenate([state_conv[0], u_s], axis=1)[:, -(CONV_WIDTH - 1):]
    return (y2_p[None], y2_s,
            rows6(kvc_p, 1, seq), rows6(kvc_s, nb, tq),
            rows6(kvs_p, 1, seq), rows6(kvs_s, nb, tq),
            rows6(kvw_p[seq - n_win:], 1, n_win), rows6(win_s, nb, n_buf),
            s_p.reshape(1, 1, RET_HEADS, RET_DK, RET_DV), s_s.reshape(1, nb, RET_HEADS, RET_DK, RET_DV),
            ust_p[_HALO - (CONV_WIDTH - 1):][None, None], conv_s[None])
```
